```python
import math, functools
import jax, jax.numpy as jnp
from jax import lax
import numpy as np

D_MODEL = 1024
BATCH = 8
SEQ = 2048
DEPTH = 4
DEC_BATCH = 32
DEC_SEQ = 4
PAST_LEN = 8192
PAGE_SIZE = 128

HEAD_DIM = 64
FOX_HEADS = 8
FOX_W = FOX_HEADS * HEAD_DIM
Q_BLOCK = 128
FOX_BIAS_INIT = 3.0
RW_HEADS = 8
RW_W = RW_HEADS * HEAD_DIM
RW_LORA_W = 64
RW_LORA_A = 64
RW_LORA_G = 128
RW_COLS = 3 * RW_W + RW_LORA_W + RW_LORA_A + RW_LORA_G
RW_GN_EPS = 64e-5
GD_HEADS = 4
GD_DK = 128
GD_DV = 128
GD_W = GD_HEADS * GD_DK
GD_QKV = 2 * GD_W + GD_HEADS * GD_DV
GD_CONV = 4
GD_CHUNK = 64
N_BRANCH = 3
D_FF = 2816
EPS = 1e-6
L2_EPS = 1e-6

FOX_COLS = 3 * FOX_W + FOX_HEADS
GD_COLS = GD_QKV + GD_HEADS * GD_DV + 2 * GD_HEADS
O_RW = FOX_COLS
O_GD = O_RW + RW_COLS
O_GATE = O_GD + GD_COLS
IN_COLS = O_GATE + N_BRANCH * D_MODEL
IN_SPLIT = (FOX_W, 2 * FOX_W, 3 * FOX_W, O_RW, O_GD, O_GD + GD_QKV,
            O_GD + GD_QKV + GD_HEADS * GD_DV, O_GD + GD_QKV + GD_HEADS * GD_DV + GD_HEADS, O_GATE)
RW_SPLIT = (RW_W, 2 * RW_W, 3 * RW_W, 3 * RW_W + RW_LORA_W, 3 * RW_W + RW_LORA_W + RW_LORA_A)
GD_SPLIT = (GD_W, 2 * GD_W)

kernel_name = 'hybrid_fox_rwkv7_gdn_macaron_step'


def rmsnorm(x, g):
    x32 = x.astype(jnp.float32)
    y = x32 * lax.rsqrt(jnp.mean(x32 * x32, axis=-1, keepdims=True) + EPS)
    return y.astype(x.dtype) * g


def l2norm(x):
    x32 = x.astype(jnp.float32)
    return x32 * lax.rsqrt(jnp.sum(x32 * x32, axis=-1, keepdims=True) + L2_EPS)


def swiglu(h, wg, wu, wd):
    return (jax.nn.silu(h @ wg) * (h @ wu)) @ wd


def fox_prompt(q, k, v, logf):
    B, S, H, Dh = q.shape
    nb = S // Q_BLOCK
    c = jnp.swapaxes(jnp.cumsum(logf, axis=1), 1, 2)
    qb = jnp.moveaxis(q.reshape(B, nb, Q_BLOCK, H, Dh), 1, 0)
    cb = jnp.moveaxis(c.reshape(B, H, nb, Q_BLOCK), 2, 0)
    pos_k = jnp.arange(S)
    scale = Dh ** -0.5

    def block(args):
        i, q_i, c_i = args
        s = (jnp.einsum('bqhd,bkhd->bhqk', q_i, k).astype(jnp.float32) * scale
             + c_i[..., :, None] - c[:, :, None, :])
        pos_q = i * Q_BLOCK + jnp.arange(Q_BLOCK)
        s = jnp.where(pos_k[None, :] <= pos_q[:, None], s, -jnp.inf)
        p = jax.nn.softmax(s, axis=-1)
        return jnp.einsum('bhqk,bkhd->bqhd', p.astype(v.dtype), v)

    o = lax.map(block, (jnp.arange(nb), qb, cb))
    return jnp.moveaxis(o, 0, 1).reshape(B, S, H * Dh)


def fox_sample(q, k, v, logf, *, cache_k, cache_v, cache_logf, page_table, layer):
    B, T, H, Dh = q.shape
    kp = cache_k[layer, page_table].reshape(B, -1, H, Dh)
    vp = cache_v[layer, page_table].reshape(B, -1, H, Dh)
    lp = cache_logf[layer, page_table].reshape(B, -1, H).astype(jnp.float32)
    n_past = kp.shape[1]
    suf = jnp.sum(lp, axis=1, keepdims=True) - jnp.cumsum(lp, axis=1)
    cn = jnp.swapaxes(jnp.cumsum(logf, axis=1), 1, 2)
    scale = Dh ** -0.5
    s_past = (jnp.einsum('bthd,bshd->bhts', q, kp).astype(jnp.float32) * scale
              + jnp.swapaxes(suf, 1, 2)[:, :, None, :] + cn[..., :, None])
    s_new = (jnp.einsum('bthd,buhd->bhtu', q, k).astype(jnp.float32) * scale
             + cn[..., :, None] - cn[..., None, :])
    s_new = jnp.where(jnp.tril(jnp.ones((T, T), bool)), s_new, -jnp.inf)
    p = jax.nn.softmax(jnp.concatenate([s_past, s_new], axis=-1), axis=-1)
    o = (jnp.einsum('bhts,bshd->bthd', p[..., :n_past].astype(vp.dtype), vp)
         + jnp.einsum('bhtu,buhd->bthd', p[..., n_past:].astype(v.dtype), v))
    return o.reshape(B, T, H * Dh).astype(v.dtype)


def rwkv7_mix(z, shift0, S0, P):
    B, T, _ = z.shape
    z_prev = jnp.concatenate([shift0[:, None].astype(z.dtype), z[:, :-1]], axis=1)
    zs = z + (z_prev - z) * P['rw_mu']
    r, k, v, wd, ad, gd = jnp.split(zs, RW_SPLIT, axis=-1)
    w_log = -jax.nn.softplus(-(P['rw_w0'] + jnp.tanh(wd) @ P['rw_w_up'])) - 0.5
    decay = jnp.exp(-jnp.exp(w_log.astype(jnp.float32)))
    a = jax.nn.sigmoid(P['rw_a0'] + ad @ P['rw_a_up'])
    g = jax.nn.sigmoid(gd) @ P['rw_g_up']
    kk = k * P['rw_kk']
    k = k * (1.0 + (a - 1.0) * P['rw_ka'])
    hs = lambda t: t.reshape(B, T, RW_HEADS, HEAD_DIM).astype(jnp.float32)
    r, k, v, decay, a = hs(r), hs(k), hs(v), hs(decay), hs(a)
    kk = l2norm(hs(kk))

    def step(S, inp):
        r_t, w_t, k_t, v_t, kk_t, a_t = inp
        sa = jnp.einsum('bhij,bhj->bhi', S, -kk_t)
        S = (S * w_t[:, :, None, :] + sa[..., None] * (kk_t * a_t)[:, :, None, :]
             + v_t[..., None] * k_t[:, :, None, :])
        return S, jnp.einsum('bhij,bhj->bhi', S, r_t)

    tm = lambda t: jnp.swapaxes(t, 0, 1)
    S, y = lax.scan(step, S0.astype(jnp.float32), (tm(r), tm(decay), tm(k), tm(v), tm(kk), tm(a)))
    y = tm(y)
    mu = jnp.mean(y, axis=-1, keepdims=True)
    var = jnp.mean(jnp.square(y - mu), axis=-1, keepdims=True)
    y = ((y - mu) * lax.rsqrt(var + RW_GN_EPS)).reshape(B, T, RW_W) * P['rw_ln_g'] + P['rw_ln_b']
    bonus = (jnp.sum(r * k * P['rw_rk'], axis=-1, keepdims=True) * v).reshape(B, T, RW_W)
    out = ((y + bonus) * g).astype(z.dtype)
    return out, z[:, -1], S.astype(S0.dtype)


def chunk_gated_delta(q, k, v, beta, g, S0):
    B, T, H, Dk = q.shape
    Dv = v.shape[-1]
    C = GD_CHUNK
    n = -(-T // C)
    pad = n * C - T

    def blocks(t):
        t = jnp.pad(t, [(0, 0), (0, pad)] + [(0, 0)] * (t.ndim - 2))
        t = jnp.moveaxis(t, 2, 1)
        return t.reshape(t.shape[:2] + (n, C) + t.shape[3:])

    q, k, v, beta, g = blocks(q * Dk ** -0.5), blocks(k), blocks(v), blocks(beta), blocks(g)
    gc = jnp.cumsum(g, axis=-1)
    incl = jnp.tril(jnp.ones((C, C), bool))
    strict = jnp.tril(jnp.ones((C, C), bool), -1)
    decay = jnp.exp(jnp.where(incl, gc[..., :, None] - gc[..., None, :], -jnp.inf))
    kb = k * beta[..., None]
    M = jnp.where(strict, jnp.einsum('bhncd,bhnsd->bhncs', kb, k) * decay, 0.0)
    eye = jnp.eye(C, dtype=M.dtype)
    Tm = lax.linalg.triangular_solve(M + eye, jnp.broadcast_to(eye, M.shape),
                                     left_side=True, lower=True, unit_diagonal=True)
    u = Tm @ (v * beta[..., None])
    w = Tm @ (kb * jnp.exp(gc)[..., None])
    A = jnp.where(incl, jnp.einsum('bhncd,bhnsd->bhncs', q, k) * decay, 0.0)
    qg = q * jnp.exp(gc)[..., None]
    kg = k * jnp.exp(gc[..., -1:] - gc)[..., None]
    glast = jnp.exp(gc[..., -1])

    def step(S, inp):
        qg_i, kg_i, u_i, w_i, A_i, gl_i = inp
        v_new = u_i - jnp.einsum('bhcd,bhde->bhce', w_i, S)
        o = jnp.einsum('bhcd,bhde->bhce', qg_i, S) + jnp.einsum('bhcs,bhse->bhce', A_i, v_new)
        S = S * gl_i[..., None, None] + jnp.einsum('bhcd,bhce->bhde', kg_i, v_new)
        return S, o

    mv = lambda t: jnp.moveaxis(t, 2, 0)
    S, o = lax.scan(step, S0, (mv(qg), mv(kg), mv(u), mv(w), mv(A), mv(glast)))
    o = jnp.moveaxis(o, 0, 2).reshape(B, H, n * C, Dv)[:, :, :T]
    return jnp.moveaxis(o, 1, 2), S


def gated_deltanet_mix(qkv, zg, zb, za, conv0, S0, P):
    B, T, _ = qkv.shape
    xp = jnp.concatenate([conv0.astype(qkv.dtype), qkv], axis=1)
    conv = xp[:, 0:T] * P['gd_conv'][0]
    for i in range(1, GD_CONV):
        conv = conv + xp[:, i:i + T] * P['gd_conv'][i]
    q, k, v = jnp.split(jax.nn.silu(conv), GD_SPLIT, axis=-1)
    q = l2norm(q.reshape(B, T, GD_HEADS, GD_DK))
    k = l2norm(k.reshape(B, T, GD_HEADS, GD_DK))
    v = v.reshape(B, T, GD_HEADS, GD_DV).astype(jnp.float32)
    beta = jax.nn.sigmoid(zb.astype(jnp.float32))
    g = -jnp.exp(P['gd_a_log'].astype(jnp.float32)) * jax.nn.softplus(za.astype(jnp.float32) + P['gd_dt_bias'])
    o, S = chunk_gated_delta(q, k, v, beta, g, S0.astype(jnp.float32))
    o = rmsnorm(o, P['gd_norm_g']) * jax.nn.silu(zg.reshape(B, T, GD_HEADS, GD_DV).astype(jnp.float32))
    return o.reshape(B, T, GD_HEADS * GD_DV).astype(qkv.dtype), xp[:, T:], S.astype(S0.dtype)


def trunk_layer(x, P, fox_fn, rw_shift0, rw_S0, gd_conv0, gd_S0):
    B, T, _ = x.shape
    x = x + 0.5 * swiglu(rmsnorm(x, P['norm_ffn1']), P['ffn1_wg'], P['ffn1_wu'], P['ffn1_wd'])
    h = rmsnorm(x, P['norm_mix'])
    fq, fk, fv, ff, zr, gqkv, gz, gb, ga, zg = jnp.split(h @ P['w_in'], IN_SPLIT, axis=-1)
    heads = lambda t: t.reshape(B, T, FOX_HEADS, HEAD_DIM)
    fk, fv = heads(fk), heads(fv)
    logf = jax.nn.log_sigmoid((ff + P['fox_fb']).astype(jnp.float32))
    o_fox = fox_fn(heads(fq), fk, fv, logf)
    o_rw, rw_shift, rw_S = rwkv7_mix(zr, rw_shift0, rw_S0, P)
    o_gd, gd_conv, gd_S = gated_deltanet_mix(gqkv, gz, gb, ga, gd_conv0, gd_S0, P)
    gate = jax.nn.sigmoid(zg).reshape(B, T, N_BRANCH, D_MODEL)
    merged = (gate[:, :, 0] * (o_fox @ P['w_br_fox'])
              + gate[:, :, 1] * (o_rw @ P['w_br_rw'])
              + gate[:, :, 2] * (o_gd @ P['w_br_gd']))
    x = x + merged @ P['w_out']
    x = x + 0.5 * swiglu(rmsnorm(x, P['norm_ffn2']), P['ffn2_wg'], P['ffn2_wu'], P['ffn2_wd'])
    return x, (fk, fv, logf.astype(x.dtype), rw_shift, rw_S, gd_conv, gd_S)


def setup_inputs(seed: int = 0) -> dict:
    key = jax.random.key(seed)
    ks = iter(jax.random.split(key, 64))
    nrm = lambda shape, scale: jax.random.normal(next(ks), shape, jnp.float32) * scale
    gain = lambda shape: 1.0 + nrm(shape, 0.05)
    wt = lambda shape, fan_in: nrm(shape, fan_in ** -0.5)
    L = DEPTH
    n_pages = PAST_LEN // PAGE_SIZE
    n_used = DEC_BATCH * n_pages
    n_pool = n_used + n_used // 4
    page_table = jax.random.permutation(next(ks), n_pool)[:n_used].reshape(DEC_BATCH, n_pages).astype(jnp.int32)
    dt = jnp.exp(jax.random.uniform(next(ks), (L, GD_HEADS), minval=math.log(1e-3), maxval=math.log(1e-1)))
    return {
        'x_prompt': nrm((BATCH, SEQ, D_MODEL), 1.0),
        'x_sample': nrm((DEC_BATCH, DEC_SEQ, D_MODEL), 1.0),
        'cache_k': nrm((L, n_pool, PAGE_SIZE, FOX_HEADS, HEAD_DIM), 1.0),
        'cache_v': nrm((L, n_pool, PAGE_SIZE, FOX_HEADS, HEAD_DIM), 1.0),
        'cache_logf': jax.nn.log_sigmoid(FOX_BIAS_INIT + nrm((L, n_pool, PAGE_SIZE, FOX_HEADS), 1.0)),
        'state_rwkv_shift': nrm((L, DEC_BATCH, RW_COLS), 1.0),
        'state_rwkv': nrm((L, DEC_BATCH, RW_HEADS, HEAD_DIM, HEAD_DIM), 0.5),
        'state_gdn_conv': nrm((L, DEC_BATCH, GD_CONV - 1, GD_QKV), 1.0),
        'state_gdn': nrm((L, DEC_BATCH, GD_HEADS, GD_DK, GD_DV), 0.1),
        'page_table': page_table,
        'norm_ffn1': gain((L, D_MODEL)),
        'ffn1_wg': wt((L, D_MODEL, D_FF), D_MODEL),
        'ffn1_wu': wt((L, D_MODEL, D_FF), D_MODEL),
        'ffn1_wd': wt((L, D_FF, D_MODEL), D_FF),
        'norm_mix': gain((L, D_MODEL)),
        'w_in': wt((L, D_MODEL, IN_COLS), D_MODEL),
        'fox_fb': FOX_BIAS_INIT + nrm((L, FOX_HEADS), 0.5),
        'rw_mu': jax.random.uniform(next(ks), (L, RW_COLS), jnp.float32),
        'rw_w0': -1.0 + nrm((L, RW_W), 0.5),
        'rw_w_up': wt((L, RW_LORA_W, RW_W), RW_LORA_W),
        'rw_a0': nrm((L, RW_W), 0.5),
        'rw_a_up': wt((L, RW_LORA_A, RW_W), RW_LORA_A),
        'rw_g_up': wt((L, RW_LORA_G, RW_W), RW_LORA_G),
        'rw_kk': 1.0 + nrm((L, RW_W), 0.1),
        'rw_ka': 1.0 + nrm((L, RW_W), 0.1),
        'rw_rk': nrm((L, RW_HEADS, HEAD_DIM), 0.1),
        'rw_ln_g': gain((L, RW_W)),
        'rw_ln_b': nrm((L, RW_W), 0.01),
        'gd_conv': wt((L, GD_CONV, GD_QKV), GD_CONV),
        'gd_a_log': jnp.log(jax.random.uniform(next(ks), (L, GD_HEADS), jnp.float32, 1.0, 16.0)),
        'gd_dt_bias': jnp.log(jnp.expm1(dt)),
        'gd_norm_g': gain((L, GD_DV)),
        'w_br_fox': wt((L, FOX_W, D_MODEL), FOX_W),
        'w_br_rw': wt((L, RW_W, D_MODEL), RW_W),
        'w_br_gd': wt((L, GD_HEADS * GD_DV, D_MODEL), GD_HEADS * GD_DV),
        'w_out': wt((L, D_MODEL, D_MODEL), D_MODEL),
        'norm_ffn2': gain((L, D_MODEL)),
        'ffn2_wg': wt((L, D_MODEL, D_FF), D_MODEL),
        'ffn2_wu': wt((L, D_MODEL, D_FF), D_MODEL),
        'ffn2_wd': wt((L, D_FF, D_MODEL), D_FF),
        'final_norm': gain((D_MODEL,)),
    }


def reference(x_prompt, x_sample, cache_k, cache_v, cache_logf, state_rwkv_shift, state_rwkv,
              state_gdn_conv, state_gdn, page_table, norm_ffn1, ffn1_wg, ffn1_wu, ffn1_wd, norm_mix,
              w_in, fox_fb, rw_mu, rw_w0, rw_w_up, rw_a0, rw_a_up, rw_g_up, rw_kk, rw_ka, rw_rk,
              rw_ln_g, rw_ln_b, gd_conv, gd_a_log, gd_dt_bias, gd_norm_g, w_br_fox, w_br_rw, w_br_gd,
              w_out, norm_ffn2, ffn2_wg, ffn2_wu, ffn2_wd, final_norm):
    xp, xs = x_prompt, x_sample
    bp, dt = x_prompt.shape[0], x_prompt.dtype
    prompt_states, sample_states = [], []
    for l in range(DEPTH):
        P = dict(norm_ffn1=norm_ffn1[l], ffn1_wg=ffn1_wg[l], ffn1_wu=ffn1_wu[l], ffn1_wd=ffn1_wd[l],
                 norm_mix=norm_mix[l], w_in=w_in[l], fox_fb=fox_fb[l], rw_mu=rw_mu[l], rw_w0=rw_w0[l],
                 rw_w_up=rw_w_up[l], rw_a0=rw_a0[l], rw_a_up=rw_a_up[l], rw_g_up=rw_g_up[l],
                 rw_kk=rw_kk[l], rw_ka=rw_ka[l], rw_rk=rw_rk[l], rw_ln_g=rw_ln_g[l], rw_ln_b=rw_ln_b[l],
                 gd_conv=gd_conv[l], gd_a_log=gd_a_log[l], gd_dt_bias=gd_dt_bias[l], gd_norm_g=gd_norm_g[l],
                 w_br_fox=w_br_fox[l], w_br_rw=w_br_rw[l], w_br_gd=w_br_gd[l], w_out=w_out[l],
                 norm_ffn2=norm_ffn2[l], ffn2_wg=ffn2_wg[l], ffn2_wu=ffn2_wu[l], ffn2_wd=ffn2_wd[l])
        xp, st_p = trunk_layer(xp, P, fox_prompt,
                               jnp.zeros((bp, RW_COLS), dt),
                               jnp.zeros((bp, RW_HEADS, HEAD_DIM, HEAD_DIM), dt),
                               jnp.zeros((bp, GD_CONV - 1, GD_QKV), dt),
                               jnp.zeros((bp, GD_HEADS, GD_DK, GD_DV), dt))
        prompt_states.append(st_p)
        fox_s = functools.partial(fox_sample, cache_k=cache_k, cache_v=cache_v, cache_logf=cache_logf,
                                  page_table=page_table, layer=l)
        xs, st_s = trunk_layer(xs, P, fox_s, state_rwkv_shift[l], state_rwkv[l],
                               state_gdn_conv[l], state_gdn[l])
        sample_states.append(st_s)
    fox_k_p, fox_v_p, fox_logf_p, rw_shift_p, rw_state_p, gd_conv_p, gd_state_p = [
        jnp.stack(s) for s in zip(*prompt_states)]
    fox_k_s, fox_v_s, fox_logf_s, rw_shift_s, rw_state_s, gd_conv_s, gd_state_s = [
        jnp.stack(s) for s in zip(*sample_states)]
    y_prompt = rmsnorm(xp, final_norm)
    y_sample = rmsnorm(xs, final_norm)
    return (y_prompt, y_sample,
            fox_k_p, fox_v_p, fox_logf_p, rw_shift_p, rw_state_p, gd_conv_p, gd_state_p,
            fox_k_s, fox_v_s, fox_logf_s, rw_shift_s, rw_state_s, gd_conv_s, gd_state_s)
```

```python
import functools
import math

import jax
import jax.numpy as jnp
from jax import lax
from jax.experimental import pallas as pl
from jax.experimental.pallas import tpu as pltpu

F32 = jnp.float32
BF16 = jnp.bfloat16

HEAD_DIM = 64
FOX_HEADS = 8
FOX_W = FOX_HEADS * HEAD_DIM
RW_HEADS = 8
RW_W = RW_HEADS * HEAD_DIM
RW_LORA_W = 64
RW_LORA_A = 64
RW_LORA_G = 128
RW_COLS = 3 * RW_W + RW_LORA_W + RW_LORA_A + RW_LORA_G
RW_GN_EPS = 64e-5
GD_HEADS = 4
GD_DK = 128
GD_DV = 128
GD_W = GD_HEADS * GD_DK
GD_QKV = 2 * GD_W + GD_HEADS * GD_DV
GD_CONV = 4
N_BRANCH = 3
EPS = 1e-6
L2_EPS = 1e-6
N_SMALL = 16
SUBLANES = 8
VMEM_LIMIT = 56 * 1024 * 1024


def _cparams(*sem):
    return pltpu.CompilerParams(dimension_semantics=sem, vmem_limit_bytes=VMEM_LIMIT)


def _const_spec(shape, index_map):
    return pl.BlockSpec(shape, index_map, pipeline_mode=pl.Buffered(1))


def _dot(a, b):
    return jnp.dot(a, b, preferred_element_type=F32)


def _dot_nt(a, b):
    return lax.dot_general(a, b, (((1,), (1,)), ((), ())), preferred_element_type=F32)


def _dot_tn(a, b):
    return lax.dot_general(a, b, (((0,), (0,)), ((), ())), preferred_element_type=F32)


def _split2(x):
    hi = x.astype(BF16)
    lo = (x - hi.astype(F32)).astype(BF16)
    return hi, lo


def _split3(x):
    hi = x.astype(BF16)
    r = x - hi.astype(F32)
    mid = r.astype(BF16)
    lo = (r - mid.astype(F32)).astype(BF16)
    return hi, mid, lo


def _dot_sel_r(x, sel):
    hi, mid, lo = _split3(x)
    return _dot(hi, sel) + _dot(mid, sel) + _dot(lo, sel)


def _dot_sel_l(sel, x):
    hi, mid, lo = _split3(x)
    return _dot(sel, hi) + _dot(sel, mid) + _dot(sel, lo)


def _dot_hi(a, b):
    ah, al = _split2(a)
    bh, bl = _split2(b)
    return _dot(ah, bh) + _dot(ah, bl) + _dot(al, bh)


def _softplus(z):
    return jnp.maximum(z, 0.0) + jnp.log(1.0 + jnp.exp(-jnp.abs(z)))


def _sigmoid(z):
    return 1.0 / (1.0 + jnp.exp(-z))


def _rms(x, g):
    return x * lax.rsqrt(jnp.mean(x * x, axis=-1, keepdims=True) + EPS) * g


def _tri(n, kind):
    r = lax.broadcasted_iota(jnp.int32, (n, n), 0)
    c = lax.broadcasted_iota(jnp.int32, (n, n), 1)
    return {"incl": r >= c, "strict": r > c, "upper_incl": r <= c, "upper_strict": r < c}[kind]


def _unit_lower_inverse(m_strict, n):
    r = lax.broadcasted_iota(jnp.int32, (n, n), 0)
    c = lax.broadcasted_iota(jnp.int32, (n, n), 1)
    eye = (r == c).astype(F32)
    npow = -m_strict
    t = eye + npow
    span = 2
    while span < n:
        npow = _dot_hi(npow, npow)
        t = t + _dot_hi(t, npow)
        span *= 2
    return t


def _ffn_body(x_ref, g_ref, wg_ref, wu_ref, wd_ref, fg_ref, o_ref, *, ff_chunk, final_norm):
    x = x_ref[...]
    h = _rms(x, g_ref[...]).astype(BF16)
    d_ff = wg_ref.shape[1]
    acc = jnp.zeros_like(x)
    for c0 in range(0, d_ff, ff_chunk):
        gate = _dot(h, wg_ref[:, c0:c0 + ff_chunk])
        up = _dot(h, wu_ref[:, c0:c0 + ff_chunk])
        act = (gate * _sigmoid(gate) * up).astype(BF16)
        acc = acc + _dot(act, wd_ref[c0:c0 + ff_chunk, :])
    y = x + 0.5 * acc
    if final_norm:
        y = _rms(y, fg_ref[...])
    o_ref[...] = y


def _ffn(x, norm_g, wg, wu, wd, final_g, layer, *, tm, final_norm=False):
    m, d = x.shape
    d_ff = wg.shape[2]
    ff_chunk = d_ff // 2 if (d_ff // 2) % 128 == 0 else d_ff
    body = functools.partial(_ffn_body, ff_chunk=ff_chunk, final_norm=final_norm)
    return pl.pallas_call(
        body,
        grid=(m // tm,),
        in_specs=[
            pl.BlockSpec((tm, d), lambda i: (i, 0)),
            _const_spec((None, 1, d), lambda i: (layer, 0, 0)),
            _const_spec((None, d, d_ff), lambda i: (layer, 0, 0)),
            _const_spec((None, d, d_ff), lambda i: (layer, 0, 0)),
            _const_spec((None, d_ff, d), lambda i: (layer, 0, 0)),
            _const_spec((1, d), lambda i: (0, 0)),
        ],
        out_specs=pl.BlockSpec((tm, d), lambda i: (i, 0)),
        out_shape=jax.ShapeDtypeStruct((m, d), F32),
        compiler_params=_cparams("parallel"),
        name="ffn",
    )(x, norm_g, wg, wu, wd, final_g)


def _small_fn(z, is_logf, is_beta, neg_exp_a):
    return jnp.where(is_logf, -_softplus(-z), jnp.where(is_beta, _sigmoid(z), neg_exp_a * _softplus(z)))


def _inproj_body(x_ref, g_ref, wfox_ref, wrw_ref, wgd_ref, wgate_ref, wsm_ref, wsmt_ref,
                 pc_ref, pr_ref,
                 qkvb_ref, fk_ref, fv_ref, zr_ref, gqkv_ref, gz_ref, zg_ref, sm_ref, smt_ref):
    h = _rms(x_ref[...], g_ref[...]).astype(BF16)
    fox = _dot(h, wfox_ref[...])
    qkvb_ref[...] = fox.astype(BF16)
    fk_ref[...] = fox[:, FOX_W:2 * FOX_W]
    fv_ref[...] = fox[:, 2 * FOX_W:]
    zr_ref[...] = _dot(h, wrw_ref[...])
    gd = _dot(h, wgd_ref[...])
    gqkv_ref[...] = gd[:, :GD_QKV]
    gz_ref[...] = gd[:, GD_QKV:]
    zg_ref[...] = _dot(h, wgate_ref[...])
    pc = pc_ref[...]
    col = lax.broadcasted_iota(jnp.int32, (1, N_SMALL), 1)
    sm = _dot(h, wsm_ref[...]) + pc[0:1]
    sm_ref[...] = _small_fn(sm, col < 8, col < 12, -jnp.exp(pc[1:2]))
    pr = pr_ref[...]
    row = lax.broadcasted_iota(jnp.int32, (N_SMALL, 1), 0)
    smt = _dot_nt(wsmt_ref[...], h) + pr[:, 0:1]
    smt_ref[...] = _small_fn(smt, row < 8, row < 12, -jnp.exp(pr[:, 1:2]))


def _inproj(x, norm_g, w, layer, *, tm):
    m, d = x.shape
    n_gate = w["gate"].shape[2]
    n_gd = w["gd"].shape[2]
    row_blk = lambda n: pl.BlockSpec((tm, n), lambda i: (i, 0))
    wspec = lambda a: _const_spec((None,) + a.shape[1:], lambda i: (layer, 0, 0))
    out_shape = (
        jax.ShapeDtypeStruct((m, 3 * FOX_W), BF16),
        jax.ShapeDtypeStruct((m, FOX_W), F32),
        jax.ShapeDtypeStruct((m, FOX_W), F32),
        jax.ShapeDtypeStruct((m, RW_COLS), F32),
        jax.ShapeDtypeStruct((m, GD_QKV), F32),
        jax.ShapeDtypeStruct((m, n_gd - GD_QKV), F32),
        jax.ShapeDtypeStruct((m, n_gate), F32),
        jax.ShapeDtypeStruct((m, N_SMALL), F32),
        jax.ShapeDtypeStruct((N_SMALL, m), F32),
    )
    out_specs = (
        row_blk(3 * FOX_W), row_blk(FOX_W), row_blk(FOX_W), row_blk(RW_COLS), row_blk(GD_QKV),
        row_blk(n_gd - GD_QKV), row_blk(n_gate), row_blk(N_SMALL),
        pl.BlockSpec((N_SMALL, tm), lambda i: (0, i)),
    )
    return pl.pallas_call(
        _inproj_body,
        grid=(m // tm,),
        in_specs=[
            pl.BlockSpec((tm, d), lambda i: (i, 0)),
            _const_spec((None, 1, d), lambda i: (layer, 0, 0)),
            wspec(w["fox"]), wspec(w["rw"]), wspec(w["gd"]), wspec(w["gate"]),
            wspec(w["small"]), wspec(w["small_t"]), wspec(w["small_pc"]), wspec(w["small_pr"]),
        ],
        out_specs=out_specs,
        out_shape=out_shape,
        compiler_params=_cparams("parallel"),
        name="inproj",
    )(x, norm_g, w["fox"], w["rw"], w["gd"], w["gate"], w["small"], w["small_t"],
      w["small_pc"], w["small_pr"])


def _fox_prompt_body(q_ref, k_ref, v_ref, lf_ref, lft_ref, o_ref, ccol_ref, crow_ref, *, tq, cb):
    i = pl.program_id(1)
    seq = k_ref.shape[0]

    @pl.when(i == 0)
    def _():
        lower = _tri(cb, "incl").astype(BF16)
        upper = _tri(cb, "upper_incl").astype(BF16)
        carry_c = jnp.zeros((1, N_SMALL), F32)
        carry_r = jnp.zeros((N_SMALL, 1), F32)
        for b0 in range(0, seq, cb):
            cc = _dot_sel_l(lower, lf_ref[b0:b0 + cb, :]) + carry_c
            ccol_ref[b0:b0 + cb, :] = cc
            carry_c = cc[cb - 1:cb, :]
            cr = _dot_sel_r(lft_ref[:, b0:b0 + cb], upper) + carry_r
            crow_ref[:, b0:b0 + cb] = cr
            carry_r = cr[:, cb - 1:cb]

    q0 = pl.multiple_of(i * tq, tq)
    rows = lax.broadcasted_iota(jnp.int32, (tq, tq), 0)
    cols = lax.broadcasted_iota(jnp.int32, (tq, tq), 1)
    for h in range(FOX_HEADS):
        hs = slice(h * HEAD_DIM, (h + 1) * HEAD_DIM)
        q = q_ref[:, hs]
        cq = ccol_ref[pl.ds(q0, tq), h:h + 1]

        def step(j, carry, hs=hs, h=h, q=q, cq=cq):
            m_i, l_i, acc = carry
            k0 = pl.multiple_of(j * tq, tq)
            k = k_ref[pl.ds(k0, tq), hs]
            v = v_ref[pl.ds(k0, tq), hs]
            s = _dot_nt(q, k) + cq - crow_ref[h:h + 1, pl.ds(k0, tq)]
            s = jnp.where(rows + (i - j) * tq >= cols, s, -jnp.inf)
            m_new = jnp.maximum(m_i, jnp.max(s, axis=-1, keepdims=True))
            alpha = jnp.exp(m_i - m_new)
            p = jnp.exp(s - m_new)
            l_new = alpha * l_i + jnp.sum(p, axis=-1, keepdims=True)
            acc = alpha * acc + _dot(p.astype(BF16), v)
            return m_new, l_new, acc

        init = (jnp.full((tq, 1), -jnp.inf, F32), jnp.zeros((tq, 1), F32),
                jnp.zeros((tq, HEAD_DIM), F32))
        _, l_i, acc = lax.fori_loop(0, i + 1, step, init)
        o_ref[:, hs] = (acc / l_i).astype(BF16)


def _fox_prompt(qkvb, sm, smt, batch, seq, *, tq):
    m = batch * seq
    nq = seq // tq
    cb = min(256, seq)
    body = functools.partial(_fox_prompt_body, tq=tq, cb=cb)
    return pl.pallas_call(
        body,
        grid=(batch, nq),
        in_specs=[
            pl.BlockSpec((tq, FOX_W), lambda b, i: (b * nq + i, 0)),
            pl.BlockSpec((seq, FOX_W), lambda b, i: (b, 1)),
            pl.BlockSpec((seq, FOX_W), lambda b, i: (b, 2)),
            pl.BlockSpec((seq, N_SMALL), lambda b, i: (b, 0)),
            pl.BlockSpec((N_SMALL, seq), lambda b, i: (0, b)),
        ],
        out_specs=pl.BlockSpec((tq, FOX_W), lambda b, i: (b * nq + i, 0)),
        out_shape=jax.ShapeDtypeStruct((m, FOX_W), BF16),
        scratch_shapes=[pltpu.VMEM((seq, N_SMALL), F32), pltpu.VMEM((N_SMALL, seq), F32)],
        compiler_params=_cparams("parallel", "arbitrary"),
        name="fox_prompt",
    )(qkvb, qkvb, qkvb, sm, smt)


def _fox_sample_body(pt_ref, q_ref, kn_ref, vn_ref, lfn_ref, kp_ref, vp_ref, lfp_ref, o_ref,
                     qbd_ref, m_ref, l_ref, acc_ref, tail_ref, *, t_valid):
    s_idx = pl.program_id(1)
    tp = SUBLANES
    nrow = FOX_HEADS * tp
    row_head = lax.broadcasted_iota(jnp.int32, (nrow, FOX_W), 0) // tp
    lane_head = lax.broadcasted_iota(jnp.int32, (nrow, FOX_W), 1) // HEAD_DIM
    diag = row_head == lane_head

    def expand_rows(x):
        return jnp.broadcast_to(x[:, None, :], (FOX_HEADS, tp, x.shape[1])).reshape(nrow, x.shape[1])

    @pl.when(s_idx == 0)
    def _():
        q = q_ref[...]
        qbd = jnp.where(diag, jnp.concatenate([q] * FOX_HEADS, axis=0), jnp.zeros((), BF16))
        qbd_ref[...] = qbd
        upper = _tri(tp, "upper_incl").astype(BF16)
        lf_hi, lf_mid, lf_lo = _split3(lfn_ref[...])
        cn = (_dot_tn(lf_hi, upper) + _dot_tn(lf_mid, upper) + _dot_tn(lf_lo, upper))[:FOX_HEADS]
        s = _dot_nt(qbd, kn_ref[...]) - expand_rows(cn)
        t_q = lax.broadcasted_iota(jnp.int32, (nrow, tp), 0) % tp
        u_k = lax.broadcasted_iota(jnp.int32, (nrow, tp), 1)
        s = jnp.where((u_k <= t_q) & (u_k < t_valid), s, -jnp.inf)
        m = jnp.max(s, axis=-1, keepdims=True)
        p = jnp.exp(s - m)
        m_ref[...] = m
        l_ref[...] = jnp.sum(p, axis=-1, keepdims=True)
        acc_ref[...] = _dot(p.astype(BF16), vn_ref[...])
        tail_ref[...] = jnp.zeros_like(tail_ref)

    @pl.when(s_idx > 0)
    def _():
        page = lfp_ref.shape[1]
        lf = lfp_ref[...]
        later = _tri(page, "strict").astype(BF16)
        suf = _dot_sel_r(lf, later) + tail_ref[...]
        tail_ref[...] = tail_ref[...] + jnp.sum(lf, axis=-1, keepdims=True)
        s = _dot_nt(qbd_ref[...], kp_ref[...].astype(BF16)) + expand_rows(suf)
        m_old = m_ref[...]
        m_new = jnp.maximum(m_old, jnp.max(s, axis=-1, keepdims=True))
        alpha = jnp.exp(m_old - m_new)
        p = jnp.exp(s - m_new)
        m_ref[...] = m_new
        l_ref[...] = alpha * l_ref[...] + jnp.sum(p, axis=-1, keepdims=True)
        acc_ref[...] = alpha * acc_ref[...] + _dot(p.astype(BF16), vp_ref[...].astype(BF16))

    @pl.when(s_idx == pl.num_programs(1) - 1)
    def _():
        o = jnp.where(diag, acc_ref[...] / l_ref[...], 0.0)
        out = o[0:tp]
        for h in range(1, FOX_HEADS):
            out = out + o[h * tp:(h + 1) * tp]
        o_ref[...] = out.astype(BF16)


def _fox_sample(qkvb, sm, cache_k, cache_v, cache_lft, page_table, layer, *, t_valid):
    tp = SUBLANES
    bd, n_pages = page_table.shape
    page = cache_k.shape[2]
    nrow = FOX_HEADS * tp

    def page_of(b, s, pt):
        return pt[b, n_pages - jnp.maximum(s, 1)]

    grid_spec = pltpu.PrefetchScalarGridSpec(
        num_scalar_prefetch=1,
        grid=(bd, n_pages + 1),
        in_specs=[
            pl.BlockSpec((tp, FOX_W), lambda b, s, pt: (b, 0)),
            pl.BlockSpec((tp, FOX_W), lambda b, s, pt: (b, 1)),
            pl.BlockSpec((tp, FOX_W), lambda b, s, pt: (b, 2)),
            pl.BlockSpec((tp, N_SMALL), lambda b, s, pt: (b, 0)),
            pl.BlockSpec((None, None, page, FOX_W), lambda b, s, pt: (layer, page_of(b, s, pt), 0, 0)),
            pl.BlockSpec((None, None, page, FOX_W), lambda b, s, pt: (layer, page_of(b, s, pt), 0, 0)),
            pl.BlockSpec((None, None, FOX_HEADS, page), lambda b, s, pt: (layer, page_of(b, s, pt), 0, 0)),
        ],
        out_specs=pl.BlockSpec((tp, FOX_W), lambda b, s, pt: (b, 0)),
        scratch_shapes=[
            pltpu.VMEM((nrow, FOX_W), BF16),
            pltpu.VMEM((nrow, 1), F32),
            pltpu.VMEM((nrow, 1), F32),
            pltpu.VMEM((nrow, FOX_W), F32),
            pltpu.VMEM((FOX_HEADS, 1), F32),
        ],
    )
    body = functools.partial(_fox_sample_body, t_valid=t_valid)
    return pl.pallas_call(
        body,
        grid_spec=grid_spec,
        out_shape=jax.ShapeDtypeStruct((bd * tp, FOX_W), BF16),
        compiler_params=_cparams("parallel", "arbitrary"),
        name="fox_sample",
    )(page_table, qkvb, qkvb, qkvb, sm, cache_k, cache_v, cache_lft)


def _rwkv_body(z_ref, shift_ref, s0_ref, mu_ref, pv_ref, wup_ref, aup_ref, gup_ref, seg_ref,
               o_ref, s_ref, prev_ref, *, chunk, t_valid):
    c = pl.program_id(1)
    z = z_ref[...]

    @pl.when(c == 0)
    def _():
        s_ref[...] = s0_ref[...]
        prev_ref[0:1, :] = shift_ref[...]

    row = lax.broadcasted_iota(jnp.int32, (chunk, 1), 0)
    z_prev = jnp.where(row == 0, prev_ref[0:1, :], pltpu.roll(z, 1, axis=0))
    prev_ref[0:1, :] = z[chunk - 1:chunk, :]
    zs = z + (z_prev - z) * mu_ref[...]
    r = zs[:, 0:RW_W]
    k = zs[:, RW_W:2 * RW_W]
    v = zs[:, 2 * RW_W:3 * RW_W]
    o_l = 3 * RW_W
    wd = zs[:, o_l:o_l + RW_LORA_W]
    ad = zs[:, o_l + RW_LORA_W:o_l + RW_LORA_W + RW_LORA_A]
    gd = zs[:, o_l + RW_LORA_W + RW_LORA_A:]
    pv = pv_ref[...]
    w_log = -_softplus(-(pv[0:1] + _dot(jnp.tanh(wd).astype(BF16), wup_ref[...]))) - 0.5
    lw = -jnp.exp(w_log)
    a = _sigmoid(pv[1:2] + _dot(ad.astype(BF16), aup_ref[...]))
    g = _dot(_sigmoid(gd).astype(BF16), gup_ref[...])
    kk = k * pv[2:3]
    k = k * (1.0 + (a - 1.0) * pv[3:4])
    seg = seg_ref[...]

    def seg_sum(x):
        hi, lo = _split2(x)
        return _dot(hi, seg) + _dot(lo, seg)

    kk = kk * lax.rsqrt(seg_sum(kk * kk) + L2_EPS)
    if t_valid < chunk:
        valid = row < t_valid
        lw = jnp.where(valid, lw, 0.0)
        kk = jnp.where(valid, kk, 0.0)
        k = jnp.where(valid, k, 0.0)
    cs = _dot_sel_l(_tri(chunk, "incl").astype(BF16), lw)
    p_inv = jnp.exp(-cs)
    rp = (r * jnp.exp(cs)).astype(BF16)
    kkp = (kk * jnp.exp(cs - lw)).astype(BF16)
    bn = kk * a * p_inv
    kn = k * p_inv
    p_end = jnp.exp(cs[chunk - 1:chunk, :])
    bn_e = (bn * p_end).astype(BF16)
    kn_e = (kn * p_end).astype(BF16)
    bn = bn.astype(BF16)
    kn = kn.astype(BF16)
    vb = v.astype(BF16)
    strict = _tri(chunk, "strict")
    incl = _tri(chunk, "incl")
    ys = []
    for h in range(RW_HEADS):
        hs = slice(h * HEAD_DIM, (h + 1) * HEAD_DIM)
        st = s_ref[h]
        stb = st.astype(BF16)
        a_ab = jnp.where(strict, _dot_nt(kkp[:, hs], bn[:, hs]), 0.0)
        a_ak = jnp.where(strict, _dot_nt(kkp[:, hs], kn[:, hs]), 0.0)
        a_rb = jnp.where(incl, _dot_nt(rp[:, hs], bn[:, hs]), 0.0)
        a_rk = jnp.where(incl, _dot_nt(rp[:, hs], kn[:, hs]), 0.0)
        t_inv = _unit_lower_inverse(a_ab, chunk)
        rhs = _dot_nt(kkp[:, hs], stb) + _dot(a_ak.astype(BF16), vb[:, hs])
        sa = -_dot_hi(t_inv, rhs)
        sab = sa.astype(BF16)
        y = _dot_nt(rp[:, hs], stb) + _dot(a_rb.astype(BF16), sab) + _dot(a_rk.astype(BF16), vb[:, hs])
        ys.append(y)
        s_ref[h] = st * p_end[:, hs] + _dot_tn(sab, bn_e[:, hs]) + _dot_tn(vb[:, hs], kn_e[:, hs])
    y = jnp.concatenate(ys, axis=-1)
    inv_n = 1.0 / HEAD_DIM
    mean = seg_sum(y) * inv_n
    yc = y - mean
    var = seg_sum(yc * yc) * inv_n
    yn = yc * lax.rsqrt(var + RW_GN_EPS) * pv[5:6] + pv[6:7]
    bonus = seg_sum(r * k * pv[4:5]) * v
    o_ref[...] = ((yn + bonus) * g).astype(BF16)


def _rwkv(zr, shift0, s0t, w, layer, batch, seq, *, chunk, t_valid):
    m = batch * seq
    nc = seq // chunk
    body = functools.partial(_rwkv_body, chunk=chunk, t_valid=t_valid)
    lspec = lambda a: _const_spec((None,) + a.shape[1:], lambda b, c: (layer, 0, 0))
    return pl.pallas_call(
        body,
        grid=(batch, nc),
        in_specs=[
            pl.BlockSpec((chunk, RW_COLS), lambda b, c: (b * nc + c, 0)),
            pl.BlockSpec((None, 1, RW_COLS), lambda b, c: (b, 0, 0)),
            pl.BlockSpec((None, RW_HEADS, HEAD_DIM, HEAD_DIM), lambda b, c: (b, 0, 0, 0)),
            lspec(w["mu"]), lspec(w["pv"]), lspec(w["w_up"]), lspec(w["a_up"]), lspec(w["g_up"]),
            _const_spec((RW_W, RW_W), lambda b, c: (0, 0)),
        ],
        out_specs=(
            pl.BlockSpec((chunk, RW_W), lambda b, c: (b * nc + c, 0)),
            pl.BlockSpec((None, RW_HEADS, HEAD_DIM, HEAD_DIM), lambda b, c: (b, 0, 0, 0)),
        ),
        out_shape=(
            jax.ShapeDtypeStruct((m, RW_W), BF16),
            jax.ShapeDtypeStruct((batch, RW_HEADS, HEAD_DIM, HEAD_DIM), F32),
        ),
        scratch_shapes=[pltpu.VMEM((SUBLANES, RW_COLS), F32)],
        compiler_params=_cparams("parallel", "arbitrary"),
        name="rwkv",
    )(zr, shift0, s0t, w["mu"], w["pv"], w["w_up"], w["a_up"], w["g_up"], w["seg"])


def _gdn_body(x_ref, gz_ref, sm_ref, conv0_ref, s0_ref, cw_ref, ng_ref, seg_ref,
              o_ref, s_ref, prev_ref, *, chunk, t_valid):
    c = pl.program_id(1)
    x = x_ref[...]

    @pl.when(c == 0)
    def _():
        s_ref[...] = s0_ref[...]
        prev_ref[...] = conv0_ref[...]

    prev = prev_ref[...]
    prev_ref[...] = x[chunk - SUBLANES:, :]
    cw = cw_ref[...]
    row8 = lax.broadcasted_iota(jnp.int32, (SUBLANES, 1), 0)
    conv = x * cw[GD_CONV - 1:GD_CONV]
    for sft in range(1, GD_CONV):
        xr = pltpu.roll(x, sft, axis=0)
        top = jnp.where(row8 < sft, pltpu.roll(prev, sft, axis=0), xr[:SUBLANES])
        xs = top if chunk == SUBLANES else jnp.concatenate([top, xr[SUBLANES:]], axis=0)
        conv = conv + xs * cw[GD_CONV - 1 - sft:GD_CONV - sft]
    act = conv * _sigmoid(conv)
    seg = seg_ref[...]

    def seg_sum(t):
        hi, lo = _split2(t)
        return _dot(hi, seg) + _dot(lo, seg)

    q = act[:, :GD_W]
    k = act[:, GD_W:2 * GD_W]
    v = act[:, 2 * GD_W:]
    q = q * lax.rsqrt(seg_sum(q * q) + L2_EPS) * (GD_DK ** -0.5)
    k = k * lax.rsqrt(seg_sum(k * k) + L2_EPS)
    sm = sm_ref[...]
    beta = sm[:, 8:8 + GD_HEADS]
    g = sm[:, 12:12 + GD_HEADS]
    if t_valid < chunk:
        rowc = lax.broadcasted_iota(jnp.int32, (chunk, 1), 0)
        beta = jnp.where(rowc < t_valid, beta, 0.0)
        g = jnp.where(rowc < t_valid, g, 0.0)
    gc = _dot_sel_l(_tri(chunk, "incl").astype(BF16), g)
    gc_hi, gc_mid, gc_lo = _split3(gc)
    head_col = lax.broadcasted_iota(jnp.int32, (chunk, GD_HEADS), 1)
    strict = _tri(chunk, "strict")
    incl = _tri(chunk, "incl")
    outs = []
    for h in range(GD_HEADS):
        hs = slice(h * GD_DK, (h + 1) * GD_DK)
        gcol = gc[:, h:h + 1]
        glast = gc[chunk - 1:chunk, h:h + 1]
        pick = (head_col == h).astype(BF16)
        grow = _dot_nt(pick, gc_hi) + _dot_nt(pick, gc_mid) + _dot_nt(pick, gc_lo)
        decay = jnp.exp(jnp.minimum(gcol - grow, 0.0))
        bcol = beta[:, h:h + 1]
        kh = k[:, hs]
        khb = kh.astype(BF16)
        kb = kh * bcol
        mm = jnp.where(strict, _dot_nt(kb.astype(BF16), khb) * decay, 0.0)
        t_inv = _unit_lower_inverse(mm, chunk).astype(BF16)
        egc = jnp.exp(gcol)
        u = _dot(t_inv, (v[:, hs] * bcol).astype(BF16))
        w = _dot(t_inv, (kb * egc).astype(BF16))
        qh = q[:, hs]
        amat = jnp.where(incl, _dot_nt(qh.astype(BF16), khb) * decay, 0.0)
        st = s_ref[h]
        stb = st.astype(BF16)
        v_new = u - _dot(w.astype(BF16), stb)
        vnb = v_new.astype(BF16)
        outs.append(_dot((qh * egc).astype(BF16), stb) + _dot(amat.astype(BF16), vnb))
        kg = (kh * jnp.exp(glast - gcol)).astype(BF16)
        s_ref[h] = st * jnp.exp(glast) + _dot_tn(kg, vnb)
    o = jnp.concatenate(outs, axis=-1)
    o = o * lax.rsqrt(seg_sum(o * o) * (1.0 / GD_DV) + EPS) * ng_ref[...]
    gz = gz_ref[...]
    o_ref[...] = (o * (gz * _sigmoid(gz))).astype(BF16)


def _gdn(gqkv, gz, sm, conv0, s0, w, layer, batch, seq, *, chunk, t_valid):
    m = batch * seq
    nc = seq // chunk
    body = functools.partial(_gdn_body, chunk=chunk, t_valid=t_valid)
    lspec = lambda a: _const_spec((None,) + a.shape[1:], lambda b, c: (layer, 0, 0))
    return pl.pallas_call(
        body,
        grid=(batch, nc),
        in_specs=[
            pl.BlockSpec((chunk, GD_QKV), lambda b, c: (b * nc + c, 0)),
            pl.BlockSpec((chunk, GD_HEADS * GD_DV), lambda b, c: (b * nc + c, 0)),
            pl.BlockSpec((chunk, N_SMALL), lambda b, c: (b * nc + c, 0)),
            pl.BlockSpec((None, SUBLANES, GD_QKV), lambda b, c: (b, 0, 0)),
            pl.BlockSpec((None, GD_HEADS, GD_DK, GD_DV), lambda b, c: (b, 0, 0, 0)),
            lspec(w["conv"]), lspec(w["norm_g"]),
            _const_spec((GD_W, GD_W), lambda b, c: (0, 0)),
        ],
        out_specs=(
            pl.BlockSpec((chunk, GD_HEADS * GD_DV), lambda b, c: (b * nc + c, 0)),
            pl.BlockSpec((None, GD_HEADS, GD_DK, GD_DV), lambda b, c: (b, 0, 0, 0)),
        ),
        out_shape=(
            jax.ShapeDtypeStruct((m, GD_HEADS * GD_DV), BF16),
            jax.ShapeDtypeStruct((batch, GD_HEADS, GD_DK, GD_DV), F32),
        ),
        scratch_shapes=[pltpu.VMEM((SUBLANES, GD_QKV), F32)],
        compiler_params=_cparams("parallel", "arbitrary"),
        name="gdn",
    )(gqkv, gz, sm, conv0, s0, w["conv"], w["norm_g"], w["seg"])


def _merge_body(x_ref, of_ref, or_ref, og_ref, zg_ref, wf_ref, wr_ref, wg_ref, wo_ref, o_ref):
    d = x_ref.shape[1]
    merged = _sigmoid(zg_ref[:, 0:d]) * _dot(of_ref[...], wf_ref[...])
    merged = merged + _sigmoid(zg_ref[:, d:2 * d]) * _dot(or_ref[...], wr_ref[...])
    merged = merged + _sigmoid(zg_ref[:, 2 * d:3 * d]) * _dot(og_ref[...], wg_ref[...])
    o_ref[...] = x_ref[...] + _dot(merged.astype(BF16), wo_ref[...])


def _merge(x, o_fox, o_rw, o_gd, zg, w, layer, *, tm):
    m, d = x.shape
    row_blk = lambda n: pl.BlockSpec((tm, n), lambda i: (i, 0))
    wspec = lambda a: _const_spec((None,) + a.shape[1:], lambda i: (layer, 0, 0))
    return pl.pallas_call(
        _merge_body,
        grid=(m // tm,),
        in_specs=[row_blk(d), row_blk(FOX_W), row_blk(RW_W), row_blk(GD_HEADS * GD_DV),
                  row_blk(N_BRANCH * d),
                  wspec(w["br_fox"]), wspec(w["br_rw"]), wspec(w["br_gd"]), wspec(w["out"])],
        out_specs=row_blk(d),
        out_shape=jax.ShapeDtypeStruct((m, d), F32),
        compiler_params=_cparams("parallel"),
        name="merge",
    )(x, o_fox, o_rw, o_gd, zg, w["br_fox"], w["br_rw"], w["br_gd"], w["out"])


def _same_segment(n, width):
    idx = jnp.arange(n) // width
    return (idx[:, None] == idx[None, :]).astype(BF16)


def _row_tile(m):
    for tm in (512, 256, 128, 64, 32, 16, 8):
        if m % tm == 0:
            return tm
    raise ValueError(f"row count {m} is not a multiple of 8")


def kernel(x_prompt, x_sample, cache_k, cache_v, cache_logf, state_rwkv_shift, state_rwkv,
           state_gdn_conv, state_gdn, page_table, norm_ffn1, ffn1_wg, ffn1_wu, ffn1_wd, norm_mix,
           w_in, fox_fb, rw_mu, rw_w0, rw_w_up, rw_a0, rw_a_up, rw_g_up, rw_kk, rw_ka, rw_rk,
           rw_ln_g, rw_ln_b, gd_conv, gd_a_log, gd_dt_bias, gd_norm_g, w_br_fox, w_br_rw, w_br_gd,
           w_out, norm_ffn2, ffn2_wg, ffn2_wu, ffn2_wd, final_norm):
    bp, seq, d = x_prompt.shape
    bd, t_s, _ = x_sample.shape
    depth = w_in.shape[0]
    tp = SUBLANES
    assert t_s <= tp and seq % 64 == 0

    o_rw_c = 3 * FOX_W + FOX_HEADS
    o_gd_c = o_rw_c + RW_COLS
    o_gz = o_gd_c + GD_QKV
    o_gb = o_gz + GD_HEADS * GD_DV
    o_ga = o_gb + GD_HEADS
    o_gate = o_ga + GD_HEADS
    scale = HEAD_DIM ** -0.5
    w_small = jnp.concatenate([w_in[:, :, 3 * FOX_W:o_rw_c], w_in[:, :, o_gb:o_gate]], axis=-1)
    zeros4 = jnp.zeros((depth, GD_HEADS), F32)
    small_bias = jnp.concatenate([fox_fb, zeros4, gd_dt_bias], axis=-1)
    small_alog = jnp.concatenate([jnp.zeros((depth, 8), F32), zeros4, gd_a_log], axis=-1)
    w_proj = {
        "fox": jnp.concatenate([w_in[:, :, :FOX_W] * scale, w_in[:, :, FOX_W:3 * FOX_W]], axis=-1).astype(BF16),
        "rw": w_in[:, :, o_rw_c:o_gd_c].astype(BF16),
        "gd": w_in[:, :, o_gd_c:o_gb].astype(BF16),
        "gate": w_in[:, :, o_gate:].astype(BF16),
        "small": w_small.astype(BF16),
        "small_t": jnp.swapaxes(w_small, 1, 2).astype(BF16),
        "small_pc": jnp.stack([small_bias, small_alog], axis=1),
        "small_pr": jnp.stack([small_bias, small_alog], axis=2),
    }
    w_rw = {
        "mu": rw_mu[:, None, :],
        "pv": jnp.stack([rw_w0, rw_a0, rw_kk, rw_ka, rw_rk.reshape(depth, RW_W), rw_ln_g, rw_ln_b,
                         jnp.zeros_like(rw_w0)], axis=1),
        "w_up": rw_w_up.astype(BF16), "a_up": rw_a_up.astype(BF16), "g_up": rw_g_up.astype(BF16),
        "seg": _same_segment(RW_W, HEAD_DIM),
    }
    w_gd = {
        "conv": jnp.pad(gd_conv, ((0, 0), (0, SUBLANES - GD_CONV), (0, 0))),
        "norm_g": jnp.tile(gd_norm_g, (1, GD_HEADS))[:, None, :],
        "seg": _same_segment(GD_W, GD_DK),
    }
    w_mg = {"br_fox": w_br_fox.astype(BF16), "br_rw": w_br_rw.astype(BF16),
            "br_gd": w_br_gd.astype(BF16), "out": w_out.astype(BF16)}
    ffn1 = (norm_ffn1[:, None, :], ffn1_wg.astype(BF16), ffn1_wu.astype(BF16), ffn1_wd.astype(BF16))
    ffn2 = (norm_ffn2[:, None, :], ffn2_wg.astype(BF16), ffn2_wu.astype(BF16), ffn2_wd.astype(BF16))
    norm_mix3 = norm_mix[:, None, :]
    final_g = final_norm[None, :]

    n_pool, page = cache_k.shape[1], cache_k.shape[2]
    ck = cache_k.reshape(depth, n_pool, page, FOX_W)
    cv = cache_v.reshape(depth, n_pool, page, FOX_W)
    clft = jnp.swapaxes(cache_logf, 2, 3)
    gd_conv0_s = jnp.pad(state_gdn_conv, ((0, 0), (0, 0), (SUBLANES - (GD_CONV - 1), 0), (0, 0)))
    zeros_p = {
        "shift": jnp.zeros((bp, 1, RW_COLS), F32),
        "rw_s": jnp.zeros((bp, RW_HEADS, HEAD_DIM, HEAD_DIM), F32),
        "conv": jnp.zeros((bp, SUBLANES, GD_QKV), F32),
        "gd_s": jnp.zeros((bp, GD_HEADS, GD_DK, GD_DV), F32),
    }

    xp = x_prompt.reshape(bp * seq, d)
    xs = jnp.pad(x_sample, ((0, 0), (0, tp - t_s), (0, 0))).reshape(bd * tp, d)
    tm_p, tm_s = _row_tile(bp * seq), _row_tile(bd * tp)
    tm_in_p = min(tm_p, 256)
    tq = min(256, seq)
    chunk_p = 64

    def layer_fn(x, l, *, batch, t_len, tm, tm_in, fox_fn, shift0, rw_s0t, conv0, gd_s0, chunk, t_valid, last):
        x = _ffn(x, *ffn1, final_g, l, tm=tm)
        qkvb, fk, fv, zr, gqkv, gz, zg, sm, smt = _inproj(x, norm_mix3, w_proj, l, tm=tm_in)
        o_fox = fox_fn(qkvb, sm, smt, l)
        o_rw, rw_st = _rwkv(zr, shift0, rw_s0t, w_rw, l, batch, t_len, chunk=chunk, t_valid=t_valid)
        o_gd, gd_s = _gdn(gqkv, gz, sm, conv0, gd_s0, w_gd, l, batch, t_len, chunk=chunk, t_valid=t_valid)
        x = _merge(x, o_fox, o_rw, o_gd, zg, w_mg, l, tm=tm)
        x = _ffn(x, *ffn2, final_g, l, tm=tm, final_norm=last)
        return x, (fk, fv, sm, zr, rw_st, gqkv, gd_s)

    p_states, s_states = [], []
    for l in range(depth):
        last = l == depth - 1
        xp, st = layer_fn(
            xp, l, batch=bp, t_len=seq, tm=tm_p, tm_in=tm_in_p,
            fox_fn=lambda qkvb, sm, smt, l: _fox_prompt(qkvb, sm, smt, bp, seq, tq=tq),
            shift0=zeros_p["shift"], rw_s0t=zeros_p["rw_s"], conv0=zeros_p["conv"], gd_s0=zeros_p["gd_s"],
            chunk=chunk_p, t_valid=chunk_p, last=last)
        fk, fv, sm, zr, rw_st, gqkv, gd_s = st
        p_states.append((
            fk.reshape(bp, seq, FOX_HEADS, HEAD_DIM), fv.reshape(bp, seq, FOX_HEADS, HEAD_DIM),
            sm[:, :FOX_HEADS].reshape(bp, seq, FOX_HEADS),
            zr.reshape(bp, seq, RW_COLS)[:, seq - 1],
            rw_st,
            gqkv.reshape(bp, seq, GD_QKV)[:, seq - (GD_CONV - 1):],
            gd_s))
        xs, st = layer_fn(
            xs, l, batch=bd, t_len=tp, tm=tm_s, tm_in=tm_s,
            fox_fn=lambda qkvb, sm, smt, l: _fox_sample(qkvb, sm, ck, cv, clft, page_table, l, t_valid=t_s),
            shift0=state_rwkv_shift[l][:, None, :], rw_s0t=state_rwkv[l], conv0=gd_conv0_s[l],
            gd_s0=state_gdn[l], chunk=tp, t_valid=t_s, last=last)
        fk, fv, sm, zr, rw_st, gqkv, gd_s = st
        conv_ext = jnp.concatenate([state_gdn_conv[l], gqkv.reshape(bd, tp, GD_QKV)[:, :t_s]], axis=1)
        s_states.append((
            fk.reshape(bd, tp, FOX_HEADS, HEAD_DIM)[:, :t_s], fv.reshape(bd, tp, FOX_HEADS, HEAD_DIM)[:, :t_s],
            sm[:, :FOX_HEADS].reshape(bd, tp, FOX_HEADS)[:, :t_s],
            zr.reshape(bd, tp, RW_COLS)[:, t_s - 1],
            rw_st,
            conv_ext[:, t_s:],
            gd_s))

    p_out = [jnp.stack(s) for s in zip(*p_states)]
    s_out = [jnp.stack(s) for s in zip(*s_states)]
    y_prompt = xp.reshape(bp, seq, d)
    y_sample = xs.reshape(bd, tp, d)[:, :t_s]
    return (y_prompt, y_sample, *p_out, *s_out)
```

```python
import functools

import jax
import jax.numpy as jnp
from jax import lax
from jax.experimental import pallas as pl
from jax.experimental.pallas import tpu as pltpu

F32 = jnp.float32
BF16 = jnp.bfloat16

HEAD_DIM = 64
FOX_HEADS = 8
FOX_W = FOX_HEADS * HEAD_DIM
RW_HEADS = 8
RW_W = RW_HEADS * HEAD_DIM
RW_LORA_W = 64
RW_LORA_A = 64
RW_LORA_G = 128
RW_COLS = 3 * RW_W + RW_LORA_W + RW_LORA_A + RW_LORA_G
RW_GN_EPS = 64e-5
GD_HEADS = 4
GD_DK = 128
GD_DV = 128
GD_W = GD_HEADS * GD_DK
GD_QKV = 2 * GD_W + GD_HEADS * GD_DV
GD_CONV = 4
N_BRANCH = 3
EPS = 1e-6
L2_EPS = 1e-6
N_SMALL = 16
SUBLANES = 8
GROUP = 4
RW_GROUPS = RW_HEADS // GROUP
RW_GW = GROUP * HEAD_DIM
VMEM_LIMIT = 56 * 1024 * 1024


def _cparams(*sem):
    return pltpu.CompilerParams(dimension_semantics=sem, vmem_limit_bytes=VMEM_LIMIT)


def _const_spec(shape, index_map):
    return pl.BlockSpec(shape, index_map, pipeline_mode=pl.Buffered(1))


def _dot(a, b):
    return jnp.dot(a, b, preferred_element_type=F32)


def _dot_nt(a, b):
    return lax.dot_general(a, b, (((1,), (1,)), ((), ())), preferred_element_type=F32)


def _dot_tn(a, b):
    return lax.dot_general(a, b, (((0,), (0,)), ((), ())), preferred_element_type=F32)


def _split2(x):
    hi = x.astype(BF16)
    lo = (x - hi.astype(F32)).astype(BF16)
    return hi, lo


def _split3(x):
    hi = x.astype(BF16)
    r = x - hi.astype(F32)
    mid = r.astype(BF16)
    lo = (r - mid.astype(F32)).astype(BF16)
    return hi, mid, lo


def _dot_sel_r(x, sel):
    hi, mid, lo = _split3(x)
    return _dot(hi, sel) + _dot(mid, sel) + _dot(lo, sel)


def _dot_sel_l(sel, x):
    return _dot(jnp.concatenate([sel] * 3, axis=1), jnp.concatenate(_split3(x), axis=0))


def _seg_sum(x, seg2):
    return _dot(jnp.concatenate(_split2(x), axis=1), seg2)


def _softplus(z):
    return jnp.maximum(z, 0.0) + jnp.log(1.0 + jnp.exp(-jnp.abs(z)))


def _sigmoid(z):
    return 1.0 / (1.0 + jnp.exp(-z))


def _rms(x, g):
    return x * lax.rsqrt(jnp.mean(x * x, axis=-1, keepdims=True) + EPS) * g


def _tri(n, kind):
    r = lax.broadcasted_iota(jnp.int32, (n, n), 0)
    c = lax.broadcasted_iota(jnp.int32, (n, n), 1)
    return {"incl": r >= c, "strict": r > c, "upper_incl": r <= c}[kind]


def _block_tri(n, blk, strict, reps=1):
    r = lax.broadcasted_iota(jnp.int32, (n, n * reps), 0)
    c = lax.broadcasted_iota(jnp.int32, (n, n * reps), 1) % n
    same = (r // blk) == (c // blk)
    return same & ((r > c) if strict else (r >= c))


def _head_mask(rows, row_blk, cols, col_blk):
    r = lax.broadcasted_iota(jnp.int32, (rows, cols), 0) // row_blk
    c = lax.broadcasted_iota(jnp.int32, (rows, cols), 1) // col_blk
    return r == c


def _expand(x, mask):
    return jnp.where(mask, jnp.concatenate([x] * GROUP, axis=0), jnp.zeros((), x.dtype))


def _collapse(x, t):
    out = x[0:t]
    for h in range(1, GROUP):
        out = out + x[h * t:(h + 1) * t]
    return out


def _unit_lower_inverse_minus_eye(m_strict, order):
    npow = -m_strict
    t = npow
    span = 2
    while span < order:
        nb = npow.astype(BF16)
        npow = _dot(nb, nb)
        t = t + npow + _dot(t.astype(BF16), npow.astype(BF16))
        span *= 2
    return t


def _ffn_body(x_ref, g_ref, wg_ref, wu_ref, wd_ref, fg_ref, o_ref, *, ff_chunk, final_norm):
    x = x_ref[...]
    h = _rms(x, g_ref[...]).astype(BF16)
    d_ff = wg_ref.shape[1]
    acc = jnp.zeros_like(x)
    for c0 in range(0, d_ff, ff_chunk):
        gate = _dot(h, wg_ref[:, c0:c0 + ff_chunk])
        up = _dot(h, wu_ref[:, c0:c0 + ff_chunk])
        act = (gate * _sigmoid(gate) * up).astype(BF16)
        acc = acc + _dot(act, wd_ref[c0:c0 + ff_chunk, :])
    y = x + 0.5 * acc
    if final_norm:
        y = _rms(y, fg_ref[...])
    o_ref[...] = y


def _ffn(x, norm_g, wg, wu, wd, final_g, layer, *, tm, final_norm=False):
    m, d = x.shape
    d_ff = wg.shape[2]
    ff_chunk = d_ff // 2 if (d_ff // 2) % 128 == 0 else d_ff
    body = functools.partial(_ffn_body, ff_chunk=ff_chunk, final_norm=final_norm)
    return pl.pallas_call(
        body,
        grid=(m // tm,),
        in_specs=[
            pl.BlockSpec((tm, d), lambda i: (i, 0)),
            _const_spec((None, 1, d), lambda i: (layer, 0, 0)),
            _const_spec((None, d, d_ff), lambda i: (layer, 0, 0)),
            _const_spec((None, d, d_ff), lambda i: (layer, 0, 0)),
            _const_spec((None, d_ff, d), lambda i: (layer, 0, 0)),
            _const_spec((1, d), lambda i: (0, 0)),
        ],
        out_specs=pl.BlockSpec((tm, d), lambda i: (i, 0)),
        out_shape=jax.ShapeDtypeStruct((m, d), F32),
        compiler_params=_cparams("parallel"),
        name="ffn",
    )(x, norm_g, wg, wu, wd, final_g)


def _small_fn(z, is_logf, is_beta, neg_exp_a):
    return jnp.where(is_logf, -_softplus(-z), jnp.where(is_beta, _sigmoid(z), neg_exp_a * _softplus(z)))


def _inproj_body(x_ref, g_ref, wfox_ref, wrw_ref, wgd_ref, wgate_ref, wsm_ref, wsmt_ref,
                 pc_ref, pr_ref,
                 qkvb_ref, fk_ref, fv_ref, zr_ref, gqkv_ref, gz_ref, zg_ref, sm_ref, smt_ref):
    h = _rms(x_ref[...], g_ref[...]).astype(BF16)
    fox = _dot(h, wfox_ref[...])
    qkvb_ref[...] = fox.astype(BF16)
    fk_ref[...] = fox[:, FOX_W:2 * FOX_W]
    fv_ref[...] = fox[:, 2 * FOX_W:]
    zr_ref[...] = _dot(h, wrw_ref[...])
    gd = _dot(h, wgd_ref[...])
    gqkv_ref[...] = gd[:, :GD_QKV]
    gz_ref[...] = gd[:, GD_QKV:]
    zg_ref[...] = _dot(h, wgate_ref[...])
    pc = pc_ref[...]
    col = lax.broadcasted_iota(jnp.int32, (1, N_SMALL), 1)
    sm = _dot(h, wsm_ref[...]) + pc[0:1]
    sm_ref[...] = _small_fn(sm, col < 8, col < 12, -jnp.exp(pc[1:2]))
    pr = pr_ref[...]
    row = lax.broadcasted_iota(jnp.int32, (N_SMALL, 1), 0)
    smt = _dot_nt(wsmt_ref[...], h) + pr[:, 0:1]
    smt_ref[...] = _small_fn(smt, row < 8, row < 12, -jnp.exp(pr[:, 1:2]))


def _inproj(x, norm_g, w, layer, *, tm):
    m, d = x.shape
    n_gate = w["gate"].shape[2]
    n_gd = w["gd"].shape[2]
    row_blk = lambda n: pl.BlockSpec((tm, n), lambda i: (i, 0))
    wspec = lambda a: _const_spec((None,) + a.shape[1:], lambda i: (layer, 0, 0))
    out_shape = (
        jax.ShapeDtypeStruct((m, 3 * FOX_W), BF16),
        jax.ShapeDtypeStruct((m, FOX_W), F32),
        jax.ShapeDtypeStruct((m, FOX_W), F32),
        jax.ShapeDtypeStruct((m, RW_COLS), F32),
        jax.ShapeDtypeStruct((m, GD_QKV), F32),
        jax.ShapeDtypeStruct((m, n_gd - GD_QKV), F32),
        jax.ShapeDtypeStruct((m, n_gate), F32),
        jax.ShapeDtypeStruct((m, N_SMALL), F32),
        jax.ShapeDtypeStruct((N_SMALL, m), F32),
    )
    out_specs = (
        row_blk(3 * FOX_W), row_blk(FOX_W), row_blk(FOX_W), row_blk(RW_COLS), row_blk(GD_QKV),
        row_blk(n_gd - GD_QKV), row_blk(n_gate), row_blk(N_SMALL),
        pl.BlockSpec((N_SMALL, tm), lambda i: (0, i)),
    )
    return pl.pallas_call(
        _inproj_body,
        grid=(m // tm,),
        in_specs=[
            pl.BlockSpec((tm, d), lambda i: (i, 0)),
            _const_spec((None, 1, d), lambda i: (layer, 0, 0)),
            wspec(w["fox"]), wspec(w["rw"]), wspec(w["gd"]), wspec(w["gate"]),
            wspec(w["small"]), wspec(w["small_t"]), wspec(w["small_pc"]), wspec(w["small_pr"]),
        ],
        out_specs=out_specs,
        out_shape=out_shape,
        compiler_params=_cparams("parallel"),
        name="inproj",
    )(x, norm_g, w["fox"], w["rw"], w["gd"], w["gate"], w["small"], w["small_t"],
      w["small_pc"], w["small_pr"])


def _fox_prompt_body(q_ref, k_ref, v_ref, lft_ref, o_ref, crow_ref, m_ref, l_ref, acc_ref, *, tq, cb):
    i = pl.program_id(1)
    seq = k_ref.shape[0]

    @pl.when(i == 0)
    def _():
        upper = _tri(cb, "upper_incl").astype(BF16)
        carry_r = jnp.zeros((N_SMALL, 1), F32)
        for b0 in range(0, seq, cb):
            cr = _dot_sel_r(lft_ref[:, b0:b0 + cb], upper) + carry_r
            crow_ref[:, b0:b0 + cb] = cr
            carry_r = cr[:, cb - 1:cb]

    m_ref[...] = jnp.full(m_ref.shape, -jnp.inf, F32)
    l_ref[...] = jnp.zeros(l_ref.shape, F32)
    acc_ref[...] = jnp.zeros(acc_ref.shape, F32)

    def tile(j, diagonal):
        k0 = pl.multiple_of(j * tq, tq)
        if diagonal:
            causal = _tri(tq, "incl")
        for h in range(FOX_HEADS):
            hs = slice(h * HEAD_DIM, (h + 1) * HEAD_DIM)
            s = _dot_nt(q_ref[:, hs], k_ref[pl.ds(k0, tq), hs]) - crow_ref[h:h + 1, pl.ds(k0, tq)]
            if diagonal:
                s = jnp.where(causal, s, -jnp.inf)
            m_old = m_ref[h]
            m_new = jnp.maximum(m_old, jnp.max(s, axis=-1, keepdims=True))
            alpha = jnp.exp(m_old - m_new)
            p = jnp.exp(s - m_new)
            m_ref[h] = m_new
            l_ref[h] = alpha * l_ref[h] + jnp.sum(p, axis=-1, keepdims=True)
            acc_ref[h] = alpha * acc_ref[h] + _dot(p.astype(BF16), v_ref[pl.ds(k0, tq), hs])

    def body(j, carry):
        tile(j, False)
        return carry

    lax.fori_loop(0, i, body, 0)
    tile(i, True)
    for h in range(FOX_HEADS):
        o_ref[:, h * HEAD_DIM:(h + 1) * HEAD_DIM] = (acc_ref[h] / l_ref[h]).astype(BF16)


def _fox_prompt(qkvb, smt, batch, seq, *, tq):
    m = batch * seq
    nq = seq // tq
    cb = min(256, seq)
    body = functools.partial(_fox_prompt_body, tq=tq, cb=cb)
    return pl.pallas_call(
        body,
        grid=(batch, nq),
        in_specs=[
            pl.BlockSpec((tq, FOX_W), lambda b, i: (b * nq + i, 0)),
            pl.BlockSpec((seq, FOX_W), lambda b, i: (b, 1)),
            pl.BlockSpec((seq, FOX_W), lambda b, i: (b, 2)),
            pl.BlockSpec((N_SMALL, seq), lambda b, i: (0, b)),
        ],
        out_specs=pl.BlockSpec((tq, FOX_W), lambda b, i: (b * nq + i, 0)),
        out_shape=jax.ShapeDtypeStruct((m, FOX_W), BF16),
        scratch_shapes=[pltpu.VMEM((N_SMALL, seq), F32),
                        pltpu.VMEM((FOX_HEADS, tq, 1), F32), pltpu.VMEM((FOX_HEADS, tq, 1), F32),
                        pltpu.VMEM((FOX_HEADS, tq, HEAD_DIM), F32)],
        compiler_params=_cparams("parallel", "arbitrary"),
        name="fox_prompt",
    )(qkvb, qkvb, qkvb, smt)


def _fox_sample_body(pt_ref, q_ref, kn_ref, vn_ref, lfn_ref, *rest, t_valid, pps):
    kp_refs = rest[0:pps]
    vp_refs = rest[pps:2 * pps]
    lf_refs = rest[2 * pps:3 * pps]
    o_ref, q2_ref, m_ref, l_ref, acc_ref, tail_ref = rest[3 * pps:]
    s_idx = pl.program_id(1)
    tp = SUBLANES
    nrow = FOX_HEADS * tp
    page = kp_refs[0].shape[0]
    flat = page * FOX_HEADS

    @pl.when(s_idx == 0)
    def _():
        q = q_ref[...]
        diag = _head_mask(nrow, tp, FOX_W, HEAD_DIM)
        qbd = jnp.where(diag, jnp.concatenate([q] * FOX_HEADS, axis=0), jnp.zeros((), BF16))
        qf = q.astype(F32)
        q2_ref[...] = jnp.concatenate(
            [qf[:, h * HEAD_DIM:(h + 1) * HEAD_DIM] for h in range(FOX_HEADS)], axis=0).astype(BF16)
        upper = _tri(tp, "upper_incl").astype(BF16)
        lf_hi, lf_mid, lf_lo = _split3(lfn_ref[...])
        cn = (_dot_tn(lf_hi, upper) + _dot_tn(lf_mid, upper) + _dot_tn(lf_lo, upper))[:FOX_HEADS]
        cn_rows = jnp.broadcast_to(cn[:, None, :], (FOX_HEADS, tp, tp)).reshape(nrow, tp)
        s = _dot_nt(qbd, kn_ref[...]) - cn_rows
        t_q = lax.broadcasted_iota(jnp.int32, (nrow, tp), 0) % tp
        u_k = lax.broadcasted_iota(jnp.int32, (nrow, tp), 1)
        s = jnp.where((u_k <= t_q) & (u_k < t_valid), s, -jnp.inf)
        m = jnp.max(s, axis=-1, keepdims=True)
        p = jnp.exp(s - m)
        m_ref[...] = m
        l_ref[...] = jnp.sum(p, axis=-1, keepdims=True)
        wide = _dot(p.astype(BF16), vn_ref[...])
        acc_ref[...] = jnp.concatenate(
            [wide[h * tp:(h + 1) * tp, h * HEAD_DIM:(h + 1) * HEAD_DIM] for h in range(FOX_HEADS)], axis=0)
        tail_ref[...] = jnp.zeros_like(tail_ref)

    @pl.when(s_idx > 0)
    def _():
        lane = lax.broadcasted_iota(jnp.int32, (SUBLANES, flat), 1)
        own = (lax.broadcasted_iota(jnp.int32, (nrow, flat), 1) % FOX_HEADS
               == lax.broadcasted_iota(jnp.int32, (nrow, flat), 0) // tp)
        q2 = q2_ref[...]
        tail = tail_ref[...]
        scores = [None] * pps
        for r in reversed(range(pps)):
            lf = jnp.broadcast_to(lf_refs[r][...], (SUBLANES, flat))
            incl = lf
            tot = lf
            sft = FOX_HEADS
            while sft < flat:
                nxt = pltpu.roll(incl, flat - sft, axis=1)
                incl = incl + jnp.where(lane < flat - sft, nxt, 0.0)
                tot = tot + pltpu.roll(tot, sft, axis=1)
                sft *= 2
            suf = incl - lf + tail
            tail = tail + tot
            kf = kp_refs[r][...].reshape(flat, HEAD_DIM).astype(BF16)
            scores[r] = jnp.where(own, _dot_nt(q2, kf) + suf[0:1], -jnp.inf)
        tail_ref[...] = tail
        m_old = m_ref[...]
        m_new = m_old
        for r in range(pps):
            m_new = jnp.maximum(m_new, jnp.max(scores[r], axis=-1, keepdims=True))
        alpha = jnp.exp(m_old - m_new)
        l_new = alpha * l_ref[...]
        acc = alpha * acc_ref[...]
        for r in range(pps):
            p = jnp.exp(scores[r] - m_new)
            l_new = l_new + jnp.sum(p, axis=-1, keepdims=True)
            vf = vp_refs[r][...].reshape(flat, HEAD_DIM).astype(BF16)
            acc = acc + _dot(p.astype(BF16), vf)
        m_ref[...] = m_new
        l_ref[...] = l_new
        acc_ref[...] = acc

    @pl.when(s_idx == pl.num_programs(1) - 1)
    def _():
        o = acc_ref[...] / l_ref[...]
        o_ref[...] = jnp.concatenate([o[h * tp:(h + 1) * tp] for h in range(FOX_HEADS)], axis=1).astype(BF16)


def _fox_sample(qkvb, sm, cache_k, cache_v, cache_lf, page_table, layer, *, t_valid, pps):
    tp = SUBLANES
    bd, n_pages = page_table.shape
    page = cache_k.shape[2]
    nrow = FOX_HEADS * tp
    assert n_pages % pps == 0

    def page_spec(r, blk):
        def index_map(b, s, pt):
            return (layer, pt[b, n_pages - jnp.maximum(s, 1) * pps + r]) + (0,) * (len(blk) - 2)
        return pl.BlockSpec(blk, index_map)

    kv_blk = (None, None, page, FOX_HEADS, HEAD_DIM)
    lf_blk = (None, None, 1, page * FOX_HEADS)
    grid_spec = pltpu.PrefetchScalarGridSpec(
        num_scalar_prefetch=1,
        grid=(bd, n_pages // pps + 1),
        in_specs=[
            pl.BlockSpec((tp, FOX_W), lambda b, s, pt: (b, 0)),
            pl.BlockSpec((tp, FOX_W), lambda b, s, pt: (b, 1)),
            pl.BlockSpec((tp, FOX_W), lambda b, s, pt: (b, 2)),
            pl.BlockSpec((tp, N_SMALL), lambda b, s, pt: (b, 0)),
            *[page_spec(r, kv_blk) for r in range(pps)],
            *[page_spec(r, kv_blk) for r in range(pps)],
            *[page_spec(r, lf_blk) for r in range(pps)],
        ],
        out_specs=pl.BlockSpec((tp, FOX_W), lambda b, s, pt: (b, 0)),
        scratch_shapes=[
            pltpu.VMEM((nrow, HEAD_DIM), BF16),
            pltpu.VMEM((nrow, 1), F32),
            pltpu.VMEM((nrow, 1), F32),
            pltpu.VMEM((nrow, HEAD_DIM), F32),
            pltpu.VMEM((SUBLANES, page * FOX_HEADS), F32),
        ],
    )
    body = functools.partial(_fox_sample_body, t_valid=t_valid, pps=pps)
    return pl.pallas_call(
        body,
        grid_spec=grid_spec,
        out_shape=jax.ShapeDtypeStruct((bd * tp, FOX_W), BF16),
        compiler_params=_cparams("parallel", "arbitrary"),
        name="fox_sample",
    )(page_table, qkvb, qkvb, qkvb, sm, *([cache_k] * pps), *([cache_v] * pps), *([cache_lf] * pps))


def _rwkv_body(z_ref, shift_ref, s0_ref, mu_ref, pv_ref, wup_ref, aup_ref, gup_ref, seg_ref,
               o_ref, so_ref, sbd_ref, prev_ref, *, chunk, n_chunks, t_valid):
    c = pl.program_id(1)
    tb = chunk * n_chunks
    rows_x = GROUP * chunk
    lane_mask = _head_mask(rows_x, chunk, RW_GW, HEAD_DIM)
    state_mask = _head_mask(RW_GW, HEAD_DIM, RW_GW, HEAD_DIM)

    @pl.when(c == 0)
    def _():
        prev_ref[0:1, :] = shift_ref[...]
        for g in range(RW_GROUPS):
            blocks = [jnp.concatenate([s0_ref[g * GROUP + h]] * GROUP, axis=1) for h in range(GROUP)]
            sbd_ref[g] = jnp.where(state_mask, jnp.concatenate(blocks, axis=0), 0.0)

    z = z_ref[...]
    row = lax.broadcasted_iota(jnp.int32, (tb, 1), 0)
    z_prev = jnp.where(row == 0, prev_ref[0:1, :], pltpu.roll(z, 1, axis=0))
    prev_ref[0:1, :] = z[tb - 1:tb, :]
    zs = z + (z_prev - z) * mu_ref[...]
    r = zs[:, 0:RW_W]
    k = zs[:, RW_W:2 * RW_W]
    v = zs[:, 2 * RW_W:3 * RW_W]
    o_l = 3 * RW_W
    wd = zs[:, o_l:o_l + RW_LORA_W]
    ad = zs[:, o_l + RW_LORA_W:o_l + RW_LORA_W + RW_LORA_A]
    gd = zs[:, o_l + RW_LORA_W + RW_LORA_A:]
    pv = pv_ref[...]
    w_log = -_softplus(-(pv[0:1] + _dot(jnp.tanh(wd).astype(BF16), wup_ref[...]))) - 0.5
    lw = -jnp.exp(w_log)
    a = _sigmoid(pv[1:2] + _dot(ad.astype(BF16), aup_ref[...]))
    g_out = _dot(_sigmoid(gd).astype(BF16), gup_ref[...])
    kk = k * pv[2:3]
    k = k * (1.0 + (a - 1.0) * pv[3:4])
    seg2 = seg_ref[...]
    kk = kk * lax.rsqrt(_seg_sum(kk * kk, seg2) + L2_EPS)
    if t_valid < chunk:
        valid = row < t_valid
        lw = jnp.where(valid, lw, 0.0)
        kk = jnp.where(valid, kk, 0.0)
        k = jnp.where(valid, k, 0.0)
    cs = _dot_sel_l(_block_tri(tb, chunk, strict=False).astype(BF16), lw)
    p_inv = jnp.exp(-cs)
    rp = (r * jnp.exp(cs)).astype(BF16)
    kkp = (kk * jnp.exp(cs - lw)).astype(BF16)
    bn = kk * a * p_inv
    kn = k * p_inv
    vb = v.astype(BF16)
    strict = _block_tri(rows_x, chunk, strict=True)
    incl2 = _block_tri(rows_x, chunk, strict=False, reps=2)
    y_rows = []
    for ci in range(n_chunks):
        rs = slice(ci * chunk, (ci + 1) * chunk)
        p_end = jnp.exp(cs[(ci + 1) * chunk - 1:(ci + 1) * chunk, :])
        bn_c = bn[rs]
        kn_c = kn[rs]
        bn_e = (bn_c * p_end).astype(BF16)
        kn_e = (kn_c * p_end).astype(BF16)
        bn_c = bn_c.astype(BF16)
        kn_c = kn_c.astype(BF16)
        y_groups = []
        for g in range(RW_GROUPS):
            gs = slice(g * RW_GW, (g + 1) * RW_GW)
            kkp_x = _expand(kkp[rs, gs], lane_mask)
            rp_x = _expand(rp[rs, gs], lane_mask)
            v_x = _expand(vb[rs, gs], lane_mask)
            lhs = jnp.concatenate([kkp_x, rp_x], axis=0)
            rhs = jnp.concatenate([bn_c[:, gs]] * GROUP + [kn_c[:, gs]] * GROUP, axis=0)
            a_all = _dot_nt(lhs, rhs)
            a_ab = jnp.where(strict, a_all[:rows_x, :rows_x], 0.0)
            a_ak = jnp.where(strict, a_all[:rows_x, rows_x:], 0.0).astype(BF16)
            a_r = jnp.where(incl2, a_all[rows_x:, :], 0.0).astype(BF16)
            t_m1 = _unit_lower_inverse_minus_eye(a_ab, chunk).astype(BF16)
            st = sbd_ref[g]
            stb = st.astype(BF16)
            rhs_sa = _dot_nt(kkp_x, stb) + _dot(a_ak, v_x)
            sa = -(rhs_sa + _dot(t_m1, rhs_sa.astype(BF16)))
            sa_v = jnp.concatenate([sa.astype(BF16), v_x], axis=0)
            y_x = _dot_nt(rp_x, stb) + _dot(a_r, sa_v)
            y_groups.append(_collapse(y_x, chunk))
            bk_e = jnp.concatenate([_expand(bn_e[:, gs], lane_mask), _expand(kn_e[:, gs], lane_mask)], axis=0)
            sbd_ref[g] = st * p_end[:, gs] + _dot_tn(sa_v, bk_e)
        y_rows.append(jnp.concatenate(y_groups, axis=1))
    y = y_rows[0] if n_chunks == 1 else jnp.concatenate(y_rows, axis=0)
    inv_n = 1.0 / HEAD_DIM
    mean = _seg_sum(y, seg2) * inv_n
    yc = y - mean
    var = _seg_sum(yc * yc, seg2) * inv_n
    yn = yc * lax.rsqrt(var + RW_GN_EPS) * pv[5:6] + pv[6:7]
    bonus = _seg_sum(r * k * pv[4:5], seg2) * v
    o_ref[...] = ((yn + bonus) * g_out).astype(BF16)

    @pl.when(c == pl.num_programs(1) - 1)
    def _():
        for g in range(RW_GROUPS):
            st = sbd_ref[g]
            for h in range(GROUP):
                hs = slice(h * HEAD_DIM, (h + 1) * HEAD_DIM)
                so_ref[g * GROUP + h] = st[hs, hs]


def _rwkv(zr, shift0, s0, w, layer, batch, seq, *, chunk, n_chunks, t_valid):
    m = batch * seq
    tb = chunk * n_chunks
    nc = seq // tb
    body = functools.partial(_rwkv_body, chunk=chunk, n_chunks=n_chunks, t_valid=t_valid)
    lspec = lambda a: _const_spec((None,) + a.shape[1:], lambda b, c: (layer, 0, 0))
    state_spec = pl.BlockSpec((None, RW_HEADS, HEAD_DIM, HEAD_DIM), lambda b, c: (b, 0, 0, 0))
    return pl.pallas_call(
        body,
        grid=(batch, nc),
        in_specs=[
            pl.BlockSpec((tb, RW_COLS), lambda b, c: (b * nc + c, 0)),
            pl.BlockSpec((None, 1, RW_COLS), lambda b, c: (b, 0, 0)),
            state_spec,
            lspec(w["mu"]), lspec(w["pv"]), lspec(w["w_up"]), lspec(w["a_up"]), lspec(w["g_up"]),
            _const_spec((2 * RW_W, RW_W), lambda b, c: (0, 0)),
        ],
        out_specs=(pl.BlockSpec((tb, RW_W), lambda b, c: (b * nc + c, 0)), state_spec),
        out_shape=(
            jax.ShapeDtypeStruct((m, RW_W), BF16),
            jax.ShapeDtypeStruct((batch, RW_HEADS, HEAD_DIM, HEAD_DIM), F32),
        ),
        scratch_shapes=[pltpu.VMEM((RW_GROUPS, RW_GW, RW_GW), F32), pltpu.VMEM((SUBLANES, RW_COLS), F32)],
        compiler_params=_cparams("parallel", "arbitrary"),
        name="rwkv",
    )(zr, shift0, s0, w["mu"], w["pv"], w["w_up"], w["a_up"], w["g_up"], w["seg2"])


def _gdn_body(x_ref, gz_ref, sm_ref, conv0_ref, s0_ref, cw_ref, ng_ref, seg_ref,
              o_ref, so_ref, sbd_ref, prev_ref, *, chunk, n_chunks, t_valid):
    c = pl.program_id(1)
    tb = chunk * n_chunks
    rows_x = GD_HEADS * chunk
    lane_mask = _head_mask(rows_x, chunk, GD_W, GD_DK)
    state_mask = _head_mask(GD_W, GD_DK, GD_W, GD_DV)

    @pl.when(c == 0)
    def _():
        prev_ref[...] = conv0_ref[...]
        blocks = [jnp.concatenate([s0_ref[h]] * GD_HEADS, axis=1) for h in range(GD_HEADS)]
        sbd_ref[...] = jnp.where(state_mask, jnp.concatenate(blocks, axis=0), 0.0)

    x = x_ref[...]
    prev = prev_ref[...]
    prev_ref[...] = x[tb - SUBLANES:, :]
    cw = cw_ref[...]
    row8 = lax.broadcasted_iota(jnp.int32, (SUBLANES, 1), 0)
    conv = x * cw[GD_CONV - 1:GD_CONV]
    for sft in range(1, GD_CONV):
        xr = pltpu.roll(x, sft, axis=0)
        top = jnp.where(row8 < sft, pltpu.roll(prev, sft, axis=0), xr[:SUBLANES])
        xs = top if tb == SUBLANES else jnp.concatenate([top, xr[SUBLANES:]], axis=0)
        conv = conv + xs * cw[GD_CONV - 1 - sft:GD_CONV - sft]
    act = conv * _sigmoid(conv)
    seg2 = seg_ref[...]
    q = act[:, :GD_W]
    k = act[:, GD_W:2 * GD_W]
    v = act[:, 2 * GD_W:]
    q = q * lax.rsqrt(_seg_sum(q * q, seg2) + L2_EPS) * (GD_DK ** -0.5)
    k = k * lax.rsqrt(_seg_sum(k * k, seg2) + L2_EPS)
    sm = sm_ref[...]
    beta = sm[:, 8:8 + GD_HEADS]
    g = sm[:, 12:12 + GD_HEADS]
    if t_valid < chunk:
        rowc = lax.broadcasted_iota(jnp.int32, (tb, 1), 0)
        beta = jnp.where(rowc < t_valid, beta, 0.0)
        g = jnp.where(rowc < t_valid, g, 0.0)
    gc = _dot_sel_l(_block_tri(tb, chunk, strict=False).astype(BF16), g)
    head_lanes = _head_mask(GD_HEADS, 1, GD_W, GD_DK).astype(BF16)
    beta_l = _dot_sel_r(beta, head_lanes)
    gc_l = _dot_sel_r(gc, head_lanes)
    egc_l = jnp.exp(gc_l)
    kb = k * beta_l
    vbeta = (v * beta_l).astype(BF16)
    kbg = (kb * egc_l).astype(BF16)
    qg = (q * egc_l).astype(BF16)
    kb = kb.astype(BF16)
    qb = q.astype(BF16)
    kbf = k.astype(BF16)
    strict = _block_tri(rows_x, chunk, strict=True)
    incl = _block_tri(rows_x, chunk, strict=False)
    pick = _head_mask(rows_x, chunk, GD_HEADS, 1)
    ones = jnp.ones((rows_x, GD_HEADS), BF16)
    o_rows = []
    for ci in range(n_chunks):
        rs = slice(ci * chunk, (ci + 1) * chunk)
        last = slice((ci + 1) * chunk - 1, (ci + 1) * chunk)
        glast_l = gc_l[last]
        kg = (k[rs] * jnp.exp(glast_l - gc_l[rs])).astype(BF16)
        px = jnp.where(pick, jnp.concatenate([gc[rs]] * GD_HEADS, axis=0), 0.0)
        p_hi, p_mid, p_lo = _split3(px)
        d_exp = _dot_nt(jnp.concatenate([p_hi, p_mid, p_lo, ones, ones, ones], axis=1),
                        jnp.concatenate([ones, ones, ones, -p_hi, -p_mid, -p_lo], axis=1))
        decay = jnp.exp(jnp.minimum(d_exp, 0.0))
        kb_x = _expand(kb[rs], lane_mask)
        q_x = _expand(qb[rs], lane_mask)
        k_t = jnp.concatenate([kbf[rs]] * GD_HEADS, axis=0)
        qk = _dot_nt(jnp.concatenate([kb_x, q_x], axis=0), k_t)
        mm = jnp.where(strict, qk[:rows_x] * decay, 0.0)
        amat = jnp.where(incl, qk[rows_x:] * decay, 0.0).astype(BF16)
        t_m1 = _unit_lower_inverse_minus_eye(mm, chunk).astype(BF16)
        vk = jnp.concatenate([_expand(vbeta[rs], lane_mask), _expand(kbg[rs], lane_mask)], axis=1)
        uw = vk.astype(F32) + _dot(t_m1, vk)
        st = sbd_ref[...]
        stb = st.astype(BF16)
        v_new = uw[:, :GD_W] - _dot(uw[:, GD_W:].astype(BF16), stb)
        vnb = v_new.astype(BF16)
        o_x = _dot(_expand(qg[rs], lane_mask), stb) + _dot(amat, vnb)
        o_rows.append(_collapse(o_x, chunk))
        sbd_ref[...] = st * jnp.exp(glast_l) + _dot_tn(_expand(kg, lane_mask), vnb)
    o = o_rows[0] if n_chunks == 1 else jnp.concatenate(o_rows, axis=0)
    o = o * lax.rsqrt(_seg_sum(o * o, seg2) * (1.0 / GD_DV) + EPS) * ng_ref[...]
    gz = gz_ref[...]
    o_ref[...] = (o * (gz * _sigmoid(gz))).astype(BF16)

    @pl.when(c == pl.num_programs(1) - 1)
    def _():
        st = sbd_ref[...]
        for h in range(GD_HEADS):
            so_ref[h] = st[h * GD_DK:(h + 1) * GD_DK, h * GD_DV:(h + 1) * GD_DV]


def _gdn(gqkv, gz, sm, conv0, s0, w, layer, batch, seq, *, chunk, n_chunks, t_valid):
    m = batch * seq
    tb = chunk * n_chunks
    nc = seq // tb
    body = functools.partial(_gdn_body, chunk=chunk, n_chunks=n_chunks, t_valid=t_valid)
    lspec = lambda a: _const_spec((None,) + a.shape[1:], lambda b, c: (layer, 0, 0))
    state_spec = pl.BlockSpec((None, GD_HEADS, GD_DK, GD_DV), lambda b, c: (b, 0, 0, 0))
    return pl.pallas_call(
        body,
        grid=(batch, nc),
        in_specs=[
            pl.BlockSpec((tb, GD_QKV), lambda b, c: (b * nc + c, 0)),
            pl.BlockSpec((tb, GD_HEADS * GD_DV), lambda b, c: (b * nc + c, 0)),
            pl.BlockSpec((tb, N_SMALL), lambda b, c: (b * nc + c, 0)),
            pl.BlockSpec((None, SUBLANES, GD_QKV), lambda b, c: (b, 0, 0)),
            state_spec,
            lspec(w["conv"]), lspec(w["norm_g"]),
            _const_spec((2 * GD_W, GD_W), lambda b, c: (0, 0)),
        ],
        out_specs=(pl.BlockSpec((tb, GD_HEADS * GD_DV), lambda b, c: (b * nc + c, 0)), state_spec),
        out_shape=(
            jax.ShapeDtypeStruct((m, GD_HEADS * GD_DV), BF16),
            jax.ShapeDtypeStruct((batch, GD_HEADS, GD_DK, GD_DV), F32),
        ),
        scratch_shapes=[pltpu.VMEM((GD_W, GD_W), F32), pltpu.VMEM((SUBLANES, GD_QKV), F32)],
        compiler_params=_cparams("parallel", "arbitrary"),
        name="gdn",
    )(gqkv, gz, sm, conv0, s0, w["conv"], w["norm_g"], w["seg2"])


def _merge_body(x_ref, of_ref, or_ref, og_ref, zg_ref, wf_ref, wr_ref, wg_ref, wo_ref, o_ref):
    d = x_ref.shape[1]
    merged = _sigmoid(zg_ref[:, 0:d]) * _dot(of_ref[...], wf_ref[...])
    merged = merged + _sigmoid(zg_ref[:, d:2 * d]) * _dot(or_ref[...], wr_ref[...])
    merged = merged + _sigmoid(zg_ref[:, 2 * d:3 * d]) * _dot(og_ref[...], wg_ref[...])
    o_ref[...] = x_ref[...] + _dot(merged.astype(BF16), wo_ref[...])


def _merge(x, o_fox, o_rw, o_gd, zg, w, layer, *, tm):
    m, d = x.shape
    row_blk = lambda n: pl.BlockSpec((tm, n), lambda i: (i, 0))
    wspec = lambda a: _const_spec((None,) + a.shape[1:], lambda i: (layer, 0, 0))
    return pl.pallas_call(
        _merge_body,
        grid=(m // tm,),
        in_specs=[row_blk(d), row_blk(FOX_W), row_blk(RW_W), row_blk(GD_HEADS * GD_DV),
                  row_blk(N_BRANCH * d),
                  wspec(w["br_fox"]), wspec(w["br_rw"]), wspec(w["br_gd"]), wspec(w["out"])],
        out_specs=row_blk(d),
        out_shape=jax.ShapeDtypeStruct((m, d), F32),
        compiler_params=_cparams("parallel"),
        name="merge",
    )(x, o_fox, o_rw, o_gd, zg, w["br_fox"], w["br_rw"], w["br_gd"], w["out"])


def _same_segment2(n, width):
    idx = jnp.arange(n) // width
    seg = (idx[:, None] == idx[None, :]).astype(BF16)
    return jnp.concatenate([seg, seg], axis=0)


def _row_tile(m):
    for tm in (512, 256, 128, 64, 32, 16, 8):
        if m % tm == 0:
            return tm
    raise ValueError(f"row count {m} is not a multiple of 8")


def _pages_per_step(n_pages):
    for pps in (4, 2, 1):
        if n_pages % pps == 0:
            return pps


def kernel(x_prompt, x_sample, cache_k, cache_v, cache_logf, state_rwkv_shift, state_rwkv,
           state_gdn_conv, state_gdn, page_table, norm_ffn1, ffn1_wg, ffn1_wu, ffn1_wd, norm_mix,
           w_in, fox_fb, rw_mu, rw_w0, rw_w_up, rw_a0, rw_a_up, rw_g_up, rw_kk, rw_ka, rw_rk,
           rw_ln_g, rw_ln_b, gd_conv, gd_a_log, gd_dt_bias, gd_norm_g, w_br_fox, w_br_rw, w_br_gd,
           w_out, norm_ffn2, ffn2_wg, ffn2_wu, ffn2_wd, final_norm):
    bp, seq, d = x_prompt.shape
    bd, t_s, _ = x_sample.shape
    depth = w_in.shape[0]
    tp = SUBLANES
    chunk_p = 64
    n_chunks_p = 2 if seq % (2 * chunk_p) == 0 else 1
    assert t_s <= tp and seq % chunk_p == 0

    o_rw_c = 3 * FOX_W + FOX_HEADS
    o_gd_c = o_rw_c + RW_COLS
    o_gz = o_gd_c + GD_QKV
    o_gb = o_gz + GD_HEADS * GD_DV
    o_ga = o_gb + GD_HEADS
    o_gate = o_ga + GD_HEADS
    scale = HEAD_DIM ** -0.5
    w_small = jnp.concatenate([w_in[:, :, 3 * FOX_W:o_rw_c], w_in[:, :, o_gb:o_gate]], axis=-1)
    zeros4 = jnp.zeros((depth, GD_HEADS), F32)
    small_bias = jnp.concatenate([fox_fb, zeros4, gd_dt_bias], axis=-1)
    small_alog = jnp.concatenate([jnp.zeros((depth, 8), F32), zeros4, gd_a_log], axis=-1)
    w_proj = {
        "fox": jnp.concatenate([w_in[:, :, :FOX_W] * scale, w_in[:, :, FOX_W:3 * FOX_W]], axis=-1).astype(BF16),
        "rw": w_in[:, :, o_rw_c:o_gd_c].astype(BF16),
        "gd": w_in[:, :, o_gd_c:o_gb].astype(BF16),
        "gate": w_in[:, :, o_gate:].astype(BF16),
        "small": w_small.astype(BF16),
        "small_t": jnp.swapaxes(w_small, 1, 2).astype(BF16),
        "small_pc": jnp.stack([small_bias, small_alog], axis=1),
        "small_pr": jnp.stack([small_bias, small_alog], axis=2),
    }
    w_rw = {
        "mu": rw_mu[:, None, :],
        "pv": jnp.stack([rw_w0, rw_a0, rw_kk, rw_ka, rw_rk.reshape(depth, RW_W), rw_ln_g, rw_ln_b,
                         jnp.zeros_like(rw_w0)], axis=1),
        "w_up": rw_w_up.astype(BF16), "a_up": rw_a_up.astype(BF16), "g_up": rw_g_up.astype(BF16),
        "seg2": _same_segment2(RW_W, HEAD_DIM),
    }
    w_gd = {
        "conv": jnp.pad(gd_conv, ((0, 0), (0, SUBLANES - GD_CONV), (0, 0))),
        "norm_g": jnp.tile(gd_norm_g, (1, GD_HEADS))[:, None, :],
        "seg2": _same_segment2(GD_W, GD_DK),
    }
    w_mg = {"br_fox": w_br_fox.astype(BF16), "br_rw": w_br_rw.astype(BF16),
            "br_gd": w_br_gd.astype(BF16), "out": w_out.astype(BF16)}
    ffn1 = (norm_ffn1[:, None, :], ffn1_wg.astype(BF16), ffn1_wu.astype(BF16), ffn1_wd.astype(BF16))
    ffn2 = (norm_ffn2[:, None, :], ffn2_wg.astype(BF16), ffn2_wu.astype(BF16), ffn2_wd.astype(BF16))
    norm_mix3 = norm_mix[:, None, :]
    final_g = final_norm[None, :]

    n_pool, page = cache_k.shape[1], cache_k.shape[2]
    clf = cache_logf.reshape(depth, n_pool, 1, page * FOX_HEADS)
    pps = _pages_per_step(page_table.shape[1])
    gd_conv0_s = jnp.pad(state_gdn_conv, ((0, 0), (0, 0), (SUBLANES - (GD_CONV - 1), 0), (0, 0)))
    zeros_p = {
        "shift": jnp.zeros((bp, 1, RW_COLS), F32),
        "rw_s": jnp.zeros((bp, RW_HEADS, HEAD_DIM, HEAD_DIM), F32),
        "conv": jnp.zeros((bp, SUBLANES, GD_QKV), F32),
        "gd_s": jnp.zeros((bp, GD_HEADS, GD_DK, GD_DV), F32),
    }

    xp = x_prompt.reshape(bp * seq, d)
    xs = jnp.pad(x_sample, ((0, 0), (0, tp - t_s), (0, 0))).reshape(bd * tp, d)
    tm_p, tm_s = _row_tile(bp * seq), _row_tile(bd * tp)
    tm_in_p = min(tm_p, 256)
    tq = 512 if seq % 512 == 0 else min(256, seq)

    def layer_fn(x, l, *, batch, t_len, tm, tm_in, fox_fn, shift0, rw_s0, conv0, gd_s0, chunk, n_chunks,
                 t_valid, last):
        x = _ffn(x, *ffn1, final_g, l, tm=tm)
        qkvb, fk, fv, zr, gqkv, gz, zg, sm, smt = _inproj(x, norm_mix3, w_proj, l, tm=tm_in)
        o_fox = fox_fn(qkvb, sm, smt, l)
        o_rw, rw_st = _rwkv(zr, shift0, rw_s0, w_rw, l, batch, t_len, chunk=chunk, n_chunks=n_chunks,
                            t_valid=t_valid)
        o_gd, gd_s = _gdn(gqkv, gz, sm, conv0, gd_s0, w_gd, l, batch, t_len, chunk=chunk, n_chunks=n_chunks,
                          t_valid=t_valid)
        x = _merge(x, o_fox, o_rw, o_gd, zg, w_mg, l, tm=tm)
        x = _ffn(x, *ffn2, final_g, l, tm=tm, final_norm=last)
        return x, (fk, fv, sm, zr, rw_st, gqkv, gd_s)

    p_states, s_states = [], []
    for l in range(depth):
        last = l == depth - 1
        xp, st = layer_fn(
            xp, l, batch=bp, t_len=seq, tm=tm_p, tm_in=tm_in_p,
            fox_fn=lambda qkvb, sm, smt, l: _fox_prompt(qkvb, smt, bp, seq, tq=tq),
            shift0=zeros_p["shift"], rw_s0=zeros_p["rw_s"], conv0=zeros_p["conv"], gd_s0=zeros_p["gd_s"],
            chunk=chunk_p, n_chunks=n_chunks_p, t_valid=chunk_p, last=last)
        fk, fv, sm, zr, rw_st, gqkv, gd_s = st
        p_states.append((
            fk.reshape(bp, seq, FOX_HEADS, HEAD_DIM), fv.reshape(bp, seq, FOX_HEADS, HEAD_DIM),
            sm[:, :FOX_HEADS].reshape(bp, seq, FOX_HEADS),
            zr.reshape(bp, seq, RW_COLS)[:, seq - 1],
            rw_st,
            gqkv.reshape(bp, seq, GD_QKV)[:, seq - (GD_CONV - 1):],
            gd_s))
        xs, st = layer_fn(
            xs, l, batch=bd, t_len=tp, tm=tm_s, tm_in=tm_s,
            fox_fn=lambda qkvb, sm, smt, l: _fox_sample(qkvb, sm, cache_k, cache_v, clf, page_table, l,
                                                        t_valid=t_s, pps=pps),
            shift0=state_rwkv_shift[l][:, None, :], rw_s0=state_rwkv[l], conv0=gd_conv0_s[l],
            gd_s0=state_gdn[l], chunk=tp, n_chunks=1, t_valid=t_s, last=last)
        fk, fv, sm, zr, rw_st, gqkv, gd_s = st
        conv_ext = jnp.concatenate([state_gdn_conv[l], gqkv.reshape(bd, tp, GD_QKV)[:, :t_s]], axis=1)
        s_states.append((
            fk.reshape(bd, tp, FOX_HEADS, HEAD_DIM)[:, :t_s], fv.reshape(bd, tp, FOX_HEADS, HEAD_DIM)[:, :t_s],
            sm[:, :FOX_HEADS].reshape(bd, tp, FOX_HEADS)[:, :t_s],
            zr.reshape(bd, tp, RW_COLS)[:, t_s - 1],
            rw_st,
            conv_ext[:, t_s:],
            gd_s))

    p_out = [jnp.stack(s) for s in zip(*p_states)]
    s_out = [jnp.stack(s) for s in zip(*s_states)]
    y_prompt = xp.reshape(bp, seq, d)
    y_sample = xs.reshape(bd, tp, d)[:, :t_s]
    return (y_prompt, y_sample, *p_out, *s_out)
```

```python
import functools

import jax
import jax.numpy as jnp
from jax import lax
from jax.experimental import pallas as pl
from jax.experimental.pallas import tpu as pltpu

F32 = jnp.float32
BF16 = jnp.bfloat16

HEAD_DIM = 64
FOX_HEADS = 8
FOX_W = FOX_HEADS * HEAD_DIM
RW_HEADS = 8
RW_W = RW_HEADS * HEAD_DIM
RW_LORA_W = 64
RW_LORA_A = 64
RW_LORA_G = 128
RW_COLS = 3 * RW_W + RW_LORA_W + RW_LORA_A + RW_LORA_G
RW_GN_EPS = 64e-5
GD_HEADS = 4
GD_DK = 128
GD_DV = 128
GD_W = GD_HEADS * GD_DK
GD_QKV = 2 * GD_W + GD_HEADS * GD_DV
GD_CONV = 4
N_BRANCH = 3
EPS = 1e-6
L2_EPS = 1e-6
N_SMALL = 16
SUBLANES = 8
GROUP = 4
RW_GROUPS = RW_HEADS // GROUP
RW_GW = GROUP * HEAD_DIM
VMEM_LIMIT = 56 * 1024 * 1024


def _cparams(*sem):
    return pltpu.CompilerParams(dimension_semantics=sem, vmem_limit_bytes=VMEM_LIMIT)


def _const_spec(shape, index_map):
    return pl.BlockSpec(shape, index_map, pipeline_mode=pl.Buffered(1))


def _dot(a, b):
    return jnp.dot(a, b, preferred_element_type=F32)


def _dot_nt(a, b):
    return lax.dot_general(a, b, (((1,), (1,)), ((), ())), preferred_element_type=F32)


def _dot_tn(a, b):
    return lax.dot_general(a, b, (((0,), (0,)), ((), ())), preferred_element_type=F32)


def _split2(x):
    hi = x.astype(BF16)
    lo = (x - hi.astype(F32)).astype(BF16)
    return hi, lo


def _split3(x):
    hi = x.astype(BF16)
    r = x - hi.astype(F32)
    mid = r.astype(BF16)
    lo = (r - mid.astype(F32)).astype(BF16)
    return hi, mid, lo


def _dot_sel_r(x, sel):
    hi, mid, lo = _split3(x)
    return _dot(hi, sel) + _dot(mid, sel) + _dot(lo, sel)


def _dot_sel_l(sel, x):
    return _dot(jnp.concatenate([sel] * 3, axis=1), jnp.concatenate(_split3(x), axis=0))


def _seg_sum(x, seg2):
    return _dot(jnp.concatenate(_split2(x), axis=1), seg2)


def _softplus(z):
    return jnp.maximum(z, 0.0) + jnp.log(1.0 + jnp.exp(-jnp.abs(z)))


def _sigmoid(z):
    return 1.0 / (1.0 + jnp.exp(-z))


def _rms(x, g):
    return x * lax.rsqrt(jnp.mean(x * x, axis=-1, keepdims=True) + EPS) * g


def _tri(n, kind):
    r = lax.broadcasted_iota(jnp.int32, (n, n), 0)
    c = lax.broadcasted_iota(jnp.int32, (n, n), 1)
    return {"incl": r >= c, "strict": r > c, "upper_incl": r <= c}[kind]


def _block_tri(n, blk, strict, reps=1):
    r = lax.broadcasted_iota(jnp.int32, (n, n * reps), 0)
    c = lax.broadcasted_iota(jnp.int32, (n, n * reps), 1) % n
    same = (r // blk) == (c // blk)
    return same & ((r > c) if strict else (r >= c))


def _head_mask(rows, row_blk, cols, col_blk):
    r = lax.broadcasted_iota(jnp.int32, (rows, cols), 0) // row_blk
    c = lax.broadcasted_iota(jnp.int32, (rows, cols), 1) // col_blk
    return r == c


def _expand(x, mask):
    return jnp.where(mask, jnp.concatenate([x] * GROUP, axis=0), jnp.zeros((), x.dtype))


def _collapse(x, t):
    out = x[0:t]
    for h in range(1, GROUP):
        out = out + x[h * t:(h + 1) * t]
    return out


def _unit_lower_inverse_minus_eye(m_strict, order):
    npow = -m_strict
    t = npow
    span = 2
    while span < order:
        nb = npow.astype(BF16)
        npow = _dot(nb, nb)
        t = t + npow + _dot(t.astype(BF16), npow.astype(BF16))
        span *= 2
    return t


def _ffn_body(x_ref, g_ref, wg_ref, wu_ref, wd_ref, fg_ref, o_ref, *, ff_chunk, final_norm):
    x = x_ref[...]
    h = _rms(x, g_ref[...]).astype(BF16)
    d_ff = wg_ref.shape[1]
    acc = jnp.zeros_like(x)
    for c0 in range(0, d_ff, ff_chunk):
        gate = _dot(h, wg_ref[:, c0:c0 + ff_chunk])
        up = _dot(h, wu_ref[:, c0:c0 + ff_chunk])
        act = (gate * _sigmoid(gate) * up).astype(BF16)
        acc = acc + _dot(act, wd_ref[c0:c0 + ff_chunk, :])
    y = x + 0.5 * acc
    if final_norm:
        y = _rms(y, fg_ref[...])
    o_ref[...] = y


def _ffn(x, norm_g, wg, wu, wd, final_g, layer, *, tm, final_norm=False):
    m, d = x.shape
    d_ff = wg.shape[2]
    ff_chunk = d_ff // 2 if (d_ff // 2) % 128 == 0 else d_ff
    body = functools.partial(_ffn_body, ff_chunk=ff_chunk, final_norm=final_norm)
    return pl.pallas_call(
        body,
        grid=(m // tm,),
        in_specs=[
            pl.BlockSpec((tm, d), lambda i: (i, 0)),
            _const_spec((None, 1, d), lambda i: (layer, 0, 0)),
            _const_spec((None, d, d_ff), lambda i: (layer, 0, 0)),
            _const_spec((None, d, d_ff), lambda i: (layer, 0, 0)),
            _const_spec((None, d_ff, d), lambda i: (layer, 0, 0)),
            _const_spec((1, d), lambda i: (0, 0)),
        ],
        out_specs=pl.BlockSpec((tm, d), lambda i: (i, 0)),
        out_shape=jax.ShapeDtypeStruct((m, d), F32),
        compiler_params=_cparams("parallel"),
        name="ffn",
    )(x, norm_g, wg, wu, wd, final_g)


def _small_fn(z, is_logf, is_beta, neg_exp_a):
    return jnp.where(is_logf, -_softplus(-z), jnp.where(is_beta, _sigmoid(z), neg_exp_a * _softplus(z)))


def _inproj_body(x_ref, g_ref, wfox_ref, wrw_ref, wgd_ref, wgate_ref, wsm_ref, wsmt_ref,
                 pc_ref, pr_ref,
                 qkvb_ref, fk_ref, fv_ref, zr_ref, gqkv_ref, gz_ref, zg_ref, sm_ref, smt_ref):
    h = _rms(x_ref[...], g_ref[...]).astype(BF16)
    fox = _dot(h, wfox_ref[...])
    qkvb_ref[...] = fox.astype(BF16)
    fk_ref[...] = fox[:, FOX_W:2 * FOX_W]
    fv_ref[...] = fox[:, 2 * FOX_W:]
    zr_ref[...] = _dot(h, wrw_ref[...])
    gd = _dot(h, wgd_ref[...])
    gqkv_ref[...] = gd[:, :GD_QKV]
    gz_ref[...] = gd[:, GD_QKV:]
    zg_ref[...] = _dot(h, wgate_ref[...])
    pc = pc_ref[...]
    col = lax.broadcasted_iota(jnp.int32, (1, N_SMALL), 1)
    sm = _dot(h, wsm_ref[...]) + pc[0:1]
    sm_ref[...] = _small_fn(sm, col < 8, col < 12, -jnp.exp(pc[1:2]))
    pr = pr_ref[...]
    row = lax.broadcasted_iota(jnp.int32, (N_SMALL, 1), 0)
    smt = _dot_nt(wsmt_ref[...], h) + pr[:, 0:1]
    smt_ref[...] = _small_fn(smt, row < 8, row < 12, -jnp.exp(pr[:, 1:2]))


def _inproj(x, norm_g, w, layer, *, tm):
    m, d = x.shape
    n_gate = w["gate"].shape[2]
    n_gd = w["gd"].shape[2]
    row_blk = lambda n: pl.BlockSpec((tm, n), lambda i: (i, 0))
    wspec = lambda a: _const_spec((None,) + a.shape[1:], lambda i: (layer, 0, 0))
    out_shape = (
        jax.ShapeDtypeStruct((m, 3 * FOX_W), BF16),
        jax.ShapeDtypeStruct((m, FOX_W), F32),
        jax.ShapeDtypeStruct((m, FOX_W), F32),
        jax.ShapeDtypeStruct((m, RW_COLS), F32),
        jax.ShapeDtypeStruct((m, GD_QKV), F32),
        jax.ShapeDtypeStruct((m, n_gd - GD_QKV), F32),
        jax.ShapeDtypeStruct((m, n_gate), F32),
        jax.ShapeDtypeStruct((m, N_SMALL), F32),
        jax.ShapeDtypeStruct((N_SMALL, m), F32),
    )
    out_specs = (
        row_blk(3 * FOX_W), row_blk(FOX_W), row_blk(FOX_W), row_blk(RW_COLS), row_blk(GD_QKV),
        row_blk(n_gd - GD_QKV), row_blk(n_gate), row_blk(N_SMALL),
        pl.BlockSpec((N_SMALL, tm), lambda i: (0, i)),
    )
    return pl.pallas_call(
        _inproj_body,
        grid=(m // tm,),
        in_specs=[
            pl.BlockSpec((tm, d), lambda i: (i, 0)),
            _const_spec((None, 1, d), lambda i: (layer, 0, 0)),
            wspec(w["fox"]), wspec(w["rw"]), wspec(w["gd"]), wspec(w["gate"]),
            wspec(w["small"]), wspec(w["small_t"]), wspec(w["small_pc"]), wspec(w["small_pr"]),
        ],
        out_specs=out_specs,
        out_shape=out_shape,
        compiler_params=_cparams("parallel"),
        name="inproj",
    )(x, norm_g, w["fox"], w["rw"], w["gd"], w["gate"], w["small"], w["small_t"],
      w["small_pc"], w["small_pr"])


FOX_AW = 2 * HEAD_DIM
N_PIECES = 3


def _eye(n):
    return (lax.broadcasted_iota(jnp.int32, (n, n), 0) == lax.broadcasted_iota(jnp.int32, (n, n), 1)).astype(BF16)


def _fox_prompt_body(q_ref, k_ref, v_ref, lf_ref, pk_ref, pc_ref, one_ref, o_ref,
                     kaug_ref, vt_ref, qaug_ref, m_ref, l_ref, acc_ref, *, tq, cb):
    i = pl.program_id(1)
    seq = k_ref.shape[0]

    @pl.when(i == 0)
    def _():
        lower = _tri(cb, "incl").astype(BF16)
        eye = _eye(FOX_W)
        carry = jnp.zeros((1, N_SMALL), F32)
        for b0 in range(0, seq, cb):
            cc = _dot_sel_l(lower, lf_ref[b0:b0 + cb, :]) + carry
            carry = cc[cb - 1:cb, :]
            pieces = jnp.concatenate(_split3(-cc), axis=1)
            kaug_ref[b0:b0 + cb, :] = (_dot(k_ref[b0:b0 + cb, :], pk_ref[...])
                                       + _dot(pieces, pc_ref[...])).astype(BF16)
            vt_ref[:, b0:b0 + cb] = _dot_nt(eye, v_ref[b0:b0 + cb, :]).astype(BF16)

    qaug_ref[...] = (_dot(q_ref[...], pk_ref[...]) + one_ref[...]).astype(BF16)
    m_ref[...] = jnp.full(m_ref.shape, -jnp.inf, F32)
    l_ref[...] = jnp.zeros(l_ref.shape, F32)
    acc_ref[...] = jnp.zeros(acc_ref.shape, F32)

    def tile(j, diagonal):
        k0 = pl.multiple_of(j * tq, tq)
        if diagonal:
            visible = _tri(tq, "upper_incl")
        for h in range(FOX_HEADS):
            hs = slice(h * HEAD_DIM, (h + 1) * HEAD_DIM)
            aw = slice(h * FOX_AW, (h + 1) * FOX_AW)
            st = _dot_nt(kaug_ref[pl.ds(k0, tq), aw], qaug_ref[:, aw])
            if diagonal:
                st = jnp.where(visible, st, -jnp.inf)
            m_old = m_ref[h]
            m_new = jnp.maximum(m_old, jnp.max(st, axis=0, keepdims=True))
            alpha = jnp.exp(m_old - m_new)
            p = jnp.exp(st - m_new)
            m_ref[h] = m_new
            l_ref[h] = alpha * l_ref[h] + jnp.sum(p, axis=0, keepdims=True)
            acc_ref[hs, :] = alpha * acc_ref[hs, :] + _dot(vt_ref[hs, pl.ds(k0, tq)], p.astype(BF16))

    def body(j, carry):
        tile(j, False)
        return carry

    lax.fori_loop(0, i, body, 0)
    tile(i, True)
    out_t = jnp.concatenate(
        [acc_ref[h * HEAD_DIM:(h + 1) * HEAD_DIM, :] / l_ref[h] for h in range(FOX_HEADS)], axis=0)
    o_ref[...] = _dot_nt(_eye(tq), out_t.astype(BF16)).astype(BF16)


def _fox_prompt(qkvb, sm, consts, batch, seq, *, tq):
    m = batch * seq
    nq = seq // tq
    cb = min(512, seq)
    body = functools.partial(_fox_prompt_body, tq=tq, cb=cb)
    aug_w = FOX_HEADS * FOX_AW
    return pl.pallas_call(
        body,
        grid=(batch, nq),
        in_specs=[
            pl.BlockSpec((tq, FOX_W), lambda b, i: (b * nq + i, 0)),
            pl.BlockSpec((seq, FOX_W), lambda b, i: (b, 1)),
            pl.BlockSpec((seq, FOX_W), lambda b, i: (b, 2)),
            pl.BlockSpec((seq, N_SMALL), lambda b, i: (b, 0)),
            _const_spec((FOX_W, aug_w), lambda b, i: (0, 0)),
            _const_spec((N_PIECES * N_SMALL, aug_w), lambda b, i: (0, 0)),
            _const_spec((1, aug_w), lambda b, i: (0, 0)),
        ],
        out_specs=pl.BlockSpec((tq, FOX_W), lambda b, i: (b * nq + i, 0)),
        out_shape=jax.ShapeDtypeStruct((m, FOX_W), BF16),
        scratch_shapes=[pltpu.VMEM((seq, aug_w), BF16), pltpu.VMEM((FOX_W, seq), BF16),
                        pltpu.VMEM((tq, aug_w), BF16),
                        pltpu.VMEM((FOX_HEADS, 1, tq), F32), pltpu.VMEM((FOX_HEADS, 1, tq), F32),
                        pltpu.VMEM((FOX_W, tq), F32)],
        compiler_params=_cparams("parallel", "arbitrary"),
        name="fox_prompt",
    )(qkvb, qkvb, qkvb, sm, consts["place_k"], consts["place_c"], consts["ones"])


def _fox_prompt_consts():
    aug_w = FOX_HEADS * FOX_AW
    lane = jnp.arange(aug_w)
    src = jnp.arange(FOX_W)
    place_k = (lane[None, :] == (src // HEAD_DIM * FOX_AW + src % HEAD_DIM)[:, None]).astype(BF16)
    col = jnp.arange(N_PIECES * N_SMALL)
    tgt = jnp.where(col % N_SMALL < FOX_HEADS, (col % N_SMALL) * FOX_AW + HEAD_DIM + col // N_SMALL, -1)
    place_c = (lane[None, :] == tgt[:, None]).astype(BF16)
    ones = ((lane % FOX_AW >= HEAD_DIM) & (lane % FOX_AW < HEAD_DIM + N_PIECES)).astype(F32)[None, :]
    return {"place_k": place_k, "place_c": place_c, "ones": ones}


def _fox_sample_body(pt_ref, q_ref, kn_ref, vn_ref, lfn_ref, *rest, t_valid, pps):
    kp_refs = rest[0:pps]
    vp_refs = rest[pps:2 * pps]
    lf_refs = rest[2 * pps:3 * pps]
    o_ref, qbd_ref, m_ref, l_ref, acc_ref, tail_ref = rest[3 * pps:]
    s_idx = pl.program_id(1)
    tp = SUBLANES
    nrow = FOX_HEADS * tp
    page = kp_refs[0].shape[1]
    diag = _head_mask(nrow, tp, FOX_W, HEAD_DIM)

    def expand_rows(x):
        return jnp.broadcast_to(x[:, None, :], (FOX_HEADS, tp, x.shape[1])).reshape(nrow, x.shape[1])

    @pl.when(s_idx == 0)
    def _():
        q = q_ref[...]
        qbd = jnp.where(diag, jnp.concatenate([q] * FOX_HEADS, axis=0), jnp.zeros((), BF16))
        qbd_ref[...] = qbd
        upper = _tri(tp, "upper_incl").astype(BF16)
        lf_hi, lf_mid, lf_lo = _split3(lfn_ref[...])
        cn = (_dot_tn(lf_hi, upper) + _dot_tn(lf_mid, upper) + _dot_tn(lf_lo, upper))[:FOX_HEADS]
        s = _dot_nt(qbd, kn_ref[...]) - expand_rows(cn)
        t_q = lax.broadcasted_iota(jnp.int32, (nrow, tp), 0) % tp
        u_k = lax.broadcasted_iota(jnp.int32, (nrow, tp), 1)
        s = jnp.where((u_k <= t_q) & (u_k < t_valid), s, -jnp.inf)
        m = jnp.max(s, axis=-1, keepdims=True)
        p = jnp.exp(s - m)
        m_ref[...] = m
        l_ref[...] = jnp.sum(p, axis=-1, keepdims=True)
        acc_ref[...] = _dot(p.astype(BF16), vn_ref[...])
        tail_ref[...] = jnp.zeros_like(tail_ref)

    @pl.when(s_idx > 0)
    def _():
        lane = lax.broadcasted_iota(jnp.int32, (FOX_HEADS, page), 1)
        qbd = qbd_ref[...]
        tail = tail_ref[...]
        scores = [None] * pps
        for r in reversed(range(pps)):
            lf = lf_refs[r][...]
            incl = lf
            sft = 1
            while sft < page:
                nxt = pltpu.roll(incl, page - sft, axis=1)
                incl = incl + jnp.where(lane < page - sft, nxt, 0.0)
                sft *= 2
            suf = incl - lf + tail
            tail = tail + jnp.sum(lf, axis=-1, keepdims=True)
            scores[r] = _dot(qbd, kp_refs[r][...].astype(BF16)) + expand_rows(suf)
        tail_ref[...] = tail
        m_old = m_ref[...]
        m_new = m_old
        for r in range(pps):
            m_new = jnp.maximum(m_new, jnp.max(scores[r], axis=-1, keepdims=True))
        alpha = jnp.exp(m_old - m_new)
        l_new = alpha * l_ref[...]
        acc = alpha * acc_ref[...]
        for r in range(pps):
            p = jnp.exp(scores[r] - m_new)
            l_new = l_new + jnp.sum(p, axis=-1, keepdims=True)
            acc = acc + _dot_nt(p.astype(BF16), vp_refs[r][...].astype(BF16))
        m_ref[...] = m_new
        l_ref[...] = l_new
        acc_ref[...] = acc

    @pl.when(s_idx == pl.num_programs(1) - 1)
    def _():
        o = jnp.where(diag, acc_ref[...] / l_ref[...], 0.0)
        out = o[0:tp]
        for h in range(1, FOX_HEADS):
            out = out + o[h * tp:(h + 1) * tp]
        o_ref[...] = out.astype(BF16)


def _fox_sample(qkvb, sm, cache_kt, cache_vt, cache_lft, page_table, layer, *, t_valid, pps):
    tp = SUBLANES
    bd, n_pages = page_table.shape
    page = cache_kt.shape[3]
    nrow = FOX_HEADS * tp
    assert n_pages % pps == 0

    def page_spec(r, blk):
        def index_map(b, s, pt):
            return (layer, pt[b, n_pages - jnp.maximum(s, 1) * pps + r]) + (0,) * (len(blk) - 2)
        return pl.BlockSpec(blk, index_map)

    kv_blk = (None, None, FOX_W, page)
    lf_blk = (None, None, FOX_HEADS, page)
    grid_spec = pltpu.PrefetchScalarGridSpec(
        num_scalar_prefetch=1,
        grid=(bd, n_pages // pps + 1),
        in_specs=[
            pl.BlockSpec((tp, FOX_W), lambda b, s, pt: (b, 0)),
            pl.BlockSpec((tp, FOX_W), lambda b, s, pt: (b, 1)),
            pl.BlockSpec((tp, FOX_W), lambda b, s, pt: (b, 2)),
            pl.BlockSpec((tp, N_SMALL), lambda b, s, pt: (b, 0)),
            *[page_spec(r, kv_blk) for r in range(pps)],
            *[page_spec(r, kv_blk) for r in range(pps)],
            *[page_spec(r, lf_blk) for r in range(pps)],
        ],
        out_specs=pl.BlockSpec((tp, FOX_W), lambda b, s, pt: (b, 0)),
        scratch_shapes=[
            pltpu.VMEM((nrow, FOX_W), BF16),
            pltpu.VMEM((nrow, 1), F32),
            pltpu.VMEM((nrow, 1), F32),
            pltpu.VMEM((nrow, FOX_W), F32),
            pltpu.VMEM((FOX_HEADS, 1), F32),
        ],
    )
    body = functools.partial(_fox_sample_body, t_valid=t_valid, pps=pps)
    return pl.pallas_call(
        body,
        grid_spec=grid_spec,
        out_shape=jax.ShapeDtypeStruct((bd * tp, FOX_W), BF16),
        compiler_params=_cparams("parallel", "arbitrary"),
        name="fox_sample",
    )(page_table, qkvb, qkvb, qkvb, sm, *([cache_kt] * pps), *([cache_vt] * pps), *([cache_lft] * pps))


def _rwkv_body(z_ref, shift_ref, s0_ref, mu_ref, pv_ref, wup_ref, aup_ref, gup_ref, seg_ref,
               o_ref, so_ref, sbd_ref, prev_ref, *, chunk, n_chunks, t_valid):
    c = pl.program_id(1)
    tb = chunk * n_chunks
    rows_x = GROUP * chunk
    lane_mask = _head_mask(rows_x, chunk, RW_GW, HEAD_DIM)
    state_mask = _head_mask(RW_GW, HEAD_DIM, RW_GW, HEAD_DIM)

    @pl.when(c == 0)
    def _():
        prev_ref[0:1, :] = shift_ref[...]
        for g in range(RW_GROUPS):
            blocks = [jnp.concatenate([s0_ref[g * GROUP + h]] * GROUP, axis=1) for h in range(GROUP)]
            sbd_ref[g] = jnp.where(state_mask, jnp.concatenate(blocks, axis=0), 0.0)

    z = z_ref[...]
    row = lax.broadcasted_iota(jnp.int32, (tb, 1), 0)
    z_prev = jnp.where(row == 0, prev_ref[0:1, :], pltpu.roll(z, 1, axis=0))
    prev_ref[0:1, :] = z[tb - 1:tb, :]
    zs = z + (z_prev - z) * mu_ref[...]
    r = zs[:, 0:RW_W]
    k = zs[:, RW_W:2 * RW_W]
    v = zs[:, 2 * RW_W:3 * RW_W]
    o_l = 3 * RW_W
    wd = zs[:, o_l:o_l + RW_LORA_W]
    ad = zs[:, o_l + RW_LORA_W:o_l + RW_LORA_W + RW_LORA_A]
    gd = zs[:, o_l + RW_LORA_W + RW_LORA_A:]
    pv = pv_ref[...]
    w_log = -_softplus(-(pv[0:1] + _dot(jnp.tanh(wd).astype(BF16), wup_ref[...]))) - 0.5
    lw = -jnp.exp(w_log)
    a = _sigmoid(pv[1:2] + _dot(ad.astype(BF16), aup_ref[...]))
    g_out = _dot(_sigmoid(gd).astype(BF16), gup_ref[...])
    kk = k * pv[2:3]
    k = k * (1.0 + (a - 1.0) * pv[3:4])
    seg2 = seg_ref[...]
    kk = kk * lax.rsqrt(_seg_sum(kk * kk, seg2) + L2_EPS)
    if t_valid < chunk:
        valid = row < t_valid
        lw = jnp.where(valid, lw, 0.0)
        kk = jnp.where(valid, kk, 0.0)
        k = jnp.where(valid, k, 0.0)
    cs = _dot_sel_l(_block_tri(tb, chunk, strict=False).astype(BF16), lw)
    p_inv = jnp.exp(-cs)
    rp = (r * jnp.exp(cs)).astype(BF16)
    kkp = (kk * jnp.exp(cs - lw)).astype(BF16)
    bn = kk * a * p_inv
    kn = k * p_inv
    vb = v.astype(BF16)
    strict = _block_tri(rows_x, chunk, strict=True)
    incl2 = _block_tri(rows_x, chunk, strict=False, reps=2)
    y_rows = []
    for ci in range(n_chunks):
        rs = slice(ci * chunk, (ci + 1) * chunk)
        p_end = jnp.exp(cs[(ci + 1) * chunk - 1:(ci + 1) * chunk, :])
        bn_c = bn[rs]
        kn_c = kn[rs]
        bn_e = (bn_c * p_end).astype(BF16)
        kn_e = (kn_c * p_end).astype(BF16)
        bn_c = bn_c.astype(BF16)
        kn_c = kn_c.astype(BF16)
        y_groups = []
        for g in range(RW_GROUPS):
            gs = slice(g * RW_GW, (g + 1) * RW_GW)
            kkp_x = _expand(kkp[rs, gs], lane_mask)
            rp_x = _expand(rp[rs, gs], lane_mask)
            v_x = _expand(vb[rs, gs], lane_mask)
            lhs = jnp.concatenate([kkp_x, rp_x], axis=0)
            rhs = jnp.concatenate([bn_c[:, gs]] * GROUP + [kn_c[:, gs]] * GROUP, axis=0)
            a_all = _dot_nt(lhs, rhs)
            a_ab = jnp.where(strict, a_all[:rows_x, :rows_x], 0.0)
            a_ak = jnp.where(strict, a_all[:rows_x, rows_x:], 0.0).astype(BF16)
            a_r = jnp.where(incl2, a_all[rows_x:, :], 0.0).astype(BF16)
            t_m1 = _unit_lower_inverse_minus_eye(a_ab, chunk).astype(BF16)
            st = sbd_ref[g]
            stb = st.astype(BF16)
            rhs_sa = _dot_nt(kkp_x, stb) + _dot(a_ak, v_x)
            sa = -(rhs_sa + _dot(t_m1, rhs_sa.astype(BF16)))
            sa_v = jnp.concatenate([sa.astype(BF16), v_x], axis=0)
            y_x = _dot_nt(rp_x, stb) + _dot(a_r, sa_v)
            y_groups.append(_collapse(y_x, chunk))
            bk_e = jnp.concatenate([_expand(bn_e[:, gs], lane_mask), _expand(kn_e[:, gs], lane_mask)], axis=0)
            sbd_ref[g] = st * p_end[:, gs] + _dot_tn(sa_v, bk_e)
        y_rows.append(jnp.concatenate(y_groups, axis=1))
    y = y_rows[0] if n_chunks == 1 else jnp.concatenate(y_rows, axis=0)
    inv_n = 1.0 / HEAD_DIM
    mean = _seg_sum(y, seg2) * inv_n
    yc = y - mean
    var = _seg_sum(yc * yc, seg2) * inv_n
    yn = yc * lax.rsqrt(var + RW_GN_EPS) * pv[5:6] + pv[6:7]
    bonus = _seg_sum(r * k * pv[4:5], seg2) * v
    o_ref[...] = ((yn + bonus) * g_out).astype(BF16)

    @pl.when(c == pl.num_programs(1) - 1)
    def _():
        for g in range(RW_GROUPS):
            st = sbd_ref[g]
            for h in range(GROUP):
                hs = slice(h * HEAD_DIM, (h + 1) * HEAD_DIM)
                so_ref[g * GROUP + h] = st[hs, hs]


def _rwkv(zr, shift0, s0, w, layer, batch, seq, *, chunk, n_chunks, t_valid):
    m = batch * seq
    tb = chunk * n_chunks
    nc = seq // tb
    body = functools.partial(_rwkv_body, chunk=chunk, n_chunks=n_chunks, t_valid=t_valid)
    lspec = lambda a: _const_spec((None,) + a.shape[1:], lambda b, c: (layer, 0, 0))
    state_spec = pl.BlockSpec((None, RW_HEADS, HEAD_DIM, HEAD_DIM), lambda b, c: (b, 0, 0, 0))
    return pl.pallas_call(
        body,
        grid=(batch, nc),
        in_specs=[
            pl.BlockSpec((tb, RW_COLS), lambda b, c: (b * nc + c, 0)),
            pl.BlockSpec((None, 1, RW_COLS), lambda b, c: (b, 0, 0)),
            state_spec,
            lspec(w["mu"]), lspec(w["pv"]), lspec(w["w_up"]), lspec(w["a_up"]), lspec(w["g_up"]),
            _const_spec((2 * RW_W, RW_W), lambda b, c: (0, 0)),
        ],
        out_specs=(pl.BlockSpec((tb, RW_W), lambda b, c: (b * nc + c, 0)), state_spec),
        out_shape=(
            jax.ShapeDtypeStruct((m, RW_W), BF16),
            jax.ShapeDtypeStruct((batch, RW_HEADS, HEAD_DIM, HEAD_DIM), F32),
        ),
        scratch_shapes=[pltpu.VMEM((RW_GROUPS, RW_GW, RW_GW), F32), pltpu.VMEM((SUBLANES, RW_COLS), F32)],
        compiler_params=_cparams("parallel", "arbitrary"),
        name="rwkv",
    )(zr, shift0, s0, w["mu"], w["pv"], w["w_up"], w["a_up"], w["g_up"], w["seg2"])


def _gdn_body(x_ref, gz_ref, sm_ref, conv0_ref, s0_ref, cw_ref, ng_ref, seg_ref,
              o_ref, so_ref, sbd_ref, prev_ref, *, chunk, n_chunks, t_valid):
    c = pl.program_id(1)
    tb = chunk * n_chunks
    rows_x = GD_HEADS * chunk
    lane_mask = _head_mask(rows_x, chunk, GD_W, GD_DK)
    state_mask = _head_mask(GD_W, GD_DK, GD_W, GD_DV)

    @pl.when(c == 0)
    def _():
        prev_ref[...] = conv0_ref[...]
        blocks = [jnp.concatenate([s0_ref[h]] * GD_HEADS, axis=1) for h in range(GD_HEADS)]
        sbd_ref[...] = jnp.where(state_mask, jnp.concatenate(blocks, axis=0), 0.0)

    x = x_ref[...]
    prev = prev_ref[...]
    prev_ref[...] = x[tb - SUBLANES:, :]
    cw = cw_ref[...]
    row8 = lax.broadcasted_iota(jnp.int32, (SUBLANES, 1), 0)
    conv = x * cw[GD_CONV - 1:GD_CONV]
    for sft in range(1, GD_CONV):
        xr = pltpu.roll(x, sft, axis=0)
        top = jnp.where(row8 < sft, pltpu.roll(prev, sft, axis=0), xr[:SUBLANES])
        xs = top if tb == SUBLANES else jnp.concatenate([top, xr[SUBLANES:]], axis=0)
        conv = conv + xs * cw[GD_CONV - 1 - sft:GD_CONV - sft]
    act = conv * _sigmoid(conv)
    seg2 = seg_ref[...]
    q = act[:, :GD_W]
    k = act[:, GD_W:2 * GD_W]
    v = act[:, 2 * GD_W:]
    q = q * lax.rsqrt(_seg_sum(q * q, seg2) + L2_EPS) * (GD_DK ** -0.5)
    k = k * lax.rsqrt(_seg_sum(k * k, seg2) + L2_EPS)
    sm = sm_ref[...]
    beta = sm[:, 8:8 + GD_HEADS]
    g = sm[:, 12:12 + GD_HEADS]
    if t_valid < chunk:
        rowc = lax.broadcasted_iota(jnp.int32, (tb, 1), 0)
        beta = jnp.where(rowc < t_valid, beta, 0.0)
        g = jnp.where(rowc < t_valid, g, 0.0)
    gc = _dot_sel_l(_block_tri(tb, chunk, strict=False).astype(BF16), g)
    head_lanes = _head_mask(GD_HEADS, 1, GD_W, GD_DK).astype(BF16)
    beta_l = _dot_sel_r(beta, head_lanes)
    gc_l = _dot_sel_r(gc, head_lanes)
    egc_l = jnp.exp(gc_l)
    kb = k * beta_l
    vbeta = (v * beta_l).astype(BF16)
    kbg = (kb * egc_l).astype(BF16)
    qg = (q * egc_l).astype(BF16)
    kb = kb.astype(BF16)
    qb = q.astype(BF16)
    kbf = k.astype(BF16)
    strict = _block_tri(rows_x, chunk, strict=True)
    incl = _block_tri(rows_x, chunk, strict=False)
    pick = _head_mask(rows_x, chunk, GD_HEADS, 1)
    ones = jnp.ones((rows_x, GD_HEADS), BF16)
    o_rows = []
    for ci in range(n_chunks):
        rs = slice(ci * chunk, (ci + 1) * chunk)
        last = slice((ci + 1) * chunk - 1, (ci + 1) * chunk)
        glast_l = gc_l[last]
        kg = (k[rs] * jnp.exp(glast_l - gc_l[rs])).astype(BF16)
        px = jnp.where(pick, jnp.concatenate([gc[rs]] * GD_HEADS, axis=0), 0.0)
        p_hi, p_mid, p_lo = _split3(px)
        d_exp = _dot_nt(jnp.concatenate([p_hi, p_mid, p_lo, ones, ones, ones], axis=1),
                        jnp.concatenate([ones, ones, ones, -p_hi, -p_mid, -p_lo], axis=1))
        decay = jnp.exp(jnp.minimum(d_exp, 0.0))
        kb_x = _expand(kb[rs], lane_mask)
        q_x = _expand(qb[rs], lane_mask)
        k_t = jnp.concatenate([kbf[rs]] * GD_HEADS, axis=0)
        qk = _dot_nt(jnp.concatenate([kb_x, q_x], axis=0), k_t)
        mm = jnp.where(strict, qk[:rows_x] * decay, 0.0)
        amat = jnp.where(incl, qk[rows_x:] * decay, 0.0).astype(BF16)
        t_m1 = _unit_lower_inverse_minus_eye(mm, chunk).astype(BF16)
        vk = jnp.concatenate([_expand(vbeta[rs], lane_mask), _expand(kbg[rs], lane_mask)], axis=1)
        uw = vk.astype(F32) + _dot(t_m1, vk)
        st = sbd_ref[...]
        stb = st.astype(BF16)
        v_new = uw[:, :GD_W] - _dot(uw[:, GD_W:].astype(BF16), stb)
        vnb = v_new.astype(BF16)
        o_x = _dot(_expand(qg[rs], lane_mask), stb) + _dot(amat, vnb)
        o_rows.append(_collapse(o_x, chunk))
        sbd_ref[...] = st * jnp.exp(glast_l) + _dot_tn(_expand(kg, lane_mask), vnb)
    o = o_rows[0] if n_chunks == 1 else jnp.concatenate(o_rows, axis=0)
    o = o * lax.rsqrt(_seg_sum(o * o, seg2) * (1.0 / GD_DV) + EPS) * ng_ref[...]
    gz = gz_ref[...]
    o_ref[...] = (o * (gz * _sigmoid(gz))).astype(BF16)

    @pl.when(c == pl.num_programs(1) - 1)
    def _():
        st = sbd_ref[...]
        for h in range(GD_HEADS):
            so_ref[h] = st[h * GD_DK:(h + 1) * GD_DK, h * GD_DV:(h + 1) * GD_DV]


def _gdn(gqkv, gz, sm, conv0, s0, w, layer, batch, seq, *, chunk, n_chunks, t_valid):
    m = batch * seq
    tb = chunk * n_chunks
    nc = seq // tb
    body = functools.partial(_gdn_body, chunk=chunk, n_chunks=n_chunks, t_valid=t_valid)
    lspec = lambda a: _const_spec((None,) + a.shape[1:], lambda b, c: (layer, 0, 0))
    state_spec = pl.BlockSpec((None, GD_HEADS, GD_DK, GD_DV), lambda b, c: (b, 0, 0, 0))
    return pl.pallas_call(
        body,
        grid=(batch, nc),
        in_specs=[
            pl.BlockSpec((tb, GD_QKV), lambda b, c: (b * nc + c, 0)),
            pl.BlockSpec((tb, GD_HEADS * GD_DV), lambda b, c: (b * nc + c, 0)),
            pl.BlockSpec((tb, N_SMALL), lambda b, c: (b * nc + c, 0)),
            pl.BlockSpec((None, SUBLANES, GD_QKV), lambda b, c: (b, 0, 0)),
            state_spec,
            lspec(w["conv"]), lspec(w["norm_g"]),
            _const_spec((2 * GD_W, GD_W), lambda b, c: (0, 0)),
        ],
        out_specs=(pl.BlockSpec((tb, GD_HEADS * GD_DV), lambda b, c: (b * nc + c, 0)), state_spec),
        out_shape=(
            jax.ShapeDtypeStruct((m, GD_HEADS * GD_DV), BF16),
            jax.ShapeDtypeStruct((batch, GD_HEADS, GD_DK, GD_DV), F32),
        ),
        scratch_shapes=[pltpu.VMEM((GD_W, GD_W), F32), pltpu.VMEM((SUBLANES, GD_QKV), F32)],
        compiler_params=_cparams("parallel", "arbitrary"),
        name="gdn",
    )(gqkv, gz, sm, conv0, s0, w["conv"], w["norm_g"], w["seg2"])


def _merge_body(x_ref, of_ref, or_ref, og_ref, zg_ref, wf_ref, wr_ref, wg_ref, wo_ref, o_ref):
    d = x_ref.shape[1]
    merged = _sigmoid(zg_ref[:, 0:d]) * _dot(of_ref[...], wf_ref[...])
    merged = merged + _sigmoid(zg_ref[:, d:2 * d]) * _dot(or_ref[...], wr_ref[...])
    merged = merged + _sigmoid(zg_ref[:, 2 * d:3 * d]) * _dot(og_ref[...], wg_ref[...])
    o_ref[...] = x_ref[...] + _dot(merged.astype(BF16), wo_ref[...])


def _merge(x, o_fox, o_rw, o_gd, zg, w, layer, *, tm):
    m, d = x.shape
    row_blk = lambda n: pl.BlockSpec((tm, n), lambda i: (i, 0))
    wspec = lambda a: _const_spec((None,) + a.shape[1:], lambda i: (layer, 0, 0))
    return pl.pallas_call(
        _merge_body,
        grid=(m // tm,),
        in_specs=[row_blk(d), row_blk(FOX_W), row_blk(RW_W), row_blk(GD_HEADS * GD_DV),
                  row_blk(N_BRANCH * d),
                  wspec(w["br_fox"]), wspec(w["br_rw"]), wspec(w["br_gd"]), wspec(w["out"])],
        out_specs=row_blk(d),
        out_shape=jax.ShapeDtypeStruct((m, d), F32),
        compiler_params=_cparams("parallel"),
        name="merge",
    )(x, o_fox, o_rw, o_gd, zg, w["br_fox"], w["br_rw"], w["br_gd"], w["out"])


def _same_segment2(n, width):
    idx = jnp.arange(n) // width
    seg = (idx[:, None] == idx[None, :]).astype(BF16)
    return jnp.concatenate([seg, seg], axis=0)


def _row_tile(m):
    for tm in (512, 256, 128, 64, 32, 16, 8):
        if m % tm == 0:
            return tm
    raise ValueError(f"row count {m} is not a multiple of 8")


def _pages_per_step(n_pages):
    for pps in (8, 4, 2, 1):
        if n_pages % pps == 0:
            return pps


def kernel(x_prompt, x_sample, cache_k, cache_v, cache_logf, state_rwkv_shift, state_rwkv,
           state_gdn_conv, state_gdn, page_table, norm_ffn1, ffn1_wg, ffn1_wu, ffn1_wd, norm_mix,
           w_in, fox_fb, rw_mu, rw_w0, rw_w_up, rw_a0, rw_a_up, rw_g_up, rw_kk, rw_ka, rw_rk,
           rw_ln_g, rw_ln_b, gd_conv, gd_a_log, gd_dt_bias, gd_norm_g, w_br_fox, w_br_rw, w_br_gd,
           w_out, norm_ffn2, ffn2_wg, ffn2_wu, ffn2_wd, final_norm):
    bp, seq, d = x_prompt.shape
    bd, t_s, _ = x_sample.shape
    depth = w_in.shape[0]
    tp = SUBLANES
    chunk_p = 64
    n_chunks_p = 4 if seq % (4 * chunk_p) == 0 else 1
    assert t_s <= tp and seq % chunk_p == 0

    o_rw_c = 3 * FOX_W + FOX_HEADS
    o_gd_c = o_rw_c + RW_COLS
    o_gz = o_gd_c + GD_QKV
    o_gb = o_gz + GD_HEADS * GD_DV
    o_ga = o_gb + GD_HEADS
    o_gate = o_ga + GD_HEADS
    scale = HEAD_DIM ** -0.5
    w_small = jnp.concatenate([w_in[:, :, 3 * FOX_W:o_rw_c], w_in[:, :, o_gb:o_gate]], axis=-1)
    zeros4 = jnp.zeros((depth, GD_HEADS), F32)
    small_bias = jnp.concatenate([fox_fb, zeros4, gd_dt_bias], axis=-1)
    small_alog = jnp.concatenate([jnp.zeros((depth, 8), F32), zeros4, gd_a_log], axis=-1)
    w_proj = {
        "fox": jnp.concatenate([w_in[:, :, :FOX_W] * scale, w_in[:, :, FOX_W:3 * FOX_W]], axis=-1).astype(BF16),
        "rw": w_in[:, :, o_rw_c:o_gd_c].astype(BF16),
        "gd": w_in[:, :, o_gd_c:o_gb].astype(BF16),
        "gate": w_in[:, :, o_gate:].astype(BF16),
        "small": w_small.astype(BF16),
        "small_t": jnp.swapaxes(w_small, 1, 2).astype(BF16),
        "small_pc": jnp.stack([small_bias, small_alog], axis=1),
        "small_pr": jnp.stack([small_bias, small_alog], axis=2),
    }
    w_rw = {
        "mu": rw_mu[:, None, :],
        "pv": jnp.stack([rw_w0, rw_a0, rw_kk, rw_ka, rw_rk.reshape(depth, RW_W), rw_ln_g, rw_ln_b,
                         jnp.zeros_like(rw_w0)], axis=1),
        "w_up": rw_w_up.astype(BF16), "a_up": rw_a_up.astype(BF16), "g_up": rw_g_up.astype(BF16),
        "seg2": _same_segment2(RW_W, HEAD_DIM),
    }
    w_gd = {
        "conv": jnp.pad(gd_conv, ((0, 0), (0, SUBLANES - GD_CONV), (0, 0))),
        "norm_g": jnp.tile(gd_norm_g, (1, GD_HEADS))[:, None, :],
        "seg2": _same_segment2(GD_W, GD_DK),
    }
    w_mg = {"br_fox": w_br_fox.astype(BF16), "br_rw": w_br_rw.astype(BF16),
            "br_gd": w_br_gd.astype(BF16), "out": w_out.astype(BF16)}
    ffn1 = (norm_ffn1[:, None, :], ffn1_wg.astype(BF16), ffn1_wu.astype(BF16), ffn1_wd.astype(BF16))
    ffn2 = (norm_ffn2[:, None, :], ffn2_wg.astype(BF16), ffn2_wu.astype(BF16), ffn2_wd.astype(BF16))
    norm_mix3 = norm_mix[:, None, :]
    final_g = final_norm[None, :]
    fox_consts = _fox_prompt_consts()

    n_pool, page = cache_k.shape[1], cache_k.shape[2]
    ckt = jnp.transpose(cache_k, (0, 1, 3, 4, 2)).reshape(depth, n_pool, FOX_W, page)
    cvt = jnp.transpose(cache_v, (0, 1, 3, 4, 2)).reshape(depth, n_pool, FOX_W, page)
    clft = jnp.swapaxes(cache_logf, 2, 3)
    pps = _pages_per_step(page_table.shape[1])
    gd_conv0_s = jnp.pad(state_gdn_conv, ((0, 0), (0, 0), (SUBLANES - (GD_CONV - 1), 0), (0, 0)))
    zeros_p = {
        "shift": jnp.zeros((bp, 1, RW_COLS), F32),
        "rw_s": jnp.zeros((bp, RW_HEADS, HEAD_DIM, HEAD_DIM), F32),
        "conv": jnp.zeros((bp, SUBLANES, GD_QKV), F32),
        "gd_s": jnp.zeros((bp, GD_HEADS, GD_DK, GD_DV), F32),
    }

    xp = x_prompt.reshape(bp * seq, d)
    xs = jnp.pad(x_sample, ((0, 0), (0, tp - t_s), (0, 0))).reshape(bd * tp, d)
    tm_p, tm_s = _row_tile(bp * seq), _row_tile(bd * tp)
    tm_in_p = min(tm_p, 256)
    tq = 512 if seq % 512 == 0 else min(256, seq)

    def layer_fn(x, l, *, batch, t_len, tm, tm_in, fox_fn, shift0, rw_s0, conv0, gd_s0, chunk, n_chunks,
                 t_valid, last):
        x = _ffn(x, *ffn1, final_g, l, tm=tm)
        qkvb, fk, fv, zr, gqkv, gz, zg, sm, smt = _inproj(x, norm_mix3, w_proj, l, tm=tm_in)
        o_fox = fox_fn(qkvb, sm, smt, l)
        o_rw, rw_st = _rwkv(zr, shift0, rw_s0, w_rw, l, batch, t_len, chunk=chunk, n_chunks=n_chunks,
                            t_valid=t_valid)
        o_gd, gd_s = _gdn(gqkv, gz, sm, conv0, gd_s0, w_gd, l, batch, t_len, chunk=chunk, n_chunks=n_chunks,
                          t_valid=t_valid)
        x = _merge(x, o_fox, o_rw, o_gd, zg, w_mg, l, tm=tm)
        x = _ffn(x, *ffn2, final_g, l, tm=tm, final_norm=last)
        return x, (fk, fv, sm, zr, rw_st, gqkv, gd_s)

    p_states, s_states = [], []
    for l in range(depth):
        last = l == depth - 1
        xp, st = layer_fn(
            xp, l, batch=bp, t_len=seq, tm=tm_p, tm_in=tm_in_p,
            fox_fn=lambda qkvb, sm, smt, l: _fox_prompt(qkvb, sm, fox_consts, bp, seq, tq=tq),
            shift0=zeros_p["shift"], rw_s0=zeros_p["rw_s"], conv0=zeros_p["conv"], gd_s0=zeros_p["gd_s"],
            chunk=chunk_p, n_chunks=n_chunks_p, t_valid=chunk_p, last=last)
        fk, fv, sm, zr, rw_st, gqkv, gd_s = st
        p_states.append((
            fk.reshape(bp, seq, FOX_HEADS, HEAD_DIM), fv.reshape(bp, seq, FOX_HEADS, HEAD_DIM),
            sm[:, :FOX_HEADS].reshape(bp, seq, FOX_HEADS),
            zr.reshape(bp, seq, RW_COLS)[:, seq - 1],
            rw_st,
            gqkv.reshape(bp, seq, GD_QKV)[:, seq - (GD_CONV - 1):],
            gd_s))
        xs, st = layer_fn(
            xs, l, batch=bd, t_len=tp, tm=tm_s, tm_in=tm_s,
            fox_fn=lambda qkvb, sm, smt, l: _fox_sample(qkvb, sm, ckt, cvt, clft, page_table, l,
                                                        t_valid=t_s, pps=pps),
            shift0=state_rwkv_shift[l][:, None, :], rw_s0=state_rwkv[l], conv0=gd_conv0_s[l],
            gd_s0=state_gdn[l], chunk=tp, n_chunks=1, t_valid=t_s, last=last)
        fk, fv, sm, zr, rw_st, gqkv, gd_s = st
        conv_ext = jnp.concatenate([state_gdn_conv[l], gqkv.reshape(bd, tp, GD_QKV)[:, :t_s]], axis=1)
        s_states.append((
            fk.reshape(bd, tp, FOX_HEADS, HEAD_DIM)[:, :t_s], fv.reshape(bd, tp, FOX_HEADS, HEAD_DIM)[:, :t_s],
            sm[:, :FOX_HEADS].reshape(bd, tp, FOX_HEADS)[:, :t_s],
            zr.reshape(bd, tp, RW_COLS)[:, t_s - 1],
            rw_st,
            conv_ext[:, t_s:],
            gd_s))

    p_out = [jnp.stack(s) for s in zip(*p_states)]
    s_out = [jnp.stack(s) for s in zip(*s_states)]
    y_prompt = xp.reshape(bp, seq, d)
    y_sample = xs.reshape(bd, tp, d)[:, :t_s]
    return (y_prompt, y_sample, *p_out, *s_out)
```

```python
import functools

import jax
import jax.numpy as jnp
from jax import lax
from jax.experimental import pallas as pl
from jax.experimental.pallas import tpu as pltpu

F32 = jnp.float32
BF16 = jnp.bfloat16

HEAD_DIM = 64
FOX_HEADS = 8
FOX_W = FOX_HEADS * HEAD_DIM
RW_HEADS = 8
RW_W = RW_HEADS * HEAD_DIM
RW_LORA_W = 64
RW_LORA_A = 64
RW_LORA_G = 128
RW_COLS = 3 * RW_W + RW_LORA_W + RW_LORA_A + RW_LORA_G
RW_GN_EPS = 64e-5
GD_HEADS = 4
GD_DK = 128
GD_DV = 128
GD_W = GD_HEADS * GD_DK
GD_QKV = 2 * GD_W + GD_HEADS * GD_DV
GD_CONV = 4
N_BRANCH = 3
EPS = 1e-6
L2_EPS = 1e-6
N_SMALL = 16
SUBLANES = 8
GROUP = 4
RW_GROUPS = RW_HEADS // GROUP
RW_GW = GROUP * HEAD_DIM
VMEM_LIMIT = 56 * 1024 * 1024


def _cparams(*sem):
    return pltpu.CompilerParams(dimension_semantics=sem, vmem_limit_bytes=VMEM_LIMIT)


def _const_spec(shape, index_map):
    return pl.BlockSpec(shape, index_map, pipeline_mode=pl.Buffered(1))


def _dot(a, b):
    return jnp.dot(a, b, preferred_element_type=F32)


def _dot_nt(a, b):
    return lax.dot_general(a, b, (((1,), (1,)), ((), ())), preferred_element_type=F32)


def _dot_tn(a, b):
    return lax.dot_general(a, b, (((0,), (0,)), ((), ())), preferred_element_type=F32)


def _split2(x):
    hi = x.astype(BF16)
    lo = (x - hi.astype(F32)).astype(BF16)
    return hi, lo


def _split3(x):
    hi = x.astype(BF16)
    r = x - hi.astype(F32)
    mid = r.astype(BF16)
    lo = (r - mid.astype(F32)).astype(BF16)
    return hi, mid, lo


def _dot_sel_r(x, sel):
    hi, mid, lo = _split3(x)
    return _dot(hi, sel) + _dot(mid, sel) + _dot(lo, sel)


def _dot_sel_l(sel, x):
    return _dot(jnp.concatenate([sel] * 3, axis=1), jnp.concatenate(_split3(x), axis=0))


def _seg_sum(x, seg2):
    return _dot(jnp.concatenate(_split2(x), axis=1), seg2)


def _softplus(z):
    return jnp.maximum(z, 0.0) + jnp.log(1.0 + jnp.exp(-jnp.abs(z)))


def _sigmoid(z):
    return 1.0 / (1.0 + jnp.exp(-z))


def _rms(x, g):
    return x * lax.rsqrt(jnp.mean(x * x, axis=-1, keepdims=True) + EPS) * g


def _tri(n, kind):
    r = lax.broadcasted_iota(jnp.int32, (n, n), 0)
    c = lax.broadcasted_iota(jnp.int32, (n, n), 1)
    return {"incl": r >= c, "strict": r > c, "upper_incl": r <= c}[kind]


def _block_tri(n, blk, strict, reps=1):
    r = lax.broadcasted_iota(jnp.int32, (n, n * reps), 0)
    c = lax.broadcasted_iota(jnp.int32, (n, n * reps), 1) % n
    same = (r // blk) == (c // blk)
    return same & ((r > c) if strict else (r >= c))


def _head_mask(rows, row_blk, cols, col_blk):
    r = lax.broadcasted_iota(jnp.int32, (rows, cols), 0) // row_blk
    c = lax.broadcasted_iota(jnp.int32, (rows, cols), 1) // col_blk
    return r == c


def _expand(x, mask):
    return jnp.where(mask, jnp.concatenate([x] * GROUP, axis=0), jnp.zeros((), x.dtype))


def _collapse(x, t):
    out = x[0:t]
    for h in range(1, GROUP):
        out = out + x[h * t:(h + 1) * t]
    return out


def _unit_lower_inverse_minus_eye(m_wides, order, amask):
    width = m_wides[0].shape[1]
    npows = [-m for m in m_wides]
    ts = list(npows)
    if order <= 2:
        return ts
    nbs = [n.astype(BF16) for n in npows]
    npows = [_dot(nb, _expand(nb, amask)) for nb in nbs]
    span = 4
    while span <= order:
        nbs = [n.astype(BF16) for n in npows]
        tbs = [_expand(t.astype(BF16), amask) for t in ts]
        if span < order:
            prods = [_dot(nb, jnp.concatenate([tb, _expand(nb, amask)], axis=1)) for nb, tb in zip(nbs, tbs)]
            ts = [t + n + p[:, :width] for t, n, p in zip(ts, npows, prods)]
            npows = [p[:, width:] for p in prods]
        else:
            ts = [t + n + _dot(nb, tb) for t, n, nb, tb in zip(ts, npows, nbs, tbs)]
        span *= 2
    return ts


def _ffn_body(x_ref, g_ref, wg_ref, wu_ref, wd_ref, fg_ref, o_ref, *, ff_chunk, final_norm):
    x = x_ref[...]
    h = _rms(x, g_ref[...]).astype(BF16)
    d_ff = wg_ref.shape[1]
    acc = jnp.zeros_like(x)
    for c0 in range(0, d_ff, ff_chunk):
        gate = _dot(h, wg_ref[:, c0:c0 + ff_chunk])
        up = _dot(h, wu_ref[:, c0:c0 + ff_chunk])
        act = (gate * _sigmoid(gate) * up).astype(BF16)
        acc = acc + _dot(act, wd_ref[c0:c0 + ff_chunk, :])
    y = x + 0.5 * acc
    if final_norm:
        y = _rms(y, fg_ref[...])
    o_ref[...] = y


def _ffn(x, norm_g, wg, wu, wd, final_g, layer, *, tm, final_norm=False):
    m, d = x.shape
    d_ff = wg.shape[2]
    ff_chunk = d_ff // 2 if (d_ff // 2) % 128 == 0 else d_ff
    body = functools.partial(_ffn_body, ff_chunk=ff_chunk, final_norm=final_norm)
    return pl.pallas_call(
        body,
        grid=(m // tm,),
        in_specs=[
            pl.BlockSpec((tm, d), lambda i: (i, 0)),
            _const_spec((None, 1, d), lambda i: (layer, 0, 0)),
            _const_spec((None, d, d_ff), lambda i: (layer, 0, 0)),
            _const_spec((None, d, d_ff), lambda i: (layer, 0, 0)),
            _const_spec((None, d_ff, d), lambda i: (layer, 0, 0)),
            _const_spec((1, d), lambda i: (0, 0)),
        ],
        out_specs=pl.BlockSpec((tm, d), lambda i: (i, 0)),
        out_shape=jax.ShapeDtypeStruct((m, d), F32),
        compiler_params=_cparams("parallel"),
        name="ffn",
    )(x, norm_g, wg, wu, wd, final_g)


def _small_fn(z, is_logf, is_beta, neg_exp_a):
    return jnp.where(is_logf, -_softplus(-z), jnp.where(is_beta, _sigmoid(z), neg_exp_a * _softplus(z)))


def _inproj_body(x_ref, g_ref, wfox_ref, wrw_ref, wgd_ref, wgate_ref, wsm_ref, wsmt_ref,
                 pc_ref, pr_ref,
                 qkvb_ref, fk_ref, fv_ref, zr_ref, gqkv_ref, gz_ref, zg_ref, sm_ref, smt_ref):
    h = _rms(x_ref[...], g_ref[...]).astype(BF16)
    fox = _dot(h, wfox_ref[...])
    qkvb_ref[...] = fox.astype(BF16)
    fk_ref[...] = fox[:, FOX_W:2 * FOX_W]
    fv_ref[...] = fox[:, 2 * FOX_W:]
    zr_ref[...] = _dot(h, wrw_ref[...])
    gd = _dot(h, wgd_ref[...])
    gqkv_ref[...] = gd[:, :GD_QKV]
    gz_ref[...] = gd[:, GD_QKV:]
    zg_ref[...] = _dot(h, wgate_ref[...])
    pc = pc_ref[...]
    col = lax.broadcasted_iota(jnp.int32, (1, N_SMALL), 1)
    sm = _dot(h, wsm_ref[...]) + pc[0:1]
    sm_ref[...] = _small_fn(sm, col < 8, col < 12, -jnp.exp(pc[1:2]))
    pr = pr_ref[...]
    row = lax.broadcasted_iota(jnp.int32, (N_SMALL, 1), 0)
    smt = _dot_nt(wsmt_ref[...], h) + pr[:, 0:1]
    smt_ref[...] = _small_fn(smt, row < 8, row < 12, -jnp.exp(pr[:, 1:2]))


def _inproj(x, norm_g, w, layer, *, tm):
    m, d = x.shape
    n_gate = w["gate"].shape[2]
    n_gd = w["gd"].shape[2]
    row_blk = lambda n: pl.BlockSpec((tm, n), lambda i: (i, 0))
    wspec = lambda a: _const_spec((None,) + a.shape[1:], lambda i: (layer, 0, 0))
    out_shape = (
        jax.ShapeDtypeStruct((m, 3 * FOX_W), BF16),
        jax.ShapeDtypeStruct((m, FOX_W), F32),
        jax.ShapeDtypeStruct((m, FOX_W), F32),
        jax.ShapeDtypeStruct((m, RW_COLS), F32),
        jax.ShapeDtypeStruct((m, GD_QKV), F32),
        jax.ShapeDtypeStruct((m, n_gd - GD_QKV), F32),
        jax.ShapeDtypeStruct((m, n_gate), F32),
        jax.ShapeDtypeStruct((m, N_SMALL), F32),
        jax.ShapeDtypeStruct((N_SMALL, m), F32),
    )
    out_specs = (
        row_blk(3 * FOX_W), row_blk(FOX_W), row_blk(FOX_W), row_blk(RW_COLS), row_blk(GD_QKV),
        row_blk(n_gd - GD_QKV), row_blk(n_gate), row_blk(N_SMALL),
        pl.BlockSpec((N_SMALL, tm), lambda i: (0, i)),
    )
    return pl.pallas_call(
        _inproj_body,
        grid=(m // tm,),
        in_specs=[
            pl.BlockSpec((tm, d), lambda i: (i, 0)),
            _const_spec((None, 1, d), lambda i: (layer, 0, 0)),
            wspec(w["fox"]), wspec(w["rw"]), wspec(w["gd"]), wspec(w["gate"]),
            wspec(w["small"]), wspec(w["small_t"]), wspec(w["small_pc"]), wspec(w["small_pr"]),
        ],
        out_specs=out_specs,
        out_shape=out_shape,
        compiler_params=_cparams("parallel"),
        name="inproj",
    )(x, norm_g, w["fox"], w["rw"], w["gd"], w["gate"], w["small"], w["small_t"],
      w["small_pc"], w["small_pr"])


FOX_AW = 2 * HEAD_DIM
N_PIECES = 3


def _eye(n):
    return (lax.broadcasted_iota(jnp.int32, (n, n), 0) == lax.broadcasted_iota(jnp.int32, (n, n), 1)).astype(BF16)


def _fox_prompt_body(q_ref, k_ref, v_ref, lf_ref, pk_ref, pc_ref, one_ref, o_ref,
                     kaug_ref, vt_ref, qaug_ref, m_ref, l_ref, acc_ref, *, tq, cb):
    i = pl.program_id(1)
    seq = k_ref.shape[0]

    @pl.when(i == 0)
    def _():
        lower = _tri(cb, "incl").astype(BF16)
        eye = _eye(FOX_W)
        carry = jnp.zeros((1, N_SMALL), F32)
        for b0 in range(0, seq, cb):
            cc = _dot_sel_l(lower, lf_ref[b0:b0 + cb, :]) + carry
            carry = cc[cb - 1:cb, :]
            pieces = jnp.concatenate(_split3(-cc), axis=1)
            kaug_ref[b0:b0 + cb, :] = (_dot(k_ref[b0:b0 + cb, :], pk_ref[...])
                                       + _dot(pieces, pc_ref[...])).astype(BF16)
            vt_ref[:, b0:b0 + cb] = _dot_nt(eye, v_ref[b0:b0 + cb, :]).astype(BF16)

    qaug_ref[...] = (_dot(q_ref[...], pk_ref[...]) + one_ref[...]).astype(BF16)
    m_ref[...] = jnp.full(m_ref.shape, -jnp.inf, F32)
    l_ref[...] = jnp.zeros(l_ref.shape, F32)
    acc_ref[...] = jnp.zeros(acc_ref.shape, F32)

    def tile(j, diagonal):
        k0 = pl.multiple_of(j * tq, tq)
        if diagonal:
            visible = _tri(tq, "upper_incl")
        scores = [_dot_nt(kaug_ref[pl.ds(k0, tq), h * FOX_AW:(h + 1) * FOX_AW],
                          qaug_ref[:, h * FOX_AW:(h + 1) * FOX_AW]) for h in range(FOX_HEADS)]
        for h in range(FOX_HEADS):
            hs = slice(h * HEAD_DIM, (h + 1) * HEAD_DIM)
            st = scores[h]
            if diagonal:
                st = jnp.where(visible, st, -jnp.inf)
            m_old = m_ref[h]
            m_new = jnp.maximum(m_old, jnp.max(st, axis=0, keepdims=True))
            alpha = jnp.exp(m_old - m_new)
            p = jnp.exp(st - m_new)
            m_ref[h] = m_new
            l_ref[h] = alpha * l_ref[h] + jnp.sum(p, axis=0, keepdims=True)
            acc_ref[hs, :] = alpha * acc_ref[hs, :] + _dot(vt_ref[hs, pl.ds(k0, tq)], p.astype(BF16))

    def body(j, carry):
        tile(j, False)
        return carry

    lax.fori_loop(0, i, body, 0)
    tile(i, True)
    out_t = jnp.concatenate(
        [acc_ref[h * HEAD_DIM:(h + 1) * HEAD_DIM, :] / l_ref[h] for h in range(FOX_HEADS)], axis=0)
    o_ref[...] = _dot_nt(_eye(tq), out_t.astype(BF16)).astype(BF16)


def _fox_prompt(qkvb, sm, consts, batch, seq, *, tq):
    m = batch * seq
    nq = seq // tq
    cb = min(512, seq)
    body = functools.partial(_fox_prompt_body, tq=tq, cb=cb)
    aug_w = FOX_HEADS * FOX_AW
    return pl.pallas_call(
        body,
        grid=(batch, nq),
        in_specs=[
            pl.BlockSpec((tq, FOX_W), lambda b, i: (b * nq + i, 0)),
            pl.BlockSpec((seq, FOX_W), lambda b, i: (b, 1)),
            pl.BlockSpec((seq, FOX_W), lambda b, i: (b, 2)),
            pl.BlockSpec((seq, N_SMALL), lambda b, i: (b, 0)),
            _const_spec((FOX_W, aug_w), lambda b, i: (0, 0)),
            _const_spec((N_PIECES * N_SMALL, aug_w), lambda b, i: (0, 0)),
            _const_spec((1, aug_w), lambda b, i: (0, 0)),
        ],
        out_specs=pl.BlockSpec((tq, FOX_W), lambda b, i: (b * nq + i, 0)),
        out_shape=jax.ShapeDtypeStruct((m, FOX_W), BF16),
        scratch_shapes=[pltpu.VMEM((seq, aug_w), BF16), pltpu.VMEM((FOX_W, seq), BF16),
                        pltpu.VMEM((tq, aug_w), BF16),
                        pltpu.VMEM((FOX_HEADS, 1, tq), F32), pltpu.VMEM((FOX_HEADS, 1, tq), F32),
                        pltpu.VMEM((FOX_W, tq), F32)],
        compiler_params=_cparams("parallel", "arbitrary"),
        name="fox_prompt",
    )(qkvb, qkvb, qkvb, sm, consts["place_k"], consts["place_c"], consts["ones"])


def _fox_prompt_consts():
    aug_w = FOX_HEADS * FOX_AW
    lane = jnp.arange(aug_w)
    src = jnp.arange(FOX_W)
    place_k = (lane[None, :] == (src // HEAD_DIM * FOX_AW + src % HEAD_DIM)[:, None]).astype(BF16)
    col = jnp.arange(N_PIECES * N_SMALL)
    tgt = jnp.where(col % N_SMALL < FOX_HEADS, (col % N_SMALL) * FOX_AW + HEAD_DIM + col // N_SMALL, -1)
    place_c = (lane[None, :] == tgt[:, None]).astype(BF16)
    ones = ((lane % FOX_AW >= HEAD_DIM) & (lane % FOX_AW < HEAD_DIM + N_PIECES)).astype(F32)[None, :]
    return {"place_k": place_k, "place_c": place_c, "ones": ones}


def _fox_sample_body(pt_ref, q_ref, kn_ref, vn_ref, lfn_ref, *rest, t_valid, pps):
    kp_refs = rest[0:pps]
    vp_refs = rest[pps:2 * pps]
    lf_refs = rest[2 * pps:3 * pps]
    o_ref, qbd_ref, m_ref, l_ref, acc_ref, tail_ref = rest[3 * pps:]
    s_idx = pl.program_id(1)
    tp = SUBLANES
    nrow = FOX_HEADS * tp
    page = kp_refs[0].shape[1]
    diag = _head_mask(nrow, tp, FOX_W, HEAD_DIM)

    def expand_rows(x):
        return jnp.broadcast_to(x[:, None, :], (FOX_HEADS, tp, x.shape[1])).reshape(nrow, x.shape[1])

    @pl.when(s_idx == 0)
    def _():
        q = q_ref[...]
        qbd = jnp.where(diag, jnp.concatenate([q] * FOX_HEADS, axis=0), jnp.zeros((), BF16))
        qbd_ref[...] = qbd
        upper = _tri(tp, "upper_incl").astype(BF16)
        lf_hi, lf_mid, lf_lo = _split3(lfn_ref[...])
        cn = (_dot_tn(lf_hi, upper) + _dot_tn(lf_mid, upper) + _dot_tn(lf_lo, upper))[:FOX_HEADS]
        s = _dot_nt(qbd, kn_ref[...]) - expand_rows(cn)
        t_q = lax.broadcasted_iota(jnp.int32, (nrow, tp), 0) % tp
        u_k = lax.broadcasted_iota(jnp.int32, (nrow, tp), 1)
        s = jnp.where((u_k <= t_q) & (u_k < t_valid), s, -jnp.inf)
        m = jnp.max(s, axis=-1, keepdims=True)
        p = jnp.exp(s - m)
        m_ref[...] = m
        l_ref[...] = jnp.sum(p, axis=-1, keepdims=True)
        acc_ref[...] = _dot(p.astype(BF16), vn_ref[...])
        tail_ref[...] = jnp.zeros_like(tail_ref)

    @pl.when(s_idx > 0)
    def _():
        lane = lax.broadcasted_iota(jnp.int32, (FOX_HEADS, page), 1)
        qbd = qbd_ref[...]
        tail = tail_ref[...]
        scores = [None] * pps
        for r in reversed(range(pps)):
            lf = lf_refs[r][...]
            incl = lf
            sft = 1
            while sft < page:
                nxt = pltpu.roll(incl, page - sft, axis=1)
                incl = incl + jnp.where(lane < page - sft, nxt, 0.0)
                sft *= 2
            suf = incl - lf + tail
            tail = tail + jnp.sum(lf, axis=-1, keepdims=True)
            scores[r] = _dot(qbd, kp_refs[r][...].astype(BF16)) + expand_rows(suf)
        tail_ref[...] = tail
        m_old = m_ref[...]
        m_new = m_old
        for r in range(pps):
            m_new = jnp.maximum(m_new, jnp.max(scores[r], axis=-1, keepdims=True))
        alpha = jnp.exp(m_old - m_new)
        l_new = alpha * l_ref[...]
        acc = alpha * acc_ref[...]
        for r in range(pps):
            p = jnp.exp(scores[r] - m_new)
            l_new = l_new + jnp.sum(p, axis=-1, keepdims=True)
            acc = acc + _dot_nt(p.astype(BF16), vp_refs[r][...].astype(BF16))
        m_ref[...] = m_new
        l_ref[...] = l_new
        acc_ref[...] = acc

    @pl.when(s_idx == pl.num_programs(1) - 1)
    def _():
        o = jnp.where(diag, acc_ref[...] / l_ref[...], 0.0)
        out = o[0:tp]
        for h in range(1, FOX_HEADS):
            out = out + o[h * tp:(h + 1) * tp]
        o_ref[...] = out.astype(BF16)


def _fox_sample(qkvb, sm, cache_kt, cache_vt, cache_lft, page_table, layer, *, t_valid, pps):
    tp = SUBLANES
    bd, n_pages = page_table.shape
    page = cache_kt.shape[3]
    nrow = FOX_HEADS * tp
    assert n_pages % pps == 0

    def page_spec(r, blk):
        def index_map(b, s, pt):
            return (layer, pt[b, n_pages - jnp.maximum(s, 1) * pps + r]) + (0,) * (len(blk) - 2)
        return pl.BlockSpec(blk, index_map)

    kv_blk = (None, None, FOX_W, page)
    lf_blk = (None, None, FOX_HEADS, page)
    grid_spec = pltpu.PrefetchScalarGridSpec(
        num_scalar_prefetch=1,
        grid=(bd, n_pages // pps + 1),
        in_specs=[
            pl.BlockSpec((tp, FOX_W), lambda b, s, pt: (b, 0)),
            pl.BlockSpec((tp, FOX_W), lambda b, s, pt: (b, 1)),
            pl.BlockSpec((tp, FOX_W), lambda b, s, pt: (b, 2)),
            pl.BlockSpec((tp, N_SMALL), lambda b, s, pt: (b, 0)),
            *[page_spec(r, kv_blk) for r in range(pps)],
            *[page_spec(r, kv_blk) for r in range(pps)],
            *[page_spec(r, lf_blk) for r in range(pps)],
        ],
        out_specs=pl.BlockSpec((tp, FOX_W), lambda b, s, pt: (b, 0)),
        scratch_shapes=[
            pltpu.VMEM((nrow, FOX_W), BF16),
            pltpu.VMEM((nrow, 1), F32),
            pltpu.VMEM((nrow, 1), F32),
            pltpu.VMEM((nrow, FOX_W), F32),
            pltpu.VMEM((FOX_HEADS, 1), F32),
        ],
    )
    body = functools.partial(_fox_sample_body, t_valid=t_valid, pps=pps)
    return pl.pallas_call(
        body,
        grid_spec=grid_spec,
        out_shape=jax.ShapeDtypeStruct((bd * tp, FOX_W), BF16),
        compiler_params=_cparams("parallel", "arbitrary"),
        name="fox_sample",
    )(page_table, qkvb, qkvb, qkvb, sm, *([cache_kt] * pps), *([cache_vt] * pps), *([cache_lft] * pps))


def _rwkv_body(z_ref, shift_ref, s0_ref, mu_ref, pv_ref, wup_ref, aup_ref, gup_ref, seg_ref,
               o_ref, so_ref, sbd_ref, prev_ref, *, chunk, n_chunks, t_valid):
    c = pl.program_id(1)
    tb = chunk * n_chunks
    rows_x = GROUP * chunk
    lane_mask = _head_mask(rows_x, chunk, RW_GW, HEAD_DIM)
    amask = _head_mask(rows_x, chunk, rows_x, chunk)
    state_mask = _head_mask(RW_GW, HEAD_DIM, RW_GW, HEAD_DIM)

    @pl.when(c == 0)
    def _():
        prev_ref[0:1, :] = shift_ref[...]
        for g in range(RW_GROUPS):
            blocks = [jnp.concatenate([s0_ref[g * GROUP + h]] * GROUP, axis=1) for h in range(GROUP)]
            sbd_ref[g] = jnp.where(state_mask, jnp.concatenate(blocks, axis=0), 0.0)

    z = z_ref[...]
    row = lax.broadcasted_iota(jnp.int32, (tb, 1), 0)
    z_prev = jnp.where(row == 0, prev_ref[0:1, :], pltpu.roll(z, 1, axis=0))
    prev_ref[0:1, :] = z[tb - 1:tb, :]
    zs = z + (z_prev - z) * mu_ref[...]
    r = zs[:, 0:RW_W]
    k = zs[:, RW_W:2 * RW_W]
    v = zs[:, 2 * RW_W:3 * RW_W]
    o_l = 3 * RW_W
    wd = zs[:, o_l:o_l + RW_LORA_W]
    ad = zs[:, o_l + RW_LORA_W:o_l + RW_LORA_W + RW_LORA_A]
    gd = zs[:, o_l + RW_LORA_W + RW_LORA_A:]
    pv = pv_ref[...]
    w_log = -_softplus(-(pv[0:1] + _dot(jnp.tanh(wd).astype(BF16), wup_ref[...]))) - 0.5
    lw = -jnp.exp(w_log)
    a = _sigmoid(pv[1:2] + _dot(ad.astype(BF16), aup_ref[...]))
    g_out = _dot(_sigmoid(gd).astype(BF16), gup_ref[...])
    kk = k * pv[2:3]
    k = k * (1.0 + (a - 1.0) * pv[3:4])
    seg2 = seg_ref[...]
    kk = kk * lax.rsqrt(_seg_sum(kk * kk, seg2) + L2_EPS)
    if t_valid < chunk:
        valid = row < t_valid
        lw = jnp.where(valid, lw, 0.0)
        kk = jnp.where(valid, kk, 0.0)
        k = jnp.where(valid, k, 0.0)
    cs = _dot_sel_l(_block_tri(tb, chunk, strict=False).astype(BF16), lw)
    p_inv = jnp.exp(-cs)
    rp = (r * jnp.exp(cs)).astype(BF16)
    kkp = (kk * jnp.exp(cs - lw)).astype(BF16)
    bn = kk * a * p_inv
    kn = k * p_inv
    vb = v.astype(BF16)
    t_idx = lax.broadcasted_iota(jnp.int32, (chunk, 2 * rows_x), 0)
    j_idx = lax.broadcasted_iota(jnp.int32, (chunk, 2 * rows_x), 1) % chunk
    strict = (t_idx > j_idx)[:, :rows_x]
    incl2 = t_idx >= j_idx
    groups = range(RW_GROUPS)
    gsl = [slice(g * RW_GW, (g + 1) * RW_GW) for g in groups]
    bnb = bn.astype(BF16)
    knb = kn.astype(BF16)
    pre = {}
    for ci in range(n_chunks):
        rs = slice(ci * chunk, (ci + 1) * chunk)
        for g in groups:
            bk_x = jnp.concatenate([_expand(bnb[rs, gsl[g]], lane_mask), _expand(knb[rs, gsl[g]], lane_mask)],
                                   axis=0)
            a_all = _dot_nt(jnp.concatenate([kkp[rs, gsl[g]], rp[rs, gsl[g]]], axis=0), bk_x)
            pre[ci, g] = (jnp.where(strict, a_all[:chunk, :rows_x], 0.0),
                          jnp.where(strict, a_all[:chunk, rows_x:], 0.0).astype(BF16),
                          jnp.where(incl2, a_all[chunk:, :], 0.0).astype(BF16))
    keys = list(pre)
    t_m1s = dict(zip(keys, _unit_lower_inverse_minus_eye([pre[key][0] for key in keys], chunk, amask)))
    y_rows = []
    for ci in range(n_chunks):
        rs = slice(ci * chunk, (ci + 1) * chunk)
        p_end = jnp.exp(cs[(ci + 1) * chunk - 1:(ci + 1) * chunk, :])
        bk_e = jnp.concatenate([(bn[rs] * p_end).astype(BF16), (kn[rs] * p_end).astype(BF16)], axis=0)
        v_xs = [_expand(vb[rs, gsl[g]], lane_mask) for g in groups]
        sts = [sbd_ref[g] for g in groups]
        stbs = [st.astype(BF16) for st in sts]
        rhs = [_dot_nt(kkp[rs, gsl[g]], stbs[g]) + _dot(pre[ci, g][1], v_xs[g]) for g in groups]
        sabs = [(-(rhs[g] + _dot(t_m1s[ci, g].astype(BF16), _expand(rhs[g].astype(BF16), lane_mask)))
                 ).astype(BF16) for g in groups]
        y_rows.append(jnp.concatenate(
            [_dot_nt(rp[rs, gsl[g]], stbs[g])
             + _dot(pre[ci, g][2], jnp.concatenate([_expand(sabs[g], lane_mask), v_xs[g]], axis=0))
             for g in groups], axis=1))
        for g in groups:
            outer = _dot_tn(jnp.concatenate([sabs[g], vb[rs, gsl[g]]], axis=0), bk_e[:, gsl[g]])
            sbd_ref[g] = sts[g] * p_end[:, gsl[g]] + jnp.where(state_mask, outer, 0.0)
    y = y_rows[0] if n_chunks == 1 else jnp.concatenate(y_rows, axis=0)
    inv_n = 1.0 / HEAD_DIM
    mean = _seg_sum(y, seg2) * inv_n
    yc = y - mean
    var = _seg_sum(yc * yc, seg2) * inv_n
    yn = yc * lax.rsqrt(var + RW_GN_EPS) * pv[5:6] + pv[6:7]
    bonus = _seg_sum(r * k * pv[4:5], seg2) * v
    o_ref[...] = ((yn + bonus) * g_out).astype(BF16)

    @pl.when(c == pl.num_programs(1) - 1)
    def _():
        for g in range(RW_GROUPS):
            st = sbd_ref[g]
            for h in range(GROUP):
                hs = slice(h * HEAD_DIM, (h + 1) * HEAD_DIM)
                so_ref[g * GROUP + h] = st[hs, hs]


def _rwkv(zr, shift0, s0, w, layer, batch, seq, *, chunk, n_chunks, t_valid):
    m = batch * seq
    tb = chunk * n_chunks
    nc = seq // tb
    body = functools.partial(_rwkv_body, chunk=chunk, n_chunks=n_chunks, t_valid=t_valid)
    lspec = lambda a: _const_spec((None,) + a.shape[1:], lambda b, c: (layer, 0, 0))
    state_spec = pl.BlockSpec((None, RW_HEADS, HEAD_DIM, HEAD_DIM), lambda b, c: (b, 0, 0, 0))
    return pl.pallas_call(
        body,
        grid=(batch, nc),
        in_specs=[
            pl.BlockSpec((tb, RW_COLS), lambda b, c: (b * nc + c, 0)),
            pl.BlockSpec((None, 1, RW_COLS), lambda b, c: (b, 0, 0)),
            state_spec,
            lspec(w["mu"]), lspec(w["pv"]), lspec(w["w_up"]), lspec(w["a_up"]), lspec(w["g_up"]),
            _const_spec((2 * RW_W, RW_W), lambda b, c: (0, 0)),
        ],
        out_specs=(pl.BlockSpec((tb, RW_W), lambda b, c: (b * nc + c, 0)), state_spec),
        out_shape=(
            jax.ShapeDtypeStruct((m, RW_W), BF16),
            jax.ShapeDtypeStruct((batch, RW_HEADS, HEAD_DIM, HEAD_DIM), F32),
        ),
        scratch_shapes=[pltpu.VMEM((RW_GROUPS, RW_GW, RW_GW), F32), pltpu.VMEM((SUBLANES, RW_COLS), F32)],
        compiler_params=_cparams("parallel", "arbitrary"),
        name="rwkv",
    )(zr, shift0, s0, w["mu"], w["pv"], w["w_up"], w["a_up"], w["g_up"], w["seg2"])


def _gdn_body(x_ref, gz_ref, sm_ref, conv0_ref, s0_ref, cw_ref, ng_ref, seg_ref,
              o_ref, so_ref, prev_ref, *, chunk, n_chunks, t_valid):
    c = pl.program_id(1)
    tb = chunk * n_chunks
    rows_x = GD_HEADS * chunk
    lane_mask = _head_mask(rows_x, chunk, GD_W, GD_DK)
    amask = _head_mask(rows_x, chunk, rows_x, chunk)

    @pl.when(c == 0)
    def _():
        prev_ref[...] = conv0_ref[...]
        so_ref[...] = s0_ref[...]

    x = x_ref[...]
    prev = prev_ref[...]
    prev_ref[...] = x[tb - SUBLANES:, :]
    cw = cw_ref[...]
    row8 = lax.broadcasted_iota(jnp.int32, (SUBLANES, 1), 0)
    conv = x * cw[GD_CONV - 1:GD_CONV]
    for sft in range(1, GD_CONV):
        xr = pltpu.roll(x, sft, axis=0)
        top = jnp.where(row8 < sft, pltpu.roll(prev, sft, axis=0), xr[:SUBLANES])
        xs = top if tb == SUBLANES else jnp.concatenate([top, xr[SUBLANES:]], axis=0)
        conv = conv + xs * cw[GD_CONV - 1 - sft:GD_CONV - sft]
    act = conv * _sigmoid(conv)
    seg2 = seg_ref[...]
    q = act[:, :GD_W]
    k = act[:, GD_W:2 * GD_W]
    v = act[:, 2 * GD_W:]
    q = q * lax.rsqrt(_seg_sum(q * q, seg2) + L2_EPS) * (GD_DK ** -0.5)
    k = k * lax.rsqrt(_seg_sum(k * k, seg2) + L2_EPS)
    sm = sm_ref[...]
    beta = sm[:, 8:8 + GD_HEADS]
    g = sm[:, 12:12 + GD_HEADS]
    if t_valid < chunk:
        rowc = lax.broadcasted_iota(jnp.int32, (tb, 1), 0)
        beta = jnp.where(rowc < t_valid, beta, 0.0)
        g = jnp.where(rowc < t_valid, g, 0.0)
    gc = _dot_sel_l(_block_tri(tb, chunk, strict=False).astype(BF16), g)
    head_lanes = _head_mask(GD_HEADS, 1, GD_W, GD_DK).astype(BF16)
    beta_l = _dot_sel_r(beta, head_lanes)
    gc_l = _dot_sel_r(gc, head_lanes)
    egc_l = jnp.exp(gc_l)
    kb = k * beta_l
    vbeta = (v * beta_l).astype(BF16)
    kbg = (kb * egc_l).astype(BF16)
    qg = (q * egc_l).astype(BF16)
    kb = kb.astype(BF16)
    qb = q.astype(BF16)
    kbf = k.astype(BF16)
    t_idx = lax.broadcasted_iota(jnp.int32, (chunk, rows_x), 0)
    s_idx = lax.broadcasted_iota(jnp.int32, (chunk, rows_x), 1) % chunk
    strict = t_idx > s_idx
    incl = t_idx >= s_idx
    diag = t_idx == s_idx
    head_cols = _head_mask(GD_HEADS, 1, rows_x, chunk).astype(BF16)
    ones_cc = jnp.ones((chunk, chunk), BF16)
    mms, amats = [], []
    for ci in range(n_chunks):
        rs = slice(ci * chunk, (ci + 1) * chunk)
        g_t = _dot_sel_r(gc[rs], head_cols)
        g_s = _dot_sel_l(ones_cc, jnp.where(diag, g_t, 0.0))
        decay = jnp.exp(jnp.minimum(g_t - g_s, 0.0))
        qk = _dot_nt(jnp.concatenate([kb[rs], qb[rs]], axis=0), _expand(kbf[rs], lane_mask))
        mms.append(jnp.where(strict, qk[:chunk] * decay, 0.0))
        amats.append(jnp.where(incl, qk[chunk:] * decay, 0.0).astype(BF16))
    t_m1s = _unit_lower_inverse_minus_eye(mms, chunk, amask)
    uws = []
    for ci in range(n_chunks):
        rs = slice(ci * chunk, (ci + 1) * chunk)
        vk = jnp.concatenate([vbeta[rs], kbg[rs]], axis=1)
        vk_x = jnp.concatenate([_expand(vbeta[rs], lane_mask), _expand(kbg[rs], lane_mask)], axis=1)
        uws.append(vk.astype(F32) + _dot(t_m1s[ci].astype(BF16), vk_x))
    o_rows = []
    for ci in range(n_chunks):
        rs = slice(ci * chunk, (ci + 1) * chunk)
        last = slice((ci + 1) * chunk - 1, (ci + 1) * chunk)
        glast_l = gc_l[last]
        kg = (k[rs] * jnp.exp(glast_l - gc_l[rs])).astype(BF16)
        uw = uws[ci]
        amat = amats[ci]
        wb = uw[:, GD_W:].astype(BF16)
        v_news, o_state = [], []
        for h in range(GD_HEADS):
            hs = slice(h * GD_DK, (h + 1) * GD_DK)
            stb = so_ref[h].astype(BF16)
            v_news.append(uw[:, hs] - _dot(wb[:, hs], stb))
            o_state.append(_dot(qg[rs, hs], stb))
        vnb = jnp.concatenate(v_news, axis=1).astype(BF16)
        o_rows.append(jnp.concatenate(o_state, axis=1) + _dot(amat, _expand(vnb, lane_mask)))
        for h in range(GD_HEADS):
            hs = slice(h * GD_DK, (h + 1) * GD_DK)
            so_ref[h] = so_ref[h] * jnp.exp(glast_l[:, hs]) + _dot_tn(kg[:, hs], vnb[:, hs])
    o = o_rows[0] if n_chunks == 1 else jnp.concatenate(o_rows, axis=0)
    o = o * lax.rsqrt(_seg_sum(o * o, seg2) * (1.0 / GD_DV) + EPS) * ng_ref[...]
    gz = gz_ref[...]
    o_ref[...] = (o * (gz * _sigmoid(gz))).astype(BF16)


def _gdn(gqkv, gz, sm, conv0, s0, w, layer, batch, seq, *, chunk, n_chunks, t_valid):
    m = batch * seq
    tb = chunk * n_chunks
    nc = seq // tb
    body = functools.partial(_gdn_body, chunk=chunk, n_chunks=n_chunks, t_valid=t_valid)
    lspec = lambda a: _const_spec((None,) + a.shape[1:], lambda b, c: (layer, 0, 0))
    state_spec = pl.BlockSpec((None, GD_HEADS, GD_DK, GD_DV), lambda b, c: (b, 0, 0, 0))
    return pl.pallas_call(
        body,
        grid=(batch, nc),
        in_specs=[
            pl.BlockSpec((tb, GD_QKV), lambda b, c: (b * nc + c, 0)),
            pl.BlockSpec((tb, GD_HEADS * GD_DV), lambda b, c: (b * nc + c, 0)),
            pl.BlockSpec((tb, N_SMALL), lambda b, c: (b * nc + c, 0)),
            pl.BlockSpec((None, SUBLANES, GD_QKV), lambda b, c: (b, 0, 0)),
            state_spec,
            lspec(w["conv"]), lspec(w["norm_g"]),
            _const_spec((2 * GD_W, GD_W), lambda b, c: (0, 0)),
        ],
        out_specs=(pl.BlockSpec((tb, GD_HEADS * GD_DV), lambda b, c: (b * nc + c, 0)), state_spec),
        out_shape=(
            jax.ShapeDtypeStruct((m, GD_HEADS * GD_DV), BF16),
            jax.ShapeDtypeStruct((batch, GD_HEADS, GD_DK, GD_DV), F32),
        ),
        scratch_shapes=[pltpu.VMEM((SUBLANES, GD_QKV), F32)],
        compiler_params=_cparams("parallel", "arbitrary"),
        name="gdn",
    )(gqkv, gz, sm, conv0, s0, w["conv"], w["norm_g"], w["seg2"])


def _merge_body(x_ref, of_ref, or_ref, og_ref, zg_ref, wf_ref, wr_ref, wg_ref, wo_ref, o_ref):
    d = x_ref.shape[1]
    merged = _sigmoid(zg_ref[:, 0:d]) * _dot(of_ref[...], wf_ref[...])
    merged = merged + _sigmoid(zg_ref[:, d:2 * d]) * _dot(or_ref[...], wr_ref[...])
    merged = merged + _sigmoid(zg_ref[:, 2 * d:3 * d]) * _dot(og_ref[...], wg_ref[...])
    o_ref[...] = x_ref[...] + _dot(merged.astype(BF16), wo_ref[...])


def _merge(x, o_fox, o_rw, o_gd, zg, w, layer, *, tm):
    m, d = x.shape
    row_blk = lambda n: pl.BlockSpec((tm, n), lambda i: (i, 0))
    wspec = lambda a: _const_spec((None,) + a.shape[1:], lambda i: (layer, 0, 0))
    return pl.pallas_call(
        _merge_body,
        grid=(m // tm,),
        in_specs=[row_blk(d), row_blk(FOX_W), row_blk(RW_W), row_blk(GD_HEADS * GD_DV),
                  row_blk(N_BRANCH * d),
                  wspec(w["br_fox"]), wspec(w["br_rw"]), wspec(w["br_gd"]), wspec(w["out"])],
        out_specs=row_blk(d),
        out_shape=jax.ShapeDtypeStruct((m, d), F32),
        compiler_params=_cparams("parallel"),
        name="merge",
    )(x, o_fox, o_rw, o_gd, zg, w["br_fox"], w["br_rw"], w["br_gd"], w["out"])


def _same_segment2(n, width):
    idx = jnp.arange(n) // width
    seg = (idx[:, None] == idx[None, :]).astype(BF16)
    return jnp.concatenate([seg, seg], axis=0)


def _row_tile(m):
    for tm in (512, 256, 128, 64, 32, 16, 8):
        if m % tm == 0:
            return tm
    raise ValueError(f"row count {m} is not a multiple of 8")


def _pages_per_step(n_pages):
    for pps in (8, 4, 2, 1):
        if n_pages % pps == 0:
            return pps


def kernel(x_prompt, x_sample, cache_k, cache_v, cache_logf, state_rwkv_shift, state_rwkv,
           state_gdn_conv, state_gdn, page_table, norm_ffn1, ffn1_wg, ffn1_wu, ffn1_wd, norm_mix,
           w_in, fox_fb, rw_mu, rw_w0, rw_w_up, rw_a0, rw_a_up, rw_g_up, rw_kk, rw_ka, rw_rk,
           rw_ln_g, rw_ln_b, gd_conv, gd_a_log, gd_dt_bias, gd_norm_g, w_br_fox, w_br_rw, w_br_gd,
           w_out, norm_ffn2, ffn2_wg, ffn2_wu, ffn2_wd, final_norm):
    bp, seq, d = x_prompt.shape
    bd, t_s, _ = x_sample.shape
    depth = w_in.shape[0]
    tp = SUBLANES
    chunk_p = 64
    n_chunks_p = 4 if seq % (4 * chunk_p) == 0 else 1
    assert t_s <= tp and seq % chunk_p == 0

    o_rw_c = 3 * FOX_W + FOX_HEADS
    o_gd_c = o_rw_c + RW_COLS
    o_gz = o_gd_c + GD_QKV
    o_gb = o_gz + GD_HEADS * GD_DV
    o_ga = o_gb + GD_HEADS
    o_gate = o_ga + GD_HEADS
    scale = HEAD_DIM ** -0.5
    w_small = jnp.concatenate([w_in[:, :, 3 * FOX_W:o_rw_c], w_in[:, :, o_gb:o_gate]], axis=-1)
    zeros4 = jnp.zeros((depth, GD_HEADS), F32)
    small_bias = jnp.concatenate([fox_fb, zeros4, gd_dt_bias], axis=-1)
    small_alog = jnp.concatenate([jnp.zeros((depth, 8), F32), zeros4, gd_a_log], axis=-1)
    w_proj = {
        "fox": jnp.concatenate([w_in[:, :, :FOX_W] * scale, w_in[:, :, FOX_W:3 * FOX_W]], axis=-1).astype(BF16),
        "rw": w_in[:, :, o_rw_c:o_gd_c].astype(BF16),
        "gd": w_in[:, :, o_gd_c:o_gb].astype(BF16),
        "gate": w_in[:, :, o_gate:].astype(BF16),
        "small": w_small.astype(BF16),
        "small_t": jnp.swapaxes(w_small, 1, 2).astype(BF16),
        "small_pc": jnp.stack([small_bias, small_alog], axis=1),
        "small_pr": jnp.stack([small_bias, small_alog], axis=2),
    }
    w_rw = {
        "mu": rw_mu[:, None, :],
        "pv": jnp.stack([rw_w0, rw_a0, rw_kk, rw_ka, rw_rk.reshape(depth, RW_W), rw_ln_g, rw_ln_b,
                         jnp.zeros_like(rw_w0)], axis=1),
        "w_up": rw_w_up.astype(BF16), "a_up": rw_a_up.astype(BF16), "g_up": rw_g_up.astype(BF16),
        "seg2": _same_segment2(RW_W, HEAD_DIM),
    }
    w_gd = {
        "conv": jnp.pad(gd_conv, ((0, 0), (0, SUBLANES - GD_CONV), (0, 0))),
        "norm_g": jnp.tile(gd_norm_g, (1, GD_HEADS))[:, None, :],
        "seg2": _same_segment2(GD_W, GD_DK),
    }
    w_mg = {"br_fox": w_br_fox.astype(BF16), "br_rw": w_br_rw.astype(BF16),
            "br_gd": w_br_gd.astype(BF16), "out": w_out.astype(BF16)}
    ffn1 = (norm_ffn1[:, None, :], ffn1_wg.astype(BF16), ffn1_wu.astype(BF16), ffn1_wd.astype(BF16))
    ffn2 = (norm_ffn2[:, None, :], ffn2_wg.astype(BF16), ffn2_wu.astype(BF16), ffn2_wd.astype(BF16))
    norm_mix3 = norm_mix[:, None, :]
    final_g = final_norm[None, :]
    fox_consts = _fox_prompt_consts()

    n_pool, page = cache_k.shape[1], cache_k.shape[2]
    ckt = jnp.transpose(cache_k, (0, 1, 3, 4, 2)).reshape(depth, n_pool, FOX_W, page)
    cvt = jnp.transpose(cache_v, (0, 1, 3, 4, 2)).reshape(depth, n_pool, FOX_W, page)
    clft = jnp.swapaxes(cache_logf, 2, 3)
    pps = _pages_per_step(page_table.shape[1])
    gd_conv0_s = jnp.pad(state_gdn_conv, ((0, 0), (0, 0), (SUBLANES - (GD_CONV - 1), 0), (0, 0)))
    zeros_p = {
        "shift": jnp.zeros((bp, 1, RW_COLS), F32),
        "rw_s": jnp.zeros((bp, RW_HEADS, HEAD_DIM, HEAD_DIM), F32),
        "conv": jnp.zeros((bp, SUBLANES, GD_QKV), F32),
        "gd_s": jnp.zeros((bp, GD_HEADS, GD_DK, GD_DV), F32),
    }

    xp = x_prompt.reshape(bp * seq, d)
    xs = jnp.pad(x_sample, ((0, 0), (0, tp - t_s), (0, 0))).reshape(bd * tp, d)
    tm_p, tm_s = _row_tile(bp * seq), _row_tile(bd * tp)
    tm_in_p = min(tm_p, 256)
    tq = 512 if seq % 512 == 0 else min(256, seq)

    def layer_fn(x, l, *, batch, t_len, tm, tm_in, fox_fn, shift0, rw_s0, conv0, gd_s0, chunk, n_chunks,
                 t_valid, last):
        x = _ffn(x, *ffn1, final_g, l, tm=tm)
        qkvb, fk, fv, zr, gqkv, gz, zg, sm, smt = _inproj(x, norm_mix3, w_proj, l, tm=tm_in)
        o_fox = fox_fn(qkvb, sm, smt, l)
        o_rw, rw_st = _rwkv(zr, shift0, rw_s0, w_rw, l, batch, t_len, chunk=chunk, n_chunks=n_chunks,
                            t_valid=t_valid)
        o_gd, gd_s = _gdn(gqkv, gz, sm, conv0, gd_s0, w_gd, l, batch, t_len, chunk=chunk, n_chunks=n_chunks,
                          t_valid=t_valid)
        x = _merge(x, o_fox, o_rw, o_gd, zg, w_mg, l, tm=tm)
        x = _ffn(x, *ffn2, final_g, l, tm=tm, final_norm=last)
        return x, (fk, fv, sm, zr, rw_st, gqkv, gd_s)

    p_states, s_states = [], []
    for l in range(depth):
        last = l == depth - 1
        xp, st = layer_fn(
            xp, l, batch=bp, t_len=seq, tm=tm_p, tm_in=tm_in_p,
            fox_fn=lambda qkvb, sm, smt, l: _fox_prompt(qkvb, sm, fox_consts, bp, seq, tq=tq),
            shift0=zeros_p["shift"], rw_s0=zeros_p["rw_s"], conv0=zeros_p["conv"], gd_s0=zeros_p["gd_s"],
            chunk=chunk_p, n_chunks=n_chunks_p, t_valid=chunk_p, last=last)
        fk, fv, sm, zr, rw_st, gqkv, gd_s = st
        p_states.append((
            fk.reshape(bp, seq, FOX_HEADS, HEAD_DIM), fv.reshape(bp, seq, FOX_HEADS, HEAD_DIM),
            sm[:, :FOX_HEADS].reshape(bp, seq, FOX_HEADS),
            zr.reshape(bp, seq, RW_COLS)[:, seq - 1],
            rw_st,
            gqkv.reshape(bp, seq, GD_QKV)[:, seq - (GD_CONV - 1):],
            gd_s))
        xs, st = layer_fn(
            xs, l, batch=bd, t_len=tp, tm=tm_s, tm_in=tm_s,
            fox_fn=lambda qkvb, sm, smt, l: _fox_sample(qkvb, sm, ckt, cvt, clft, page_table, l,
                                                        t_valid=t_s, pps=pps),
            shift0=state_rwkv_shift[l][:, None, :], rw_s0=state_rwkv[l], conv0=gd_conv0_s[l],
            gd_s0=state_gdn[l], chunk=tp, n_chunks=1, t_valid=t_s, last=last)
        fk, fv, sm, zr, rw_st, gqkv, gd_s = st
        conv_ext = jnp.concatenate([state_gdn_conv[l], gqkv.reshape(bd, tp, GD_QKV)[:, :t_s]], axis=1)
        s_states.append((
            fk.reshape(bd, tp, FOX_HEADS, HEAD_DIM)[:, :t_s], fv.reshape(bd, tp, FOX_HEADS, HEAD_DIM)[:, :t_s],
            sm[:, :FOX_HEADS].reshape(bd, tp, FOX_HEADS)[:, :t_s],
            zr.reshape(bd, tp, RW_COLS)[:, t_s - 1],
            rw_st,
            conv_ext[:, t_s:],
            gd_s))

    p_out = [jnp.stack(s) for s in zip(*p_states)]
    s_out = [jnp.stack(s) for s in zip(*s_states)]
    y_prompt = xp.reshape(bp, seq, d)
    y_sample = xs.reshape(bd, tp, d)[:, :t_s]
    return (y_prompt, y_sample, *p_out, *s_out)
```

```python
import functools

import jax
import jax.numpy as jnp
from jax import lax
from jax.experimental import pallas as pl
from jax.experimental.pallas import tpu as pltpu

F32 = jnp.float32
BF16 = jnp.bfloat16

HEAD_DIM = 64
FOX_HEADS = 8
FOX_W = FOX_HEADS * HEAD_DIM
RW_HEADS = 8
RW_W = RW_HEADS * HEAD_DIM
RW_LORA_W = 64
RW_LORA_A = 64
RW_LORA_G = 128
RW_COLS = 3 * RW_W + RW_LORA_W + RW_LORA_A + RW_LORA_G
RW_GN_EPS = 64e-5
GD_HEADS = 4
GD_DK = 128
GD_DV = 128
GD_W = GD_HEADS * GD_DK
GD_QKV = 2 * GD_W + GD_HEADS * GD_DV
GD_CONV = 4
N_BRANCH = 3
EPS = 1e-6
L2_EPS = 1e-6
N_SMALL = 16
SUBLANES = 8
GROUP = 4
RW_GROUPS = RW_HEADS // GROUP
RW_GW = GROUP * HEAD_DIM
VMEM_LIMIT = 56 * 1024 * 1024


def _cparams(*sem):
    return pltpu.CompilerParams(dimension_semantics=sem, vmem_limit_bytes=VMEM_LIMIT)


def _const_spec(shape, index_map):
    return pl.BlockSpec(shape, index_map, pipeline_mode=pl.Buffered(1))


def _dot(a, b):
    return jnp.dot(a, b, preferred_element_type=F32)


def _dot_nt(a, b):
    return lax.dot_general(a, b, (((1,), (1,)), ((), ())), preferred_element_type=F32)


def _dot_tn(a, b):
    return lax.dot_general(a, b, (((0,), (0,)), ((), ())), preferred_element_type=F32)


def _split2(x):
    hi = x.astype(BF16)
    lo = (x - hi.astype(F32)).astype(BF16)
    return hi, lo


def _split3(x):
    hi = x.astype(BF16)
    r = x - hi.astype(F32)
    mid = r.astype(BF16)
    lo = (r - mid.astype(F32)).astype(BF16)
    return hi, mid, lo


def _dot_sel_r(x, sel):
    hi, mid, lo = _split3(x)
    return _dot(hi, sel) + _dot(mid, sel) + _dot(lo, sel)


def _dot_sel_l(sel, x):
    return _dot(jnp.concatenate([sel] * 3, axis=1), jnp.concatenate(_split3(x), axis=0))


def _seg_sum(x, seg2):
    return _dot(jnp.concatenate(_split2(x), axis=1), seg2)


def _softplus(z):
    return jnp.maximum(z, 0.0) + jnp.log(1.0 + jnp.exp(-jnp.abs(z)))


def _sigmoid(z):
    return 1.0 / (1.0 + jnp.exp(-z))


def _rms(x, g):
    return x * lax.rsqrt(jnp.mean(x * x, axis=-1, keepdims=True) + EPS) * g


def _tri(n, kind):
    r = lax.broadcasted_iota(jnp.int32, (n, n), 0)
    c = lax.broadcasted_iota(jnp.int32, (n, n), 1)
    return {"incl": r >= c, "strict": r > c, "upper_incl": r <= c}[kind]


def _block_tri(n, blk, strict, reps=1):
    r = lax.broadcasted_iota(jnp.int32, (n, n * reps), 0)
    c = lax.broadcasted_iota(jnp.int32, (n, n * reps), 1) % n
    same = (r // blk) == (c // blk)
    return same & ((r > c) if strict else (r >= c))


def _head_mask(rows, row_blk, cols, col_blk):
    r = lax.broadcasted_iota(jnp.int32, (rows, cols), 0) // row_blk
    c = lax.broadcasted_iota(jnp.int32, (rows, cols), 1) // col_blk
    return r == c


def _expand(x, mask):
    return jnp.where(mask, jnp.concatenate([x] * GROUP, axis=0), jnp.zeros((), x.dtype))


def _collapse(x, t):
    out = x[0:t]
    for h in range(1, GROUP):
        out = out + x[h * t:(h + 1) * t]
    return out


def _unit_lower_inverse_minus_eye(m_wide, order, amask):
    width = m_wide.shape[1]
    npow = -m_wide
    t = npow
    if order <= 2:
        return t
    nb = npow.astype(BF16)
    npow = _dot(nb, _expand(nb, amask))
    yield
    span = 4
    while span <= order:
        nb = npow.astype(BF16)
        tb = _expand(t.astype(BF16), amask)
        if span < order:
            prod = _dot(nb, jnp.concatenate([tb, _expand(nb, amask)], axis=1))
            t = t + npow + prod[:, :width]
            npow = prod[:, width:]
        else:
            t = t + npow + _dot(nb, tb)
        yield
        span *= 2
    return t


def _run_tasks(tasks):
    done, started, active = {}, set(), []
    pending = list(tasks)
    while pending or active:
        for task in list(pending):
            name, needs_done, needs_started, fn = task
            if all(n in done for n in needs_done) and all(n in started for n in needs_started):
                pending.remove(task)
                started.add(name)
                active.append((name, fn(done)))
        assert active, "task dependencies can never be met"
        for entry in list(active):
            name, gen = entry
            try:
                next(gen)
            except StopIteration as stop:
                done[name] = stop.value
                active.remove(entry)
    return done


def _ffn_body(x_ref, g_ref, wg_ref, wu_ref, wd_ref, fg_ref, o_ref, *, ff_chunk, final_norm):
    x = x_ref[...]
    h = _rms(x, g_ref[...]).astype(BF16)
    d_ff = wg_ref.shape[1]
    acc = jnp.zeros_like(x)
    for c0 in range(0, d_ff, ff_chunk):
        gate = _dot(h, wg_ref[:, c0:c0 + ff_chunk])
        up = _dot(h, wu_ref[:, c0:c0 + ff_chunk])
        act = (gate * _sigmoid(gate) * up).astype(BF16)
        acc = acc + _dot(act, wd_ref[c0:c0 + ff_chunk, :])
    y = x + 0.5 * acc
    if final_norm:
        y = _rms(y, fg_ref[...])
    o_ref[...] = y


def _ffn(x, norm_g, wg, wu, wd, final_g, layer, *, tm, final_norm=False):
    m, d = x.shape
    d_ff = wg.shape[2]
    ff_chunk = d_ff // 2 if (d_ff // 2) % 128 == 0 else d_ff
    body = functools.partial(_ffn_body, ff_chunk=ff_chunk, final_norm=final_norm)
    return pl.pallas_call(
        body,
        grid=(m // tm,),
        in_specs=[
            pl.BlockSpec((tm, d), lambda i: (i, 0)),
            _const_spec((None, 1, d), lambda i: (layer, 0, 0)),
            _const_spec((None, d, d_ff), lambda i: (layer, 0, 0)),
            _const_spec((None, d, d_ff), lambda i: (layer, 0, 0)),
            _const_spec((None, d_ff, d), lambda i: (layer, 0, 0)),
            _const_spec((1, d), lambda i: (0, 0)),
        ],
        out_specs=pl.BlockSpec((tm, d), lambda i: (i, 0)),
        out_shape=jax.ShapeDtypeStruct((m, d), F32),
        compiler_params=_cparams("parallel"),
        name="ffn",
    )(x, norm_g, wg, wu, wd, final_g)


def _small_fn(z, is_logf, is_beta, neg_exp_a):
    return jnp.where(is_logf, -_softplus(-z), jnp.where(is_beta, _sigmoid(z), neg_exp_a * _softplus(z)))


def _inproj_body(x_ref, g_ref, wfox_ref, wrw_ref, wgd_ref, wgate_ref, wsm_ref, wsmt_ref,
                 pc_ref, pr_ref,
                 qkvb_ref, fk_ref, fv_ref, zr_ref, gqkv_ref, gz_ref, zg_ref, sm_ref, smt_ref):
    h = _rms(x_ref[...], g_ref[...]).astype(BF16)
    fox = _dot(h, wfox_ref[...])
    qkvb_ref[...] = fox.astype(BF16)
    fk_ref[...] = fox[:, FOX_W:2 * FOX_W]
    fv_ref[...] = fox[:, 2 * FOX_W:]
    zr_ref[...] = _dot(h, wrw_ref[...])
    gd = _dot(h, wgd_ref[...])
    gqkv_ref[...] = gd[:, :GD_QKV]
    gz_ref[...] = gd[:, GD_QKV:]
    zg_ref[...] = _dot(h, wgate_ref[...])
    pc = pc_ref[...]
    col = lax.broadcasted_iota(jnp.int32, (1, N_SMALL), 1)
    sm = _dot(h, wsm_ref[...]) + pc[0:1]
    sm_ref[...] = _small_fn(sm, col < 8, col < 12, -jnp.exp(pc[1:2]))
    pr = pr_ref[...]
    row = lax.broadcasted_iota(jnp.int32, (N_SMALL, 1), 0)
    smt = _dot_nt(wsmt_ref[...], h) + pr[:, 0:1]
    smt_ref[...] = _small_fn(smt, row < 8, row < 12, -jnp.exp(pr[:, 1:2]))


def _inproj(x, norm_g, w, layer, *, tm):
    m, d = x.shape
    n_gate = w["gate"].shape[2]
    n_gd = w["gd"].shape[2]
    row_blk = lambda n: pl.BlockSpec((tm, n), lambda i: (i, 0))
    wspec = lambda a: _const_spec((None,) + a.shape[1:], lambda i: (layer, 0, 0))
    out_shape = (
        jax.ShapeDtypeStruct((m, 3 * FOX_W), BF16),
        jax.ShapeDtypeStruct((m, FOX_W), F32),
        jax.ShapeDtypeStruct((m, FOX_W), F32),
        jax.ShapeDtypeStruct((m, RW_COLS), F32),
        jax.ShapeDtypeStruct((m, GD_QKV), F32),
        jax.ShapeDtypeStruct((m, n_gd - GD_QKV), F32),
        jax.ShapeDtypeStruct((m, n_gate), F32),
        jax.ShapeDtypeStruct((m, N_SMALL), F32),
        jax.ShapeDtypeStruct((N_SMALL, m), F32),
    )
    out_specs = (
        row_blk(3 * FOX_W), row_blk(FOX_W), row_blk(FOX_W), row_blk(RW_COLS), row_blk(GD_QKV),
        row_blk(n_gd - GD_QKV), row_blk(n_gate), row_blk(N_SMALL),
        pl.BlockSpec((N_SMALL, tm), lambda i: (0, i)),
    )
    return pl.pallas_call(
        _inproj_body,
        grid=(m // tm,),
        in_specs=[
            pl.BlockSpec((tm, d), lambda i: (i, 0)),
            _const_spec((None, 1, d), lambda i: (layer, 0, 0)),
            wspec(w["fox"]), wspec(w["rw"]), wspec(w["gd"]), wspec(w["gate"]),
            wspec(w["small"]), wspec(w["small_t"]), wspec(w["small_pc"]), wspec(w["small_pr"]),
        ],
        out_specs=out_specs,
        out_shape=out_shape,
        compiler_params=_cparams("parallel"),
        name="inproj",
    )(x, norm_g, w["fox"], w["rw"], w["gd"], w["gate"], w["small"], w["small_t"],
      w["small_pc"], w["small_pr"])


FOX_AW = 2 * HEAD_DIM
N_PIECES = 3


def _eye(n):
    return (lax.broadcasted_iota(jnp.int32, (n, n), 0) == lax.broadcasted_iota(jnp.int32, (n, n), 1)).astype(BF16)


def _fox_prompt_body(q_ref, k_ref, v_ref, lf_ref, pk_ref, pc_ref, one_ref, o_ref,
                     kaug_ref, vt_ref, qaug_ref, m_ref, l_ref, acc_ref, *, tq, cb):
    i = pl.program_id(1)
    seq = k_ref.shape[0]

    @pl.when(i == 0)
    def _():
        lower = _tri(cb, "incl").astype(BF16)
        eye = _eye(FOX_W)
        carry = jnp.zeros((1, N_SMALL), F32)
        for b0 in range(0, seq, cb):
            cc = _dot_sel_l(lower, lf_ref[b0:b0 + cb, :]) + carry
            carry = cc[cb - 1:cb, :]
            pieces = jnp.concatenate(_split3(-cc), axis=1)
            kaug_ref[b0:b0 + cb, :] = (_dot(k_ref[b0:b0 + cb, :], pk_ref[...])
                                       + _dot(pieces, pc_ref[...])).astype(BF16)
            vt_ref[:, b0:b0 + cb] = _dot_nt(eye, v_ref[b0:b0 + cb, :]).astype(BF16)

    qaug_ref[...] = (_dot(q_ref[...], pk_ref[...]) + one_ref[...]).astype(BF16)
    m_ref[...] = jnp.full(m_ref.shape, -jnp.inf, F32)
    l_ref[...] = jnp.zeros(l_ref.shape, F32)
    acc_ref[...] = jnp.zeros(acc_ref.shape, F32)

    def tile(j, diagonal):
        k0 = pl.multiple_of(j * tq, tq)
        if diagonal:
            visible = _tri(tq, "upper_incl")
        scores = [_dot_nt(kaug_ref[pl.ds(k0, tq), h * FOX_AW:(h + 1) * FOX_AW],
                          qaug_ref[:, h * FOX_AW:(h + 1) * FOX_AW]) for h in range(FOX_HEADS)]
        for h in range(FOX_HEADS):
            hs = slice(h * HEAD_DIM, (h + 1) * HEAD_DIM)
            st = scores[h]
            if diagonal:
                st = jnp.where(visible, st, -jnp.inf)
            m_old = m_ref[h]
            m_new = jnp.maximum(m_old, jnp.max(st, axis=0, keepdims=True))
            alpha = jnp.exp(m_old - m_new)
            p = jnp.exp(st - m_new)
            m_ref[h] = m_new
            l_ref[h] = alpha * l_ref[h] + jnp.sum(p, axis=0, keepdims=True)
            acc_ref[hs, :] = alpha * acc_ref[hs, :] + _dot(vt_ref[hs, pl.ds(k0, tq)], p.astype(BF16))

    def body(j, carry):
        tile(j, False)
        return carry

    lax.fori_loop(0, i, body, 0)
    tile(i, True)
    out_t = jnp.concatenate(
        [acc_ref[h * HEAD_DIM:(h + 1) * HEAD_DIM, :] / l_ref[h] for h in range(FOX_HEADS)], axis=0)
    o_ref[...] = _dot_nt(_eye(tq), out_t.astype(BF16)).astype(BF16)


def _fox_prompt(qkvb, sm, consts, batch, seq, *, tq):
    m = batch * seq
    nq = seq // tq
    cb = min(512, seq)
    body = functools.partial(_fox_prompt_body, tq=tq, cb=cb)
    aug_w = FOX_HEADS * FOX_AW
    return pl.pallas_call(
        body,
        grid=(batch, nq),
        in_specs=[
            pl.BlockSpec((tq, FOX_W), lambda b, i: (b * nq + i, 0)),
            pl.BlockSpec((seq, FOX_W), lambda b, i: (b, 1)),
            pl.BlockSpec((seq, FOX_W), lambda b, i: (b, 2)),
            pl.BlockSpec((seq, N_SMALL), lambda b, i: (b, 0)),
            _const_spec((FOX_W, aug_w), lambda b, i: (0, 0)),
            _const_spec((N_PIECES * N_SMALL, aug_w), lambda b, i: (0, 0)),
            _const_spec((1, aug_w), lambda b, i: (0, 0)),
        ],
        out_specs=pl.BlockSpec((tq, FOX_W), lambda b, i: (b * nq + i, 0)),
        out_shape=jax.ShapeDtypeStruct((m, FOX_W), BF16),
        scratch_shapes=[pltpu.VMEM((seq, aug_w), BF16), pltpu.VMEM((FOX_W, seq), BF16),
                        pltpu.VMEM((tq, aug_w), BF16),
                        pltpu.VMEM((FOX_HEADS, 1, tq), F32), pltpu.VMEM((FOX_HEADS, 1, tq), F32),
                        pltpu.VMEM((FOX_W, tq), F32)],
        compiler_params=_cparams("parallel", "arbitrary"),
        name="fox_prompt",
    )(qkvb, qkvb, qkvb, sm, consts["place_k"], consts["place_c"], consts["ones"])


def _fox_prompt_consts():
    aug_w = FOX_HEADS * FOX_AW
    lane = jnp.arange(aug_w)
    src = jnp.arange(FOX_W)
    place_k = (lane[None, :] == (src // HEAD_DIM * FOX_AW + src % HEAD_DIM)[:, None]).astype(BF16)
    col = jnp.arange(N_PIECES * N_SMALL)
    tgt = jnp.where(col % N_SMALL < FOX_HEADS, (col % N_SMALL) * FOX_AW + HEAD_DIM + col // N_SMALL, -1)
    place_c = (lane[None, :] == tgt[:, None]).astype(BF16)
    ones = ((lane % FOX_AW >= HEAD_DIM) & (lane % FOX_AW < HEAD_DIM + N_PIECES)).astype(F32)[None, :]
    return {"place_k": place_k, "place_c": place_c, "ones": ones}


def _fox_sample_body(pt_ref, q_ref, kn_ref, vn_ref, lfn_ref, *rest, t_valid, pps):
    kp_refs = rest[0:pps]
    vp_refs = rest[pps:2 * pps]
    lf_refs = rest[2 * pps:3 * pps]
    o_ref, qbd_ref, m_ref, l_ref, acc_ref, tail_ref = rest[3 * pps:]
    s_idx = pl.program_id(1)
    tp = SUBLANES
    nrow = t_valid * FOX_HEADS
    page = kp_refs[0].shape[1]
    diag = (lax.broadcasted_iota(jnp.int32, (nrow, FOX_W), 0) % FOX_HEADS
            == lax.broadcasted_iota(jnp.int32, (nrow, FOX_W), 1) // HEAD_DIM)

    def expand_rows(x):
        return jnp.concatenate([x] * t_valid, axis=0)

    @pl.when(s_idx == 0)
    def _():
        qf = q_ref[...].astype(F32)
        q_rows = jnp.concatenate([jnp.broadcast_to(qf[t:t + 1], (FOX_HEADS, FOX_W)) for t in range(t_valid)],
                                 axis=0)
        qbd = jnp.where(diag, q_rows, 0.0).astype(BF16)
        qbd_ref[...] = qbd
        upper = _tri(tp, "upper_incl").astype(BF16)
        lf_hi, lf_mid, lf_lo = _split3(lfn_ref[...])
        cn = (_dot_tn(lf_hi, upper) + _dot_tn(lf_mid, upper) + _dot_tn(lf_lo, upper))[:FOX_HEADS]
        s = _dot_nt(qbd, kn_ref[...]) - expand_rows(cn)
        t_q = lax.broadcasted_iota(jnp.int32, (nrow, tp), 0) // FOX_HEADS
        u_k = lax.broadcasted_iota(jnp.int32, (nrow, tp), 1)
        s = jnp.where((u_k <= t_q) & (u_k < t_valid), s, -jnp.inf)
        m = jnp.max(s, axis=-1, keepdims=True)
        p = jnp.exp(s - m)
        m_ref[...] = m
        l_ref[...] = jnp.sum(p, axis=-1, keepdims=True)
        acc_ref[...] = _dot(p.astype(BF16), vn_ref[...])
        tail_ref[...] = jnp.zeros_like(tail_ref)

    @pl.when(s_idx > 0)
    def _():
        later3 = jnp.concatenate([_tri(page, "strict").astype(BF16)] * 3, axis=0)
        qbd = qbd_ref[...]
        tail = tail_ref[...]
        scores = [None] * pps
        for r in reversed(range(pps)):
            lf = lf_refs[r][...]
            suf = _dot(jnp.concatenate(_split3(lf), axis=1), later3) + tail
            tail = tail + jnp.sum(lf, axis=-1, keepdims=True)
            scores[r] = _dot(qbd, kp_refs[r][...].astype(BF16)) + expand_rows(suf)
        tail_ref[...] = tail
        m_old = m_ref[...]
        m_new = m_old
        for r in range(pps):
            m_new = jnp.maximum(m_new, jnp.max(scores[r], axis=-1, keepdims=True))
        alpha = jnp.exp(m_old - m_new)
        l_new = alpha * l_ref[...]
        acc = alpha * acc_ref[...]
        for r in range(pps):
            p = jnp.exp(scores[r] - m_new)
            l_new = l_new + jnp.sum(p, axis=-1, keepdims=True)
            acc = acc + _dot_nt(p.astype(BF16), vp_refs[r][...].astype(BF16))
        m_ref[...] = m_new
        l_ref[...] = l_new
        acc_ref[...] = acc

    @pl.when(s_idx == pl.num_programs(1) - 1)
    def _():
        o = jnp.where(diag, acc_ref[...] / l_ref[...], 0.0).astype(BF16)
        pick = (lax.broadcasted_iota(jnp.int32, (tp, nrow), 1) // FOX_HEADS
                == lax.broadcasted_iota(jnp.int32, (tp, nrow), 0)).astype(BF16)
        o_ref[...] = _dot(pick, o).astype(BF16)


def _fox_sample(qkvb, sm, cache_kt, cache_vt, cache_lft, page_table, layer, *, t_valid, pps):
    tp = SUBLANES
    bd, n_pages = page_table.shape
    page = cache_kt.shape[3]
    nrow = t_valid * FOX_HEADS
    assert n_pages % pps == 0

    def page_spec(r, blk):
        def index_map(b, s, pt):
            return (layer, pt[b, n_pages - jnp.maximum(s, 1) * pps + r]) + (0,) * (len(blk) - 2)
        return pl.BlockSpec(blk, index_map)

    kv_blk = (None, None, FOX_W, page)
    lf_blk = (None, None, FOX_HEADS, page)
    grid_spec = pltpu.PrefetchScalarGridSpec(
        num_scalar_prefetch=1,
        grid=(bd, n_pages // pps + 1),
        in_specs=[
            pl.BlockSpec((tp, FOX_W), lambda b, s, pt: (b, 0)),
            pl.BlockSpec((tp, FOX_W), lambda b, s, pt: (b, 1)),
            pl.BlockSpec((tp, FOX_W), lambda b, s, pt: (b, 2)),
            pl.BlockSpec((tp, N_SMALL), lambda b, s, pt: (b, 0)),
            *[page_spec(r, kv_blk) for r in range(pps)],
            *[page_spec(r, kv_blk) for r in range(pps)],
            *[page_spec(r, lf_blk) for r in range(pps)],
        ],
        out_specs=pl.BlockSpec((tp, FOX_W), lambda b, s, pt: (b, 0)),
        scratch_shapes=[
            pltpu.VMEM((nrow, FOX_W), BF16),
            pltpu.VMEM((nrow, 1), F32),
            pltpu.VMEM((nrow, 1), F32),
            pltpu.VMEM((nrow, FOX_W), F32),
            pltpu.VMEM((FOX_HEADS, 1), F32),
        ],
    )
    body = functools.partial(_fox_sample_body, t_valid=t_valid, pps=pps)
    return pl.pallas_call(
        body,
        grid_spec=grid_spec,
        out_shape=jax.ShapeDtypeStruct((bd * tp, FOX_W), BF16),
        compiler_params=_cparams("parallel", "arbitrary"),
        name="fox_sample",
    )(page_table, qkvb, qkvb, qkvb, sm, *([cache_kt] * pps), *([cache_vt] * pps), *([cache_lft] * pps))


def _rwkv_body(z_ref, shift_ref, s0_ref, mu_ref, pv_ref, wup_ref, aup_ref, gup_ref, seg_ref,
               o_ref, so_ref, sbd_ref, prev_ref, *, chunk, n_chunks, t_valid):
    c = pl.program_id(1)
    tb = chunk * n_chunks
    rows_x = GROUP * chunk
    lane_mask = _head_mask(rows_x, chunk, RW_GW, HEAD_DIM)
    amask = _head_mask(rows_x, chunk, rows_x, chunk)
    state_mask = _head_mask(RW_GW, HEAD_DIM, RW_GW, HEAD_DIM)

    @pl.when(c == 0)
    def _():
        prev_ref[0:1, :] = shift_ref[...]
        for g in range(RW_GROUPS):
            blocks = [jnp.concatenate([s0_ref[g * GROUP + h]] * GROUP, axis=1) for h in range(GROUP)]
            sbd_ref[g] = jnp.where(state_mask, jnp.concatenate(blocks, axis=0), 0.0)

    z = z_ref[...]
    row = lax.broadcasted_iota(jnp.int32, (tb, 1), 0)
    z_prev = jnp.where(row == 0, prev_ref[0:1, :], pltpu.roll(z, 1, axis=0))
    prev_ref[0:1, :] = z[tb - 1:tb, :]
    zs = z + (z_prev - z) * mu_ref[...]
    r = zs[:, 0:RW_W]
    k = zs[:, RW_W:2 * RW_W]
    v = zs[:, 2 * RW_W:3 * RW_W]
    o_l = 3 * RW_W
    wd = zs[:, o_l:o_l + RW_LORA_W]
    ad = zs[:, o_l + RW_LORA_W:o_l + RW_LORA_W + RW_LORA_A]
    gd = zs[:, o_l + RW_LORA_W + RW_LORA_A:]
    pv = pv_ref[...]
    w_log = -_softplus(-(pv[0:1] + _dot(jnp.tanh(wd).astype(BF16), wup_ref[...]))) - 0.5
    lw = -jnp.exp(w_log)
    a = _sigmoid(pv[1:2] + _dot(ad.astype(BF16), aup_ref[...]))
    g_out = _dot(_sigmoid(gd).astype(BF16), gup_ref[...])
    kk = k * pv[2:3]
    k = k * (1.0 + (a - 1.0) * pv[3:4])
    seg2 = seg_ref[...]
    kk = kk * lax.rsqrt(_seg_sum(kk * kk, seg2) + L2_EPS)
    if t_valid < chunk:
        valid = row < t_valid
        lw = jnp.where(valid, lw, 0.0)
        kk = jnp.where(valid, kk, 0.0)
        k = jnp.where(valid, k, 0.0)
    cs = _dot_sel_l(_block_tri(tb, chunk, strict=False).astype(BF16), lw)
    p_inv = jnp.exp(-cs)
    rp = (r * jnp.exp(cs)).astype(BF16)
    kkp = (kk * jnp.exp(cs - lw)).astype(BF16)
    bn = kk * a * p_inv
    kn = k * p_inv
    vb = v.astype(BF16)
    t_idx = lax.broadcasted_iota(jnp.int32, (chunk, 2 * rows_x), 0)
    j_idx = lax.broadcasted_iota(jnp.int32, (chunk, 2 * rows_x), 1) % chunk
    strict = (t_idx > j_idx)[:, :rows_x]
    incl2 = t_idx >= j_idx
    groups = range(RW_GROUPS)
    gsl = [slice(g * RW_GW, (g + 1) * RW_GW) for g in groups]
    bnb = bn.astype(BF16)
    knb = kn.astype(BF16)

    def prepare(ci, g):
        def gen(done):
            rs = slice(ci * chunk, (ci + 1) * chunk)
            bk_x = jnp.concatenate([_expand(bnb[rs, gsl[g]], lane_mask), _expand(knb[rs, gsl[g]], lane_mask)],
                                   axis=0)
            a_all = _dot_nt(jnp.concatenate([kkp[rs, gsl[g]], rp[rs, gsl[g]]], axis=0), bk_x)
            yield
            a_ak = jnp.where(strict, a_all[:chunk, rows_x:], 0.0).astype(BF16)
            a_r = jnp.where(incl2, a_all[chunk:, :], 0.0).astype(BF16)
            t_m1 = yield from _unit_lower_inverse_minus_eye(
                jnp.where(strict, a_all[:chunk, :rows_x], 0.0), chunk, amask)
            return a_ak, a_r, t_m1.astype(BF16)
        return gen

    def advance(ci):
        def gen(done):
            rs = slice(ci * chunk, (ci + 1) * chunk)
            pre = [done["prepare", ci, g] for g in groups]
            p_end = jnp.exp(cs[(ci + 1) * chunk - 1:(ci + 1) * chunk, :])
            bk_e = jnp.concatenate([(bn[rs] * p_end).astype(BF16), (kn[rs] * p_end).astype(BF16)], axis=0)
            v_xs = [_expand(vb[rs, gsl[g]], lane_mask) for g in groups]
            sts = [sbd_ref[g] for g in groups]
            stbs = [st.astype(BF16) for st in sts]
            rhs = [_dot_nt(kkp[rs, gsl[g]], stbs[g]) + _dot(pre[g][0], v_xs[g]) for g in groups]
            yield
            sabs = [(-(rhs[g] + _dot(pre[g][2], _expand(rhs[g].astype(BF16), lane_mask)))).astype(BF16)
                    for g in groups]
            yield
            for g in groups:
                outer = _dot_tn(jnp.concatenate([sabs[g], vb[rs, gsl[g]]], axis=0), bk_e[:, gsl[g]])
                sbd_ref[g] = sts[g] * p_end[:, gsl[g]] + jnp.where(state_mask, outer, 0.0)
            yield
            y_c = jnp.concatenate(
                [_dot_nt(rp[rs, gsl[g]], stbs[g])
                 + _dot(pre[g][1], jnp.concatenate([_expand(sabs[g], lane_mask), v_xs[g]], axis=0))
                 for g in groups], axis=1)
            yield
            return y_c
        return gen

    tasks = []
    for ci in range(n_chunks):
        tasks.append((("advance", ci),
                      [("prepare", ci, g) for g in groups] + ([("advance", ci - 1)] if ci else []), [],
                      advance(ci)))
        for g in groups:
            tasks.append((("prepare", ci, g), [], [], prepare(ci, g)))
    done = _run_tasks(tasks)
    y_rows = [done["advance", ci] for ci in range(n_chunks)]
    y = y_rows[0] if n_chunks == 1 else jnp.concatenate(y_rows, axis=0)
    inv_n = 1.0 / HEAD_DIM
    mean = _seg_sum(y, seg2) * inv_n
    yc = y - mean
    var = _seg_sum(yc * yc, seg2) * inv_n
    yn = yc * lax.rsqrt(var + RW_GN_EPS) * pv[5:6] + pv[6:7]
    bonus = _seg_sum(r * k * pv[4:5], seg2) * v
    o_ref[...] = ((yn + bonus) * g_out).astype(BF16)

    @pl.when(c == pl.num_programs(1) - 1)
    def _():
        for g in range(RW_GROUPS):
            st = sbd_ref[g]
            for h in range(GROUP):
                hs = slice(h * HEAD_DIM, (h + 1) * HEAD_DIM)
                so_ref[g * GROUP + h] = st[hs, hs]


def _rwkv(zr, shift0, s0, w, layer, batch, seq, *, chunk, n_chunks, t_valid):
    m = batch * seq
    tb = chunk * n_chunks
    nc = seq // tb
    body = functools.partial(_rwkv_body, chunk=chunk, n_chunks=n_chunks, t_valid=t_valid)
    lspec = lambda a: _const_spec((None,) + a.shape[1:], lambda b, c: (layer, 0, 0))
    state_spec = pl.BlockSpec((None, RW_HEADS, HEAD_DIM, HEAD_DIM), lambda b, c: (b, 0, 0, 0))
    s0_layer = layer if s0.shape[0] > 1 else 0
    return pl.pallas_call(
        body,
        grid=(batch, nc),
        in_specs=[
            pl.BlockSpec((tb, RW_COLS), lambda b, c: (b * nc + c, 0)),
            pl.BlockSpec((None, 1, RW_COLS), lambda b, c: (b, 0, 0)),
            pl.BlockSpec((None, None, RW_HEADS, HEAD_DIM, HEAD_DIM), lambda b, c: (s0_layer, b, 0, 0, 0)),
            lspec(w["mu"]), lspec(w["pv"]), lspec(w["w_up"]), lspec(w["a_up"]), lspec(w["g_up"]),
            _const_spec((2 * RW_W, RW_W), lambda b, c: (0, 0)),
        ],
        out_specs=(pl.BlockSpec((tb, RW_W), lambda b, c: (b * nc + c, 0)), state_spec),
        out_shape=(
            jax.ShapeDtypeStruct((m, RW_W), BF16),
            jax.ShapeDtypeStruct((batch, RW_HEADS, HEAD_DIM, HEAD_DIM), F32),
        ),
        scratch_shapes=[pltpu.VMEM((RW_GROUPS, RW_GW, RW_GW), F32), pltpu.VMEM((SUBLANES, RW_COLS), F32)],
        compiler_params=_cparams("parallel", "arbitrary"),
        name="rwkv",
    )(zr, shift0, s0, w["mu"], w["pv"], w["w_up"], w["a_up"], w["g_up"], w["seg2"])


def _gdn_body(x_ref, gz_ref, sm_ref, conv0_ref, s0_ref, cw_ref, ng_ref, seg_ref,
              o_ref, so_ref, prev_ref, *, chunk, n_chunks, t_valid):
    c = pl.program_id(1)
    tb = chunk * n_chunks
    rows_x = GD_HEADS * chunk
    lane_mask = _head_mask(rows_x, chunk, GD_W, GD_DK)
    amask = _head_mask(rows_x, chunk, rows_x, chunk)

    @pl.when(c == 0)
    def _():
        prev_ref[...] = conv0_ref[...]
        so_ref[...] = s0_ref[...]

    x = x_ref[...]
    prev = prev_ref[...]
    prev_ref[...] = x[tb - SUBLANES:, :]
    cw = cw_ref[...]
    row8 = lax.broadcasted_iota(jnp.int32, (SUBLANES, 1), 0)
    conv = x * cw[GD_CONV - 1:GD_CONV]
    for sft in range(1, GD_CONV):
        xr = pltpu.roll(x, sft, axis=0)
        top = jnp.where(row8 < sft, pltpu.roll(prev, sft, axis=0), xr[:SUBLANES])
        xs = top if tb == SUBLANES else jnp.concatenate([top, xr[SUBLANES:]], axis=0)
        conv = conv + xs * cw[GD_CONV - 1 - sft:GD_CONV - sft]
    act = conv * _sigmoid(conv)
    seg2 = seg_ref[...]
    q = act[:, :GD_W]
    k = act[:, GD_W:2 * GD_W]
    v = act[:, 2 * GD_W:]
    q = q * lax.rsqrt(_seg_sum(q * q, seg2) + L2_EPS) * (GD_DK ** -0.5)
    k = k * lax.rsqrt(_seg_sum(k * k, seg2) + L2_EPS)
    sm = sm_ref[...]
    beta = sm[:, 8:8 + GD_HEADS]
    g = sm[:, 12:12 + GD_HEADS]
    if t_valid < chunk:
        rowc = lax.broadcasted_iota(jnp.int32, (tb, 1), 0)
        beta = jnp.where(rowc < t_valid, beta, 0.0)
        g = jnp.where(rowc < t_valid, g, 0.0)
    gc = _dot_sel_l(_block_tri(tb, chunk, strict=False).astype(BF16), g)
    head_lanes = _head_mask(GD_HEADS, 1, GD_W, GD_DK).astype(BF16)
    beta_l = _dot_sel_r(beta, head_lanes)
    gc_l = _dot_sel_r(gc, head_lanes)
    egc_l = jnp.exp(gc_l)
    kb = k * beta_l
    vbeta = (v * beta_l).astype(BF16)
    kbg = (kb * egc_l).astype(BF16)
    qg = (q * egc_l).astype(BF16)
    kb = kb.astype(BF16)
    qb = q.astype(BF16)
    kbf = k.astype(BF16)
    t_idx = lax.broadcasted_iota(jnp.int32, (chunk, rows_x), 0)
    s_idx = lax.broadcasted_iota(jnp.int32, (chunk, rows_x), 1) % chunk
    strict = t_idx > s_idx
    incl = t_idx >= s_idx
    diag = t_idx == s_idx
    head_cols = _head_mask(GD_HEADS, 1, rows_x, chunk).astype(BF16)
    ones_cc = jnp.ones((chunk, chunk), BF16)
    hsl = [slice(h * GD_DK, (h + 1) * GD_DK) for h in range(GD_HEADS)]

    def prepare(ci):
        def gen(done):
            rs = slice(ci * chunk, (ci + 1) * chunk)
            g_t = _dot_sel_r(gc[rs], head_cols)
            yield
            g_s = _dot_sel_l(ones_cc, jnp.where(diag, g_t, 0.0))
            qk = _dot_nt(jnp.concatenate([kb[rs], qb[rs]], axis=0), _expand(kbf[rs], lane_mask))
            yield
            decay = jnp.exp(jnp.minimum(g_t - g_s, 0.0))
            amat = jnp.where(incl, qk[chunk:] * decay, 0.0).astype(BF16)
            t_m1 = yield from _unit_lower_inverse_minus_eye(
                jnp.where(strict, qk[:chunk] * decay, 0.0), chunk, amask)
            vk = jnp.concatenate([vbeta[rs], kbg[rs]], axis=1)
            vk_x = jnp.concatenate([_expand(vbeta[rs], lane_mask), _expand(kbg[rs], lane_mask)], axis=1)
            uw = vk.astype(F32) + _dot(t_m1.astype(BF16), vk_x)
            yield
            return amat, uw
        return gen

    def advance(ci):
        def gen(done):
            rs = slice(ci * chunk, (ci + 1) * chunk)
            amat, uw = done["prepare", ci]
            glast_l = gc_l[(ci + 1) * chunk - 1:(ci + 1) * chunk]
            kg = (k[rs] * jnp.exp(glast_l - gc_l[rs])).astype(BF16)
            wb = uw[:, GD_W:].astype(BF16)
            sts = [so_ref[h] for h in range(GD_HEADS)]
            stbs = [st.astype(BF16) for st in sts]
            v_news = [uw[:, hs] - _dot(wb[:, hs], stb) for hs, stb in zip(hsl, stbs)]
            yield
            vnb = jnp.concatenate(v_news, axis=1).astype(BF16)
            for hs, st, h in zip(hsl, sts, range(GD_HEADS)):
                so_ref[h] = st * jnp.exp(glast_l[:, hs]) + _dot_tn(kg[:, hs], vnb[:, hs])
            yield
            o_c = (jnp.concatenate([_dot(qg[rs, hs], stb) for hs, stb in zip(hsl, stbs)], axis=1)
                   + _dot(amat, _expand(vnb, lane_mask)))
            yield
            return o_c
        return gen

    tasks = []
    for ci in range(n_chunks):
        tasks.append((("advance", ci), [("prepare", ci)] + ([("advance", ci - 1)] if ci else []), [],
                      advance(ci)))
        tasks.append((("prepare", ci), [], [], prepare(ci)))
    done = _run_tasks(tasks)
    o_rows = [done["advance", ci] for ci in range(n_chunks)]
    o = o_rows[0] if n_chunks == 1 else jnp.concatenate(o_rows, axis=0)
    o = o * lax.rsqrt(_seg_sum(o * o, seg2) * (1.0 / GD_DV) + EPS) * ng_ref[...]
    gz = gz_ref[...]
    o_ref[...] = (o * (gz * _sigmoid(gz))).astype(BF16)


def _gdn(gqkv, gz, sm, conv0, s0, w, layer, batch, seq, *, chunk, n_chunks, t_valid):
    m = batch * seq
    tb = chunk * n_chunks
    nc = seq // tb
    body = functools.partial(_gdn_body, chunk=chunk, n_chunks=n_chunks, t_valid=t_valid)
    lspec = lambda a: _const_spec((None,) + a.shape[1:], lambda b, c: (layer, 0, 0))
    state_spec = pl.BlockSpec((None, GD_HEADS, GD_DK, GD_DV), lambda b, c: (b, 0, 0, 0))
    s0_layer = layer if s0.shape[0] > 1 else 0
    return pl.pallas_call(
        body,
        grid=(batch, nc),
        in_specs=[
            pl.BlockSpec((tb, GD_QKV), lambda b, c: (b * nc + c, 0)),
            pl.BlockSpec((tb, GD_HEADS * GD_DV), lambda b, c: (b * nc + c, 0)),
            pl.BlockSpec((tb, N_SMALL), lambda b, c: (b * nc + c, 0)),
            pl.BlockSpec((None, SUBLANES, GD_QKV), lambda b, c: (b, 0, 0)),
            pl.BlockSpec((None, None, GD_HEADS, GD_DK, GD_DV), lambda b, c: (s0_layer, b, 0, 0, 0)),
            lspec(w["conv"]), lspec(w["norm_g"]),
            _const_spec((2 * GD_W, GD_W), lambda b, c: (0, 0)),
        ],
        out_specs=(pl.BlockSpec((tb, GD_HEADS * GD_DV), lambda b, c: (b * nc + c, 0)), state_spec),
        out_shape=(
            jax.ShapeDtypeStruct((m, GD_HEADS * GD_DV), BF16),
            jax.ShapeDtypeStruct((batch, GD_HEADS, GD_DK, GD_DV), F32),
        ),
        scratch_shapes=[pltpu.VMEM((SUBLANES, GD_QKV), F32)],
        compiler_params=_cparams("parallel", "arbitrary"),
        name="gdn",
    )(gqkv, gz, sm, conv0, s0, w["conv"], w["norm_g"], w["seg2"])


def _merge_body(x_ref, of_ref, or_ref, og_ref, zg_ref, wf_ref, wr_ref, wg_ref, wo_ref, o_ref):
    d = x_ref.shape[1]
    merged = _sigmoid(zg_ref[:, 0:d]) * _dot(of_ref[...], wf_ref[...])
    merged = merged + _sigmoid(zg_ref[:, d:2 * d]) * _dot(or_ref[...], wr_ref[...])
    merged = merged + _sigmoid(zg_ref[:, 2 * d:3 * d]) * _dot(og_ref[...], wg_ref[...])
    o_ref[...] = x_ref[...] + _dot(merged.astype(BF16), wo_ref[...])


def _merge(x, o_fox, o_rw, o_gd, zg, w, layer, *, tm):
    m, d = x.shape
    row_blk = lambda n: pl.BlockSpec((tm, n), lambda i: (i, 0))
    wspec = lambda a: _const_spec((None,) + a.shape[1:], lambda i: (layer, 0, 0))
    return pl.pallas_call(
        _merge_body,
        grid=(m // tm,),
        in_specs=[row_blk(d), row_blk(FOX_W), row_blk(RW_W), row_blk(GD_HEADS * GD_DV),
                  row_blk(N_BRANCH * d),
                  wspec(w["br_fox"]), wspec(w["br_rw"]), wspec(w["br_gd"]), wspec(w["out"])],
        out_specs=row_blk(d),
        out_shape=jax.ShapeDtypeStruct((m, d), F32),
        compiler_params=_cparams("parallel"),
        name="merge",
    )(x, o_fox, o_rw, o_gd, zg, w["br_fox"], w["br_rw"], w["br_gd"], w["out"])


def _same_segment2(n, width):
    idx = jnp.arange(n) // width
    seg = (idx[:, None] == idx[None, :]).astype(BF16)
    return jnp.concatenate([seg, seg], axis=0)


def _row_tile(m):
    for tm in (512, 256, 128, 64, 32, 16, 8):
        if m % tm == 0:
            return tm
    raise ValueError(f"row count {m} is not a multiple of 8")


def _pages_per_step(n_pages):
    for pps in (16, 8, 4, 2, 1):
        if n_pages % pps == 0:
            return pps


def kernel(x_prompt, x_sample, cache_k, cache_v, cache_logf, state_rwkv_shift, state_rwkv,
           state_gdn_conv, state_gdn, page_table, norm_ffn1, ffn1_wg, ffn1_wu, ffn1_wd, norm_mix,
           w_in, fox_fb, rw_mu, rw_w0, rw_w_up, rw_a0, rw_a_up, rw_g_up, rw_kk, rw_ka, rw_rk,
           rw_ln_g, rw_ln_b, gd_conv, gd_a_log, gd_dt_bias, gd_norm_g, w_br_fox, w_br_rw, w_br_gd,
           w_out, norm_ffn2, ffn2_wg, ffn2_wu, ffn2_wd, final_norm):
    bp, seq, d = x_prompt.shape
    bd, t_s, _ = x_sample.shape
    depth = w_in.shape[0]
    tp = SUBLANES
    chunk_p = 64
    n_chunks_p = 4 if seq % (4 * chunk_p) == 0 else 1
    assert t_s <= tp and seq % chunk_p == 0

    o_rw_c = 3 * FOX_W + FOX_HEADS
    o_gd_c = o_rw_c + RW_COLS
    o_gz = o_gd_c + GD_QKV
    o_gb = o_gz + GD_HEADS * GD_DV
    o_ga = o_gb + GD_HEADS
    o_gate = o_ga + GD_HEADS
    scale = HEAD_DIM ** -0.5
    w_small = jnp.concatenate([w_in[:, :, 3 * FOX_W:o_rw_c], w_in[:, :, o_gb:o_gate]], axis=-1)
    zeros4 = jnp.zeros((depth, GD_HEADS), F32)
    small_bias = jnp.concatenate([fox_fb, zeros4, gd_dt_bias], axis=-1)
    small_alog = jnp.concatenate([jnp.zeros((depth, 8), F32), zeros4, gd_a_log], axis=-1)
    w_proj = {
        "fox": jnp.concatenate([w_in[:, :, :FOX_W] * scale, w_in[:, :, FOX_W:3 * FOX_W]], axis=-1).astype(BF16),
        "rw": w_in[:, :, o_rw_c:o_gd_c].astype(BF16),
        "gd": w_in[:, :, o_gd_c:o_gb].astype(BF16),
        "gate": w_in[:, :, o_gate:].astype(BF16),
        "small": w_small.astype(BF16),
        "small_t": jnp.swapaxes(w_small, 1, 2).astype(BF16),
        "small_pc": jnp.stack([small_bias, small_alog], axis=1),
        "small_pr": jnp.stack([small_bias, small_alog], axis=2),
    }
    w_rw = {
        "mu": rw_mu[:, None, :],
        "pv": jnp.stack([rw_w0, rw_a0, rw_kk, rw_ka, rw_rk.reshape(depth, RW_W), rw_ln_g, rw_ln_b,
                         jnp.zeros_like(rw_w0)], axis=1),
        "w_up": rw_w_up.astype(BF16), "a_up": rw_a_up.astype(BF16), "g_up": rw_g_up.astype(BF16),
        "seg2": _same_segment2(RW_W, HEAD_DIM),
    }
    w_gd = {
        "conv": jnp.pad(gd_conv, ((0, 0), (0, SUBLANES - GD_CONV), (0, 0))),
        "norm_g": jnp.tile(gd_norm_g, (1, GD_HEADS))[:, None, :],
        "seg2": _same_segment2(GD_W, GD_DK),
    }
    w_mg = {"br_fox": w_br_fox.astype(BF16), "br_rw": w_br_rw.astype(BF16),
            "br_gd": w_br_gd.astype(BF16), "out": w_out.astype(BF16)}
    ffn1 = (norm_ffn1[:, None, :], ffn1_wg.astype(BF16), ffn1_wu.astype(BF16), ffn1_wd.astype(BF16))
    ffn2 = (norm_ffn2[:, None, :], ffn2_wg.astype(BF16), ffn2_wu.astype(BF16), ffn2_wd.astype(BF16))
    norm_mix3 = norm_mix[:, None, :]
    final_g = final_norm[None, :]
    fox_consts = _fox_prompt_consts()

    n_pool, page = cache_k.shape[1], cache_k.shape[2]
    ckt = jnp.transpose(cache_k, (0, 1, 3, 4, 2)).reshape(depth, n_pool, FOX_W, page)
    cvt = jnp.transpose(cache_v, (0, 1, 3, 4, 2)).reshape(depth, n_pool, FOX_W, page)
    clft = jnp.swapaxes(cache_logf, 2, 3)
    pps = _pages_per_step(page_table.shape[1])
    gd_conv0_s = jnp.pad(state_gdn_conv, ((0, 0), (0, 0), (SUBLANES - (GD_CONV - 1), 0), (0, 0)))
    zeros_p = {
        "shift": jnp.zeros((bp, 1, RW_COLS), F32),
        "rw_s": jnp.zeros((1, bp, RW_HEADS, HEAD_DIM, HEAD_DIM), F32),
        "conv": jnp.zeros((bp, SUBLANES, GD_QKV), F32),
        "gd_s": jnp.zeros((1, bp, GD_HEADS, GD_DK, GD_DV), F32),
    }

    xp = x_prompt.reshape(bp * seq, d)
    xs = jnp.pad(x_sample, ((0, 0), (0, tp - t_s), (0, 0))).reshape(bd * tp, d)
    tm_p, tm_s = _row_tile(bp * seq), _row_tile(bd * tp)
    tm_in_p = min(tm_p, 256)
    tq = 512 if seq % 512 == 0 else min(256, seq)

    def layer_fn(x, l, *, batch, t_len, tm, tm_in, fox_fn, shift0, rw_s0, conv0, gd_s0, chunk, n_chunks,
                 t_valid, last):
        x = _ffn(x, *ffn1, final_g, l, tm=tm)
        qkvb, fk, fv, zr, gqkv, gz, zg, sm, smt = _inproj(x, norm_mix3, w_proj, l, tm=tm_in)
        o_fox = fox_fn(qkvb, sm, smt, l)
        o_rw, rw_st = _rwkv(zr, shift0, rw_s0, w_rw, l, batch, t_len, chunk=chunk, n_chunks=n_chunks,
                            t_valid=t_valid)
        o_gd, gd_s = _gdn(gqkv, gz, sm, conv0, gd_s0, w_gd, l, batch, t_len, chunk=chunk, n_chunks=n_chunks,
                          t_valid=t_valid)
        x = _merge(x, o_fox, o_rw, o_gd, zg, w_mg, l, tm=tm)
        x = _ffn(x, *ffn2, final_g, l, tm=tm, final_norm=last)
        return x, (fk, fv, sm, zr, rw_st, gqkv, gd_s)

    p_states, s_states = [], []
    for l in range(depth):
        last = l == depth - 1
        xp, st = layer_fn(
            xp, l, batch=bp, t_len=seq, tm=tm_p, tm_in=tm_in_p,
            fox_fn=lambda qkvb, sm, smt, l: _fox_prompt(qkvb, sm, fox_consts, bp, seq, tq=tq),
            shift0=zeros_p["shift"], rw_s0=zeros_p["rw_s"], conv0=zeros_p["conv"], gd_s0=zeros_p["gd_s"],
            chunk=chunk_p, n_chunks=n_chunks_p, t_valid=chunk_p, last=last)
        fk, fv, sm, zr, rw_st, gqkv, gd_s = st
        p_states.append((
            fk.reshape(bp, seq, FOX_HEADS, HEAD_DIM), fv.reshape(bp, seq, FOX_HEADS, HEAD_DIM),
            sm[:, :FOX_HEADS].reshape(bp, seq, FOX_HEADS),
            zr.reshape(bp, seq, RW_COLS)[:, seq - 1],
            rw_st,
            gqkv.reshape(bp, seq, GD_QKV)[:, seq - (GD_CONV - 1):],
            gd_s))
        xs, st = layer_fn(
            xs, l, batch=bd, t_len=tp, tm=tm_s, tm_in=tm_s,
            fox_fn=lambda qkvb, sm, smt, l: _fox_sample(qkvb, sm, ckt, cvt, clft, page_table, l,
                                                        t_valid=t_s, pps=pps),
            shift0=state_rwkv_shift[l][:, None, :], rw_s0=state_rwkv, conv0=gd_conv0_s[l],
            gd_s0=state_gdn, chunk=tp, n_chunks=1, t_valid=t_s, last=last)
        fk, fv, sm, zr, rw_st, gqkv, gd_s = st
        conv_ext = jnp.concatenate([state_gdn_conv[l], gqkv.reshape(bd, tp, GD_QKV)[:, :t_s]], axis=1)
        s_states.append((
            fk.reshape(bd, tp, FOX_HEADS, HEAD_DIM)[:, :t_s], fv.reshape(bd, tp, FOX_HEADS, HEAD_DIM)[:, :t_s],
            sm[:, :FOX_HEADS].reshape(bd, tp, FOX_HEADS)[:, :t_s],
            zr.reshape(bd, tp, RW_COLS)[:, t_s - 1],
            rw_st,
            conv_ext[:, t_s:],
            gd_s))

    p_out = [jnp.stack(s) for s in zip(*p_states)]
    s_out = [jnp.stack(s) for s in zip(*s_states)]
    y_prompt = xp.reshape(bp, seq, d)
    y_sample = xs.reshape(bd, tp, d)[:, :t_s]
    return (y_prompt, y_sample, *p_out, *s_out)
```

```python
import functools

import jax
import jax.numpy as jnp
from jax import lax
from jax.experimental import pallas as pl
from jax.experimental.pallas import tpu as pltpu

F32 = jnp.float32
BF16 = jnp.bfloat16

HEAD_DIM = 64
FOX_HEADS = 8
FOX_W = FOX_HEADS * HEAD_DIM
RW_HEADS = 8
RW_W = RW_HEADS * HEAD_DIM
RW_LORA_W = 64
RW_LORA_A = 64
RW_LORA_G = 128
RW_COLS = 3 * RW_W + RW_LORA_W + RW_LORA_A + RW_LORA_G
RW_GN_EPS = 64e-5
GD_HEADS = 4
GD_DK = 128
GD_DV = 128
GD_W = GD_HEADS * GD_DK
GD_QKV = 2 * GD_W + GD_HEADS * GD_DV
GD_CONV = 4
N_BRANCH = 3
EPS = 1e-6
L2_EPS = 1e-6
N_SMALL = 16
SUBLANES = 8
GROUP = 4
RW_GROUPS = RW_HEADS // GROUP
RW_GW = GROUP * HEAD_DIM
VMEM_LIMIT = 56 * 1024 * 1024


def _cparams(*sem):
    return pltpu.CompilerParams(dimension_semantics=sem, vmem_limit_bytes=VMEM_LIMIT)


def _const_spec(shape, index_map):
    return pl.BlockSpec(shape, index_map, pipeline_mode=pl.Buffered(1))


def _dot(a, b):
    return jnp.dot(a, b, preferred_element_type=F32)


def _dot_nt(a, b):
    return lax.dot_general(a, b, (((1,), (1,)), ((), ())), preferred_element_type=F32)


def _dot_tn(a, b):
    return lax.dot_general(a, b, (((0,), (0,)), ((), ())), preferred_element_type=F32)


def _split2(x):
    hi = x.astype(BF16)
    lo = (x - hi.astype(F32)).astype(BF16)
    return hi, lo


def _split3(x):
    hi = x.astype(BF16)
    r = x - hi.astype(F32)
    mid = r.astype(BF16)
    lo = (r - mid.astype(F32)).astype(BF16)
    return hi, mid, lo


def _dot_sel_r(x, sel):
    hi, mid, lo = _split3(x)
    return _dot(hi, sel) + _dot(mid, sel) + _dot(lo, sel)


def _dot_sel_l(sel, x):
    return _dot(jnp.concatenate([sel] * 3, axis=1), jnp.concatenate(_split3(x), axis=0))


def _seg_sum(x, seg2):
    return _dot(jnp.concatenate(_split2(x), axis=1), seg2)


def _softplus(z):
    return jnp.maximum(z, 0.0) + jnp.log(1.0 + jnp.exp(-jnp.abs(z)))


def _sigmoid(z):
    return 1.0 / (1.0 + jnp.exp(-z))


def _rms(x, g):
    return x * lax.rsqrt(jnp.mean(x * x, axis=-1, keepdims=True) + EPS) * g


def _tri(n, kind):
    r = lax.broadcasted_iota(jnp.int32, (n, n), 0)
    c = lax.broadcasted_iota(jnp.int32, (n, n), 1)
    return {"incl": r >= c, "strict": r > c, "upper_incl": r <= c}[kind]


def _block_tri(n, blk, strict, reps=1):
    r = lax.broadcasted_iota(jnp.int32, (n, n * reps), 0)
    c = lax.broadcasted_iota(jnp.int32, (n, n * reps), 1) % n
    same = (r // blk) == (c // blk)
    return same & ((r > c) if strict else (r >= c))


def _head_mask(rows, row_blk, cols, col_blk):
    r = lax.broadcasted_iota(jnp.int32, (rows, cols), 0) // row_blk
    c = lax.broadcasted_iota(jnp.int32, (rows, cols), 1) // col_blk
    return r == c


def _expand(x, mask):
    return jnp.where(mask, jnp.concatenate([x] * GROUP, axis=0), jnp.zeros((), x.dtype))


def _collapse(x, t):
    out = x[0:t]
    for h in range(1, GROUP):
        out = out + x[h * t:(h + 1) * t]
    return out


def _unit_lower_inverse_minus_eye(m_wide, order, amask):
    width = m_wide.shape[1]
    npow = -m_wide
    t = npow
    if order <= 2:
        return t
    nb = npow.astype(BF16)
    npow = _dot(nb, _expand(nb, amask))
    yield
    span = 4
    while span <= order:
        nb = npow.astype(BF16)
        tb = _expand(t.astype(BF16), amask)
        if span < order:
            prod = _dot(nb, jnp.concatenate([tb, _expand(nb, amask)], axis=1))
            t = t + npow + prod[:, :width]
            npow = prod[:, width:]
        else:
            t = t + npow + _dot(nb, tb)
        yield
        span *= 2
    return t


def _run_tasks(tasks):
    done, started, active = {}, set(), []
    pending = list(tasks)
    while pending or active:
        for task in list(pending):
            name, needs_done, needs_started, fn = task
            if all(n in done for n in needs_done) and all(n in started for n in needs_started):
                pending.remove(task)
                started.add(name)
                active.append((name, fn(done)))
        assert active, "task dependencies can never be met"
        for entry in list(active):
            name, gen = entry
            try:
                next(gen)
            except StopIteration as stop:
                done[name] = stop.value
                active.remove(entry)
    return done


def _ffn_body(x_ref, g_ref, wg_ref, wu_ref, wd_ref, fg_ref, o_ref, *, ff_chunk, final_norm):
    x = x_ref[...]
    h = _rms(x, g_ref[...]).astype(BF16)
    d_ff = wg_ref.shape[1]
    acc = jnp.zeros_like(x)
    for c0 in range(0, d_ff, ff_chunk):
        gate = _dot(h, wg_ref[:, c0:c0 + ff_chunk])
        up = _dot(h, wu_ref[:, c0:c0 + ff_chunk])
        act = (gate * _sigmoid(gate) * up).astype(BF16)
        acc = acc + _dot(act, wd_ref[c0:c0 + ff_chunk, :])
    y = x + 0.5 * acc
    if final_norm:
        y = _rms(y, fg_ref[...])
    o_ref[...] = y


def _ffn(x, norm_g, wg, wu, wd, final_g, layer, *, tm, final_norm=False):
    m, d = x.shape
    d_ff = wg.shape[2]
    ff_chunk = d_ff // 2 if (d_ff // 2) % 128 == 0 else d_ff
    body = functools.partial(_ffn_body, ff_chunk=ff_chunk, final_norm=final_norm)
    return pl.pallas_call(
        body,
        grid=(m // tm,),
        in_specs=[
            pl.BlockSpec((tm, d), lambda i: (i, 0)),
            _const_spec((None, 1, d), lambda i: (layer, 0, 0)),
            _const_spec((None, d, d_ff), lambda i: (layer, 0, 0)),
            _const_spec((None, d, d_ff), lambda i: (layer, 0, 0)),
            _const_spec((None, d_ff, d), lambda i: (layer, 0, 0)),
            _const_spec((1, d), lambda i: (0, 0)),
        ],
        out_specs=pl.BlockSpec((tm, d), lambda i: (i, 0)),
        out_shape=jax.ShapeDtypeStruct((m, d), F32),
        compiler_params=_cparams("parallel"),
        name="ffn",
    )(x, norm_g, wg, wu, wd, final_g)


def _small_fn(z, is_logf, is_beta, neg_exp_a):
    return jnp.where(is_logf, -_softplus(-z), jnp.where(is_beta, _sigmoid(z), neg_exp_a * _softplus(z)))


def _inproj_body(x_ref, g_ref, wfox_ref, wrw_ref, wgd_ref, wgate_ref, wsm_ref, wkvt_ref, pc_ref,
                 qkvb_ref, fk_ref, fv_ref, kt_ref, vt_ref, zr_ref, gqkv_ref, gz_ref, zg_ref, sm_ref):
    h = _rms(x_ref[...], g_ref[...]).astype(BF16)
    fox = _dot(h, wfox_ref[...])
    qkvb_ref[...] = fox.astype(BF16)
    fk_ref[...] = fox[:, FOX_W:2 * FOX_W]
    fv_ref[...] = fox[:, 2 * FOX_W:]
    kvt = _dot_nt(wkvt_ref[...], h)
    kt_ref[...] = kvt[:FOX_W]
    vt_ref[...] = kvt[FOX_W:]
    zr_ref[...] = _dot(h, wrw_ref[...])
    gd = _dot(h, wgd_ref[...])
    gqkv_ref[...] = gd[:, :GD_QKV]
    gz_ref[...] = gd[:, GD_QKV:]
    zg_ref[...] = _dot(h, wgate_ref[...])
    pc = pc_ref[...]
    col = lax.broadcasted_iota(jnp.int32, (1, N_SMALL), 1)
    sm = _dot(h, wsm_ref[...]) + pc[0:1]
    sm_ref[...] = _small_fn(sm, col < 8, col < 12, -jnp.exp(pc[1:2]))


def _inproj(x, norm_g, w, layer, *, tm, seq):
    m, d = x.shape
    n_gate = w["gate"].shape[2]
    n_gd = w["gd"].shape[2]
    t_batch, t_seq = (m // seq, seq) if seq % tm == 0 else (1, m)
    tiles_per_seq = t_seq // tm
    row_blk = lambda n: pl.BlockSpec((tm, n), lambda i: (i, 0))
    t_blk = pl.BlockSpec((None, FOX_W, tm), lambda i: (i // tiles_per_seq, 0, i % tiles_per_seq))
    wspec = lambda a: _const_spec((None,) + a.shape[1:], lambda i: (layer, 0, 0))
    out_shape = (
        jax.ShapeDtypeStruct((m, 3 * FOX_W), BF16),
        jax.ShapeDtypeStruct((m, FOX_W), F32),
        jax.ShapeDtypeStruct((m, FOX_W), F32),
        jax.ShapeDtypeStruct((t_batch, FOX_W, t_seq), F32),
        jax.ShapeDtypeStruct((t_batch, FOX_W, t_seq), F32),
        jax.ShapeDtypeStruct((m, RW_COLS), F32),
        jax.ShapeDtypeStruct((m, GD_QKV), F32),
        jax.ShapeDtypeStruct((m, n_gd - GD_QKV), F32),
        jax.ShapeDtypeStruct((m, n_gate), F32),
        jax.ShapeDtypeStruct((m, N_SMALL), F32),
    )
    out_specs = (
        row_blk(3 * FOX_W), row_blk(FOX_W), row_blk(FOX_W), t_blk, t_blk, row_blk(RW_COLS), row_blk(GD_QKV),
        row_blk(n_gd - GD_QKV), row_blk(n_gate), row_blk(N_SMALL),
    )
    return pl.pallas_call(
        _inproj_body,
        grid=(m // tm,),
        in_specs=[
            pl.BlockSpec((tm, d), lambda i: (i, 0)),
            _const_spec((None, 1, d), lambda i: (layer, 0, 0)),
            wspec(w["fox"]), wspec(w["rw"]), wspec(w["gd"]), wspec(w["gate"]),
            wspec(w["small"]), wspec(w["kv_t"]), wspec(w["small_pc"]),
        ],
        out_specs=out_specs,
        out_shape=out_shape,
        compiler_params=_cparams("parallel"),
        name="inproj",
    )(x, norm_g, w["fox"], w["rw"], w["gd"], w["gate"], w["small"], w["kv_t"], w["small_pc"])


FOX_AW = 2 * HEAD_DIM
N_PIECES = 3


def _eye(n):
    return (lax.broadcasted_iota(jnp.int32, (n, n), 0) == lax.broadcasted_iota(jnp.int32, (n, n), 1)).astype(BF16)


def _fox_prompt_body(q_ref, k_ref, vt_in_ref, lf_ref, pk_ref, pc_ref, one_ref, o_ref,
                     kaug_ref, vt_ref, qaug_ref, m_ref, l_ref, acc_ref, *, tq, cb):
    i = pl.program_id(1)
    seq = k_ref.shape[0]

    @pl.when(i == 0)
    def _():
        lower = _tri(cb, "incl").astype(BF16)
        carry = jnp.zeros((1, N_SMALL), F32)
        for b0 in range(0, seq, cb):
            cc = _dot_sel_l(lower, lf_ref[b0:b0 + cb, :]) + carry
            carry = cc[cb - 1:cb, :]
            pieces = jnp.concatenate(_split3(-cc), axis=1)
            kaug_ref[b0:b0 + cb, :] = (_dot(k_ref[b0:b0 + cb, :], pk_ref[...])
                                       + _dot(pieces, pc_ref[...])).astype(BF16)
            vt_ref[:, b0:b0 + cb] = vt_in_ref[:, b0:b0 + cb].astype(BF16)

    qaug_ref[...] = (_dot(q_ref[...], pk_ref[...]) + one_ref[...]).astype(BF16)
    m_ref[...] = jnp.full(m_ref.shape, -jnp.inf, F32)
    l_ref[...] = jnp.zeros(l_ref.shape, F32)
    acc_ref[...] = jnp.zeros(acc_ref.shape, F32)

    def tile(j, diagonal):
        k0 = pl.multiple_of(j * tq, tq)
        if diagonal:
            visible = _tri(tq, "upper_incl")
        scores = [_dot_nt(kaug_ref[pl.ds(k0, tq), h * FOX_AW:(h + 1) * FOX_AW],
                          qaug_ref[:, h * FOX_AW:(h + 1) * FOX_AW]) for h in range(FOX_HEADS)]
        for h in range(FOX_HEADS):
            hs = slice(h * HEAD_DIM, (h + 1) * HEAD_DIM)
            st = scores[h]
            if diagonal:
                st = jnp.where(visible, st, -jnp.inf)
            m_old = m_ref[h]
            m_new = jnp.maximum(m_old, jnp.max(st, axis=0, keepdims=True))
            alpha = jnp.exp(m_old - m_new)
            p = jnp.exp(st - m_new)
            m_ref[h] = m_new
            l_ref[h] = alpha * l_ref[h] + jnp.sum(p, axis=0, keepdims=True)
            acc_ref[hs, :] = alpha * acc_ref[hs, :] + _dot(vt_ref[hs, pl.ds(k0, tq)], p.astype(BF16))

    def body(j, carry):
        tile(j, False)
        return carry

    lax.fori_loop(0, i, body, 0)
    tile(i, True)
    out_t = jnp.concatenate(
        [acc_ref[h * HEAD_DIM:(h + 1) * HEAD_DIM, :] / l_ref[h] for h in range(FOX_HEADS)], axis=0)
    o_ref[...] = _dot_nt(_eye(tq), out_t.astype(BF16)).astype(BF16)


def _fox_prompt(qkvb, vt, sm, consts, batch, seq, *, tq):
    m = batch * seq
    nq = seq // tq
    cb = min(512, seq)
    body = functools.partial(_fox_prompt_body, tq=tq, cb=cb)
    aug_w = FOX_HEADS * FOX_AW
    return pl.pallas_call(
        body,
        grid=(batch, nq),
        in_specs=[
            pl.BlockSpec((tq, FOX_W), lambda b, i: (b * nq + i, 0)),
            pl.BlockSpec((seq, FOX_W), lambda b, i: (b, 1)),
            pl.BlockSpec((None, FOX_W, seq), lambda b, i: (b, 0, 0)),
            pl.BlockSpec((seq, N_SMALL), lambda b, i: (b, 0)),
            _const_spec((FOX_W, aug_w), lambda b, i: (0, 0)),
            _const_spec((N_PIECES * N_SMALL, aug_w), lambda b, i: (0, 0)),
            _const_spec((1, aug_w), lambda b, i: (0, 0)),
        ],
        out_specs=pl.BlockSpec((tq, FOX_W), lambda b, i: (b * nq + i, 0)),
        out_shape=jax.ShapeDtypeStruct((m, FOX_W), BF16),
        scratch_shapes=[pltpu.VMEM((seq, aug_w), BF16), pltpu.VMEM((FOX_W, seq), BF16),
                        pltpu.VMEM((tq, aug_w), BF16),
                        pltpu.VMEM((FOX_HEADS, 1, tq), F32), pltpu.VMEM((FOX_HEADS, 1, tq), F32),
                        pltpu.VMEM((FOX_W, tq), F32)],
        compiler_params=_cparams("parallel", "arbitrary"),
        name="fox_prompt",
    )(qkvb, qkvb, vt, sm, consts["place_k"], consts["place_c"], consts["ones"])


def _fox_prompt_consts():
    aug_w = FOX_HEADS * FOX_AW
    lane = jnp.arange(aug_w)
    src = jnp.arange(FOX_W)
    place_k = (lane[None, :] == (src // HEAD_DIM * FOX_AW + src % HEAD_DIM)[:, None]).astype(BF16)
    col = jnp.arange(N_PIECES * N_SMALL)
    tgt = jnp.where(col % N_SMALL < FOX_HEADS, (col % N_SMALL) * FOX_AW + HEAD_DIM + col // N_SMALL, -1)
    place_c = (lane[None, :] == tgt[:, None]).astype(BF16)
    ones = ((lane % FOX_AW >= HEAD_DIM) & (lane % FOX_AW < HEAD_DIM + N_PIECES)).astype(F32)[None, :]
    return {"place_k": place_k, "place_c": place_c, "ones": ones}


def _fox_sample_body(pt_ref, q_ref, kn_ref, vn_ref, lfn_ref, *rest, t_valid, pps):
    kp_refs = rest[0:pps]
    vp_refs = rest[pps:2 * pps]
    lf_refs = rest[2 * pps:3 * pps]
    o_ref, qbd_ref, m_ref, l_ref, acc_ref, tail_ref = rest[3 * pps:]
    s_idx = pl.program_id(1)
    tp = SUBLANES
    nrow = t_valid * FOX_HEADS
    page = kp_refs[0].shape[1]
    diag = (lax.broadcasted_iota(jnp.int32, (nrow, FOX_W), 0) % FOX_HEADS
            == lax.broadcasted_iota(jnp.int32, (nrow, FOX_W), 1) // HEAD_DIM)

    def expand_rows(x):
        return jnp.concatenate([x] * t_valid, axis=0)

    @pl.when(s_idx == 0)
    def _():
        qf = q_ref[...].astype(F32)
        q_rows = jnp.concatenate([jnp.broadcast_to(qf[t:t + 1], (FOX_HEADS, FOX_W)) for t in range(t_valid)],
                                 axis=0)
        qbd = jnp.where(diag, q_rows, 0.0).astype(BF16)
        qbd_ref[...] = qbd
        upper = _tri(tp, "upper_incl").astype(BF16)
        lf_hi, lf_mid, lf_lo = _split3(lfn_ref[...])
        cn = (_dot_tn(lf_hi, upper) + _dot_tn(lf_mid, upper) + _dot_tn(lf_lo, upper))[:FOX_HEADS]
        s = _dot_nt(qbd, kn_ref[...]) - expand_rows(cn)
        t_q = lax.broadcasted_iota(jnp.int32, (nrow, tp), 0) // FOX_HEADS
        u_k = lax.broadcasted_iota(jnp.int32, (nrow, tp), 1)
        s = jnp.where((u_k <= t_q) & (u_k < t_valid), s, -jnp.inf)
        m = jnp.max(s, axis=-1, keepdims=True)
        p = jnp.exp(s - m)
        m_ref[...] = m
        l_ref[...] = jnp.sum(p, axis=-1, keepdims=True)
        acc_ref[...] = _dot(p.astype(BF16), vn_ref[...])
        tail_ref[...] = jnp.zeros_like(tail_ref)

    @pl.when(s_idx > 0)
    def _():
        later3 = jnp.concatenate([_tri(page, "strict").astype(BF16)] * 3, axis=0)
        qbd = qbd_ref[...]
        tail = tail_ref[...]
        scores = [None] * pps
        for r in reversed(range(pps)):
            lf = lf_refs[r][...]
            suf = _dot(jnp.concatenate(_split3(lf), axis=1), later3) + tail
            tail = tail + jnp.sum(lf, axis=-1, keepdims=True)
            scores[r] = _dot(qbd, kp_refs[r][...].astype(BF16)) + expand_rows(suf)
        tail_ref[...] = tail
        m_old = m_ref[...]
        m_new = m_old
        for r in range(pps):
            m_new = jnp.maximum(m_new, jnp.max(scores[r], axis=-1, keepdims=True))
        alpha = jnp.exp(m_old - m_new)
        l_new = alpha * l_ref[...]
        acc = alpha * acc_ref[...]
        for r in range(pps):
            p = jnp.exp(scores[r] - m_new)
            l_new = l_new + jnp.sum(p, axis=-1, keepdims=True)
            acc = acc + _dot_nt(p.astype(BF16), vp_refs[r][...].astype(BF16))
        m_ref[...] = m_new
        l_ref[...] = l_new
        acc_ref[...] = acc

    @pl.when(s_idx == pl.num_programs(1) - 1)
    def _():
        o = jnp.where(diag, acc_ref[...] / l_ref[...], 0.0).astype(BF16)
        pick = (lax.broadcasted_iota(jnp.int32, (tp, nrow), 1) // FOX_HEADS
                == lax.broadcasted_iota(jnp.int32, (tp, nrow), 0)).astype(BF16)
        o_ref[...] = _dot(pick, o).astype(BF16)


def _fox_sample(qkvb, sm, cache_kt, cache_vt, cache_lft, page_table, layer, *, t_valid, pps):
    tp = SUBLANES
    bd, n_pages = page_table.shape
    page = cache_kt.shape[3]
    nrow = t_valid * FOX_HEADS
    assert n_pages % pps == 0

    def page_spec(r, blk):
        def index_map(b, s, pt):
            return (layer, pt[b, n_pages - jnp.maximum(s, 1) * pps + r]) + (0,) * (len(blk) - 2)
        return pl.BlockSpec(blk, index_map)

    kv_blk = (None, None, FOX_W, page)
    lf_blk = (None, None, FOX_HEADS, page)
    grid_spec = pltpu.PrefetchScalarGridSpec(
        num_scalar_prefetch=1,
        grid=(bd, n_pages // pps + 1),
        in_specs=[
            pl.BlockSpec((tp, FOX_W), lambda b, s, pt: (b, 0)),
            pl.BlockSpec((tp, FOX_W), lambda b, s, pt: (b, 1)),
            pl.BlockSpec((tp, FOX_W), lambda b, s, pt: (b, 2)),
            pl.BlockSpec((tp, N_SMALL), lambda b, s, pt: (b, 0)),
            *[page_spec(r, kv_blk) for r in range(pps)],
            *[page_spec(r, kv_blk) for r in range(pps)],
            *[page_spec(r, lf_blk) for r in range(pps)],
        ],
        out_specs=pl.BlockSpec((tp, FOX_W), lambda b, s, pt: (b, 0)),
        scratch_shapes=[
            pltpu.VMEM((nrow, FOX_W), BF16),
            pltpu.VMEM((nrow, 1), F32),
            pltpu.VMEM((nrow, 1), F32),
            pltpu.VMEM((nrow, FOX_W), F32),
            pltpu.VMEM((FOX_HEADS, 1), F32),
        ],
    )
    body = functools.partial(_fox_sample_body, t_valid=t_valid, pps=pps)
    return pl.pallas_call(
        body,
        grid_spec=grid_spec,
        out_shape=jax.ShapeDtypeStruct((bd * tp, FOX_W), BF16),
        compiler_params=_cparams("parallel", "arbitrary"),
        name="fox_sample",
    )(page_table, qkvb, qkvb, qkvb, sm, *([cache_kt] * pps), *([cache_vt] * pps), *([cache_lft] * pps))


def _rwkv_body(z_ref, shift_ref, s0_ref, mu_ref, pv_ref, wup_ref, aup_ref, gup_ref, seg_ref,
               o_ref, so_ref, sbd_ref, prev_ref, *, chunk, n_chunks, t_valid):
    c = pl.program_id(1)
    tb = chunk * n_chunks
    rows_x = GROUP * chunk
    lane_mask = _head_mask(rows_x, chunk, RW_GW, HEAD_DIM)
    amask = _head_mask(rows_x, chunk, rows_x, chunk)
    state_mask = _head_mask(RW_GW, HEAD_DIM, RW_GW, HEAD_DIM)

    @pl.when(c == 0)
    def _():
        prev_ref[0:1, :] = shift_ref[...]
        for g in range(RW_GROUPS):
            blocks = [jnp.concatenate([s0_ref[g * GROUP + h]] * GROUP, axis=1) for h in range(GROUP)]
            sbd_ref[g] = jnp.where(state_mask, jnp.concatenate(blocks, axis=0), 0.0)

    z = z_ref[...]
    row = lax.broadcasted_iota(jnp.int32, (tb, 1), 0)
    z_prev = jnp.where(row == 0, prev_ref[0:1, :], pltpu.roll(z, 1, axis=0))
    prev_ref[0:1, :] = z[tb - 1:tb, :]
    zs = z + (z_prev - z) * mu_ref[...]
    r = zs[:, 0:RW_W]
    k = zs[:, RW_W:2 * RW_W]
    v = zs[:, 2 * RW_W:3 * RW_W]
    o_l = 3 * RW_W
    wd = zs[:, o_l:o_l + RW_LORA_W]
    ad = zs[:, o_l + RW_LORA_W:o_l + RW_LORA_W + RW_LORA_A]
    gd = zs[:, o_l + RW_LORA_W + RW_LORA_A:]
    pv = pv_ref[...]
    w_log = -_softplus(-(pv[0:1] + _dot(jnp.tanh(wd).astype(BF16), wup_ref[...]))) - 0.5
    lw = -jnp.exp(w_log)
    a = _sigmoid(pv[1:2] + _dot(ad.astype(BF16), aup_ref[...]))
    g_out = _dot(_sigmoid(gd).astype(BF16), gup_ref[...])
    kk = k * pv[2:3]
    k = k * (1.0 + (a - 1.0) * pv[3:4])
    seg2 = seg_ref[...]
    kk = kk * lax.rsqrt(_seg_sum(kk * kk, seg2) + L2_EPS)
    if t_valid < chunk:
        valid = row < t_valid
        lw = jnp.where(valid, lw, 0.0)
        kk = jnp.where(valid, kk, 0.0)
        k = jnp.where(valid, k, 0.0)
    cs = _dot_sel_l(_block_tri(tb, chunk, strict=False).astype(BF16), lw)
    p_inv = jnp.exp(-cs)
    rp = (r * jnp.exp(cs)).astype(BF16)
    kkp = (kk * jnp.exp(cs - lw)).astype(BF16)
    bn = kk * a * p_inv
    kn = k * p_inv
    vb = v.astype(BF16)
    t_idx = lax.broadcasted_iota(jnp.int32, (chunk, 2 * rows_x), 0)
    j_idx = lax.broadcasted_iota(jnp.int32, (chunk, 2 * rows_x), 1) % chunk
    strict = (t_idx > j_idx)[:, :rows_x]
    incl2 = t_idx >= j_idx
    groups = range(RW_GROUPS)
    gsl = [slice(g * RW_GW, (g + 1) * RW_GW) for g in groups]
    bnb = bn.astype(BF16)
    knb = kn.astype(BF16)

    def prepare(ci, g):
        def gen(done):
            rs = slice(ci * chunk, (ci + 1) * chunk)
            bk_x = jnp.concatenate([_expand(bnb[rs, gsl[g]], lane_mask), _expand(knb[rs, gsl[g]], lane_mask)],
                                   axis=0)
            a_all = _dot_nt(jnp.concatenate([kkp[rs, gsl[g]], rp[rs, gsl[g]]], axis=0), bk_x)
            yield
            a_ak = jnp.where(strict, a_all[:chunk, rows_x:], 0.0).astype(BF16)
            a_r = jnp.where(incl2, a_all[chunk:, :], 0.0).astype(BF16)
            t_m1 = yield from _unit_lower_inverse_minus_eye(
                jnp.where(strict, a_all[:chunk, :rows_x], 0.0), chunk, amask)
            return a_ak, a_r, t_m1.astype(BF16)
        return gen

    def advance(ci):
        def gen(done):
            rs = slice(ci * chunk, (ci + 1) * chunk)
            pre = [done["prepare", ci, g] for g in groups]
            p_end = jnp.exp(cs[(ci + 1) * chunk - 1:(ci + 1) * chunk, :])
            bk_e = jnp.concatenate([(bn[rs] * p_end).astype(BF16), (kn[rs] * p_end).astype(BF16)], axis=0)
            v_xs = [_expand(vb[rs, gsl[g]], lane_mask) for g in groups]
            sts = [sbd_ref[g] for g in groups]
            stbs = [st.astype(BF16) for st in sts]
            rhs = [_dot_nt(kkp[rs, gsl[g]], stbs[g]) + _dot(pre[g][0], v_xs[g]) for g in groups]
            yield
            sabs = [(-(rhs[g] + _dot(pre[g][2], _expand(rhs[g].astype(BF16), lane_mask)))).astype(BF16)
                    for g in groups]
            yield
            y_c = jnp.concatenate(
                [_dot_nt(rp[rs, gsl[g]], stbs[g])
                 + _dot(pre[g][1], jnp.concatenate([_expand(sabs[g], lane_mask), v_xs[g]], axis=0))
                 for g in groups], axis=1)
            yield
            for g in groups:
                outer = _dot_tn(jnp.concatenate([sabs[g], vb[rs, gsl[g]]], axis=0), bk_e[:, gsl[g]])
                sbd_ref[g] = sts[g] * p_end[:, gsl[g]] + jnp.where(state_mask, outer, 0.0)
            yield
            return y_c
        return gen

    tasks = []
    for ci in range(n_chunks):
        tasks.append((("advance", ci),
                      [("prepare", ci, g) for g in groups] + ([("advance", ci - 1)] if ci else []), [],
                      advance(ci)))
        for g in groups:
            tasks.append((("prepare", ci, g), [], [], prepare(ci, g)))
    done = _run_tasks(tasks)
    y_rows = [done["advance", ci] for ci in range(n_chunks)]
    y = y_rows[0] if n_chunks == 1 else jnp.concatenate(y_rows, axis=0)
    inv_n = 1.0 / HEAD_DIM
    mean = _seg_sum(y, seg2) * inv_n
    yc = y - mean
    var = _seg_sum(yc * yc, seg2) * inv_n
    yn = yc * lax.rsqrt(var + RW_GN_EPS) * pv[5:6] + pv[6:7]
    bonus = _seg_sum(r * k * pv[4:5], seg2) * v
    o_ref[...] = ((yn + bonus) * g_out).astype(BF16)

    @pl.when(c == pl.num_programs(1) - 1)
    def _():
        for g in range(RW_GROUPS):
            st = sbd_ref[g]
            for h in range(GROUP):
                hs = slice(h * HEAD_DIM, (h + 1) * HEAD_DIM)
                so_ref[g * GROUP + h] = st[hs, hs]


def _rwkv(zr, shift0, s0, w, layer, batch, seq, *, chunk, n_chunks, t_valid):
    m = batch * seq
    tb = chunk * n_chunks
    nc = seq // tb
    body = functools.partial(_rwkv_body, chunk=chunk, n_chunks=n_chunks, t_valid=t_valid)
    lspec = lambda a: _const_spec((None,) + a.shape[1:], lambda b, c: (layer, 0, 0))
    state_spec = pl.BlockSpec((None, RW_HEADS, HEAD_DIM, HEAD_DIM), lambda b, c: (b, 0, 0, 0))
    s0_layer = layer if s0.shape[0] > 1 else 0
    return pl.pallas_call(
        body,
        grid=(batch, nc),
        in_specs=[
            pl.BlockSpec((tb, RW_COLS), lambda b, c: (b * nc + c, 0)),
            pl.BlockSpec((None, 1, RW_COLS), lambda b, c: (b, 0, 0)),
            pl.BlockSpec((None, None, RW_HEADS, HEAD_DIM, HEAD_DIM), lambda b, c: (s0_layer, b, 0, 0, 0)),
            lspec(w["mu"]), lspec(w["pv"]), lspec(w["w_up"]), lspec(w["a_up"]), lspec(w["g_up"]),
            _const_spec((2 * RW_W, RW_W), lambda b, c: (0, 0)),
        ],
        out_specs=(pl.BlockSpec((tb, RW_W), lambda b, c: (b * nc + c, 0)), state_spec),
        out_shape=(
            jax.ShapeDtypeStruct((m, RW_W), BF16),
            jax.ShapeDtypeStruct((batch, RW_HEADS, HEAD_DIM, HEAD_DIM), F32),
        ),
        scratch_shapes=[pltpu.VMEM((RW_GROUPS, RW_GW, RW_GW), F32), pltpu.VMEM((SUBLANES, RW_COLS), F32)],
        compiler_params=_cparams("parallel", "arbitrary"),
        name="rwkv",
    )(zr, shift0, s0, w["mu"], w["pv"], w["w_up"], w["a_up"], w["g_up"], w["seg2"])


def _gdn_body(x_ref, gz_ref, sm_ref, conv0_ref, s0_ref, cw_ref, ng_ref, seg_ref,
              o_ref, so_ref, prev_ref, *, chunk, n_chunks, t_valid):
    c = pl.program_id(1)
    tb = chunk * n_chunks
    rows_x = GD_HEADS * chunk
    lane_mask = _head_mask(rows_x, chunk, GD_W, GD_DK)
    amask = _head_mask(rows_x, chunk, rows_x, chunk)

    @pl.when(c == 0)
    def _():
        prev_ref[...] = conv0_ref[...]
        so_ref[...] = s0_ref[...]

    x = x_ref[...]
    prev = prev_ref[...]
    prev_ref[...] = x[tb - SUBLANES:, :]
    cw = cw_ref[...]
    row8 = lax.broadcasted_iota(jnp.int32, (SUBLANES, 1), 0)
    conv = x * cw[GD_CONV - 1:GD_CONV]
    for sft in range(1, GD_CONV):
        xr = pltpu.roll(x, sft, axis=0)
        top = jnp.where(row8 < sft, pltpu.roll(prev, sft, axis=0), xr[:SUBLANES])
        xs = top if tb == SUBLANES else jnp.concatenate([top, xr[SUBLANES:]], axis=0)
        conv = conv + xs * cw[GD_CONV - 1 - sft:GD_CONV - sft]
    act = conv * _sigmoid(conv)
    seg2 = seg_ref[...]
    q = act[:, :GD_W]
    k = act[:, GD_W:2 * GD_W]
    v = act[:, 2 * GD_W:]
    q = q * lax.rsqrt(_seg_sum(q * q, seg2) + L2_EPS) * (GD_DK ** -0.5)
    k = k * lax.rsqrt(_seg_sum(k * k, seg2) + L2_EPS)
    sm = sm_ref[...]
    beta = sm[:, 8:8 + GD_HEADS]
    g = sm[:, 12:12 + GD_HEADS]
    if t_valid < chunk:
        rowc = lax.broadcasted_iota(jnp.int32, (tb, 1), 0)
        beta = jnp.where(rowc < t_valid, beta, 0.0)
        g = jnp.where(rowc < t_valid, g, 0.0)
    gc = _dot_sel_l(_block_tri(tb, chunk, strict=False).astype(BF16), g)
    head_lanes = _head_mask(GD_HEADS, 1, GD_W, GD_DK).astype(BF16)
    beta_l = _dot_sel_r(beta, head_lanes)
    gc_l = _dot_sel_r(gc, head_lanes)
    egc_l = jnp.exp(gc_l)
    kb = k * beta_l
    vbeta = (v * beta_l).astype(BF16)
    kbg = (kb * egc_l).astype(BF16)
    qg = (q * egc_l).astype(BF16)
    kb = kb.astype(BF16)
    qb = q.astype(BF16)
    kbf = k.astype(BF16)
    t_idx = lax.broadcasted_iota(jnp.int32, (chunk, rows_x), 0)
    s_idx = lax.broadcasted_iota(jnp.int32, (chunk, rows_x), 1) % chunk
    strict = t_idx > s_idx
    incl = t_idx >= s_idx
    diag = t_idx == s_idx
    head_cols = _head_mask(GD_HEADS, 1, rows_x, chunk).astype(BF16)
    ones_cc = jnp.ones((chunk, chunk), BF16)
    hsl = [slice(h * GD_DK, (h + 1) * GD_DK) for h in range(GD_HEADS)]

    def prepare(ci):
        def gen(done):
            rs = slice(ci * chunk, (ci + 1) * chunk)
            g_t = _dot_sel_r(gc[rs], head_cols)
            yield
            g_s = _dot_sel_l(ones_cc, jnp.where(diag, g_t, 0.0))
            qk = _dot_nt(jnp.concatenate([kb[rs], qb[rs]], axis=0), _expand(kbf[rs], lane_mask))
            yield
            decay = jnp.exp(jnp.minimum(g_t - g_s, 0.0))
            amat = jnp.where(incl, qk[chunk:] * decay, 0.0).astype(BF16)
            t_m1 = yield from _unit_lower_inverse_minus_eye(
                jnp.where(strict, qk[:chunk] * decay, 0.0), chunk, amask)
            vk = jnp.concatenate([vbeta[rs], kbg[rs]], axis=1)
            vk_x = jnp.concatenate([_expand(vbeta[rs], lane_mask), _expand(kbg[rs], lane_mask)], axis=1)
            uw = vk.astype(F32) + _dot(t_m1.astype(BF16), vk_x)
            yield
            return amat, uw
        return gen

    def advance(ci):
        def gen(done):
            rs = slice(ci * chunk, (ci + 1) * chunk)
            amat, uw = done["prepare", ci]
            glast_l = gc_l[(ci + 1) * chunk - 1:(ci + 1) * chunk]
            kg = (k[rs] * jnp.exp(glast_l - gc_l[rs])).astype(BF16)
            wb = uw[:, GD_W:].astype(BF16)
            sts = [so_ref[h] for h in range(GD_HEADS)]
            stbs = [st.astype(BF16) for st in sts]
            v_news = [uw[:, hs] - _dot(wb[:, hs], stb) for hs, stb in zip(hsl, stbs)]
            yield
            vnb = jnp.concatenate(v_news, axis=1).astype(BF16)
            for hs, st, h in zip(hsl, sts, range(GD_HEADS)):
                so_ref[h] = st * jnp.exp(glast_l[:, hs]) + _dot_tn(kg[:, hs], vnb[:, hs])
            yield
            o_c = (jnp.concatenate([_dot(qg[rs, hs], stb) for hs, stb in zip(hsl, stbs)], axis=1)
                   + _dot(amat, _expand(vnb, lane_mask)))
            yield
            return o_c
        return gen

    tasks = []
    for ci in range(n_chunks):
        tasks.append((("advance", ci), [("prepare", ci)] + ([("advance", ci - 1)] if ci else []), [],
                      advance(ci)))
        tasks.append((("prepare", ci), [], [], prepare(ci)))
    done = _run_tasks(tasks)
    o_rows = [done["advance", ci] for ci in range(n_chunks)]
    o = o_rows[0] if n_chunks == 1 else jnp.concatenate(o_rows, axis=0)
    o = o * lax.rsqrt(_seg_sum(o * o, seg2) * (1.0 / GD_DV) + EPS) * ng_ref[...]
    gz = gz_ref[...]
    o_ref[...] = (o * (gz * _sigmoid(gz))).astype(BF16)


def _gdn(gqkv, gz, sm, conv0, s0, w, layer, batch, seq, *, chunk, n_chunks, t_valid):
    m = batch * seq
    tb = chunk * n_chunks
    nc = seq // tb
    body = functools.partial(_gdn_body, chunk=chunk, n_chunks=n_chunks, t_valid=t_valid)
    lspec = lambda a: _const_spec((None,) + a.shape[1:], lambda b, c: (layer, 0, 0))
    state_spec = pl.BlockSpec((None, GD_HEADS, GD_DK, GD_DV), lambda b, c: (b, 0, 0, 0))
    s0_layer = layer if s0.shape[0] > 1 else 0
    return pl.pallas_call(
        body,
        grid=(batch, nc),
        in_specs=[
            pl.BlockSpec((tb, GD_QKV), lambda b, c: (b * nc + c, 0)),
            pl.BlockSpec((tb, GD_HEADS * GD_DV), lambda b, c: (b * nc + c, 0)),
            pl.BlockSpec((tb, N_SMALL), lambda b, c: (b * nc + c, 0)),
            pl.BlockSpec((None, SUBLANES, GD_QKV), lambda b, c: (b, 0, 0)),
            pl.BlockSpec((None, None, GD_HEADS, GD_DK, GD_DV), lambda b, c: (s0_layer, b, 0, 0, 0)),
            lspec(w["conv"]), lspec(w["norm_g"]),
            _const_spec((2 * GD_W, GD_W), lambda b, c: (0, 0)),
        ],
        out_specs=(pl.BlockSpec((tb, GD_HEADS * GD_DV), lambda b, c: (b * nc + c, 0)), state_spec),
        out_shape=(
            jax.ShapeDtypeStruct((m, GD_HEADS * GD_DV), BF16),
            jax.ShapeDtypeStruct((batch, GD_HEADS, GD_DK, GD_DV), F32),
        ),
        scratch_shapes=[pltpu.VMEM((SUBLANES, GD_QKV), F32)],
        compiler_params=_cparams("parallel", "arbitrary"),
        name="gdn",
    )(gqkv, gz, sm, conv0, s0, w["conv"], w["norm_g"], w["seg2"])


def _merge_body(x_ref, of_ref, or_ref, og_ref, zg_ref, wf_ref, wr_ref, wg_ref, wo_ref, o_ref):
    d = x_ref.shape[1]
    merged = _sigmoid(zg_ref[:, 0:d]) * _dot(of_ref[...], wf_ref[...])
    merged = merged + _sigmoid(zg_ref[:, d:2 * d]) * _dot(or_ref[...], wr_ref[...])
    merged = merged + _sigmoid(zg_ref[:, 2 * d:3 * d]) * _dot(og_ref[...], wg_ref[...])
    o_ref[...] = x_ref[...] + _dot(merged.astype(BF16), wo_ref[...])


def _merge(x, o_fox, o_rw, o_gd, zg, w, layer, *, tm):
    m, d = x.shape
    row_blk = lambda n: pl.BlockSpec((tm, n), lambda i: (i, 0))
    wspec = lambda a: _const_spec((None,) + a.shape[1:], lambda i: (layer, 0, 0))
    return pl.pallas_call(
        _merge_body,
        grid=(m // tm,),
        in_specs=[row_blk(d), row_blk(FOX_W), row_blk(RW_W), row_blk(GD_HEADS * GD_DV),
                  row_blk(N_BRANCH * d),
                  wspec(w["br_fox"]), wspec(w["br_rw"]), wspec(w["br_gd"]), wspec(w["out"])],
        out_specs=row_blk(d),
        out_shape=jax.ShapeDtypeStruct((m, d), F32),
        compiler_params=_cparams("parallel"),
        name="merge",
    )(x, o_fox, o_rw, o_gd, zg, w["br_fox"], w["br_rw"], w["br_gd"], w["out"])


def _same_segment2(n, width):
    idx = jnp.arange(n) // width
    seg = (idx[:, None] == idx[None, :]).astype(BF16)
    return jnp.concatenate([seg, seg], axis=0)


def _row_tile(m):
    for tm in (512, 256, 128, 64, 32, 16, 8):
        if m % tm == 0:
            return tm
    raise ValueError(f"row count {m} is not a multiple of 8")


def _pages_per_step(n_pages):
    for pps in (16, 8, 4, 2, 1):
        if n_pages % pps == 0:
            return pps


def kernel(x_prompt, x_sample, cache_k, cache_v, cache_logf, state_rwkv_shift, state_rwkv,
           state_gdn_conv, state_gdn, page_table, norm_ffn1, ffn1_wg, ffn1_wu, ffn1_wd, norm_mix,
           w_in, fox_fb, rw_mu, rw_w0, rw_w_up, rw_a0, rw_a_up, rw_g_up, rw_kk, rw_ka, rw_rk,
           rw_ln_g, rw_ln_b, gd_conv, gd_a_log, gd_dt_bias, gd_norm_g, w_br_fox, w_br_rw, w_br_gd,
           w_out, norm_ffn2, ffn2_wg, ffn2_wu, ffn2_wd, final_norm):
    bp, seq, d = x_prompt.shape
    bd, t_s, _ = x_sample.shape
    depth = w_in.shape[0]
    tp = SUBLANES
    chunk_p = 64
    n_chunks_p = 4 if seq % (4 * chunk_p) == 0 else 1
    assert t_s <= tp and seq % chunk_p == 0

    o_rw_c = 3 * FOX_W + FOX_HEADS
    o_gd_c = o_rw_c + RW_COLS
    o_gz = o_gd_c + GD_QKV
    o_gb = o_gz + GD_HEADS * GD_DV
    o_ga = o_gb + GD_HEADS
    o_gate = o_ga + GD_HEADS
    scale = HEAD_DIM ** -0.5
    w_small = jnp.concatenate([w_in[:, :, 3 * FOX_W:o_rw_c], w_in[:, :, o_gb:o_gate]], axis=-1)
    zeros4 = jnp.zeros((depth, GD_HEADS), F32)
    small_bias = jnp.concatenate([fox_fb, zeros4, gd_dt_bias], axis=-1)
    small_alog = jnp.concatenate([jnp.zeros((depth, 8), F32), zeros4, gd_a_log], axis=-1)
    w_proj = {
        "fox": jnp.concatenate([w_in[:, :, :FOX_W] * scale, w_in[:, :, FOX_W:3 * FOX_W]], axis=-1).astype(BF16),
        "rw": w_in[:, :, o_rw_c:o_gd_c].astype(BF16),
        "gd": w_in[:, :, o_gd_c:o_gb].astype(BF16),
        "gate": w_in[:, :, o_gate:].astype(BF16),
        "small": w_small.astype(BF16),
        "kv_t": jnp.swapaxes(w_in[:, :, FOX_W:3 * FOX_W], 1, 2).astype(BF16),
        "small_pc": jnp.stack([small_bias, small_alog], axis=1),
    }
    w_rw = {
        "mu": rw_mu[:, None, :],
        "pv": jnp.stack([rw_w0, rw_a0, rw_kk, rw_ka, rw_rk.reshape(depth, RW_W), rw_ln_g, rw_ln_b,
                         jnp.zeros_like(rw_w0)], axis=1),
        "w_up": rw_w_up.astype(BF16), "a_up": rw_a_up.astype(BF16), "g_up": rw_g_up.astype(BF16),
        "seg2": _same_segment2(RW_W, HEAD_DIM),
    }
    w_gd = {
        "conv": jnp.pad(gd_conv, ((0, 0), (0, SUBLANES - GD_CONV), (0, 0))),
        "norm_g": jnp.tile(gd_norm_g, (1, GD_HEADS))[:, None, :],
        "seg2": _same_segment2(GD_W, GD_DK),
    }
    w_mg = {"br_fox": w_br_fox.astype(BF16), "br_rw": w_br_rw.astype(BF16),
            "br_gd": w_br_gd.astype(BF16), "out": w_out.astype(BF16)}
    ffn1 = (norm_ffn1[:, None, :], ffn1_wg.astype(BF16), ffn1_wu.astype(BF16), ffn1_wd.astype(BF16))
    ffn2 = (norm_ffn2[:, None, :], ffn2_wg.astype(BF16), ffn2_wu.astype(BF16), ffn2_wd.astype(BF16))
    norm_mix3 = norm_mix[:, None, :]
    final_g = final_norm[None, :]
    fox_consts = _fox_prompt_consts()

    n_pool, page = cache_k.shape[1], cache_k.shape[2]
    ckt = jnp.transpose(cache_k, (0, 1, 3, 4, 2)).reshape(depth, n_pool, FOX_W, page)
    cvt = jnp.transpose(cache_v, (0, 1, 3, 4, 2)).reshape(depth, n_pool, FOX_W, page)
    clft = jnp.swapaxes(cache_logf, 2, 3)
    pps = _pages_per_step(page_table.shape[1])
    gd_conv0_s = jnp.pad(state_gdn_conv, ((0, 0), (0, 0), (SUBLANES - (GD_CONV - 1), 0), (0, 0)))
    zeros_p = {
        "shift": jnp.zeros((bp, 1, RW_COLS), F32),
        "rw_s": jnp.zeros((1, bp, RW_HEADS, HEAD_DIM, HEAD_DIM), F32),
        "conv": jnp.zeros((bp, SUBLANES, GD_QKV), F32),
        "gd_s": jnp.zeros((1, bp, GD_HEADS, GD_DK, GD_DV), F32),
    }

    xp = x_prompt.reshape(bp * seq, d)
    xs = jnp.pad(x_sample, ((0, 0), (0, tp - t_s), (0, 0))).reshape(bd * tp, d)
    tm_p, tm_s = _row_tile(bp * seq), _row_tile(bd * tp)
    tm_in_p = min(tm_p, 256)
    tq = 512 if seq % 512 == 0 else min(256, seq)

    def layer_fn(x, l, *, batch, t_len, tm, tm_in, fox_fn, shift0, rw_s0, conv0, gd_s0, chunk, n_chunks,
                 t_valid, last):
        x = _ffn(x, *ffn1, final_g, l, tm=tm)
        qkvb, fk, fv, kt, vt, zr, gqkv, gz, zg, sm = _inproj(x, norm_mix3, w_proj, l, tm=tm_in, seq=t_len)
        o_fox = fox_fn(qkvb, vt, sm, l)
        o_rw, rw_st = _rwkv(zr, shift0, rw_s0, w_rw, l, batch, t_len, chunk=chunk, n_chunks=n_chunks,
                            t_valid=t_valid)
        o_gd, gd_s = _gdn(gqkv, gz, sm, conv0, gd_s0, w_gd, l, batch, t_len, chunk=chunk, n_chunks=n_chunks,
                          t_valid=t_valid)
        x = _merge(x, o_fox, o_rw, o_gd, zg, w_mg, l, tm=tm)
        x = _ffn(x, *ffn2, final_g, l, tm=tm, final_norm=last)
        return x, (fk, fv, kt, vt, sm, zr, rw_st, gqkv, gd_s)

    p_states, s_states = [], []
    for l in range(depth):
        last = l == depth - 1
        xp, st = layer_fn(
            xp, l, batch=bp, t_len=seq, tm=tm_p, tm_in=tm_in_p,
            fox_fn=lambda qkvb, vt, sm, l: _fox_prompt(qkvb, vt, sm, fox_consts, bp, seq, tq=tq),
            shift0=zeros_p["shift"], rw_s0=zeros_p["rw_s"], conv0=zeros_p["conv"], gd_s0=zeros_p["gd_s"],
            chunk=chunk_p, n_chunks=n_chunks_p, t_valid=chunk_p, last=last)
        _, _, kt, vt, sm, zr, rw_st, gqkv, gd_s = st
        p_states.append((
            kt, vt,
            sm[:, :FOX_HEADS].reshape(bp, seq, FOX_HEADS),
            zr.reshape(bp, seq, RW_COLS)[:, seq - 1],
            rw_st,
            gqkv.reshape(bp, seq, GD_QKV)[:, seq - (GD_CONV - 1):],
            gd_s))
        xs, st = layer_fn(
            xs, l, batch=bd, t_len=tp, tm=tm_s, tm_in=tm_s,
            fox_fn=lambda qkvb, vt, sm, l: _fox_sample(qkvb, sm, ckt, cvt, clft, page_table, l,
                                                       t_valid=t_s, pps=pps),
            shift0=state_rwkv_shift[l][:, None, :], rw_s0=state_rwkv, conv0=gd_conv0_s[l],
            gd_s0=state_gdn, chunk=tp, n_chunks=1, t_valid=t_s, last=last)
        fk, fv, _, _, sm, zr, rw_st, gqkv, gd_s = st
        conv_ext = jnp.concatenate([state_gdn_conv[l], gqkv.reshape(bd, tp, GD_QKV)[:, :t_s]], axis=1)
        s_states.append((
            fk.reshape(bd, tp, FOX_HEADS, HEAD_DIM)[:, :t_s], fv.reshape(bd, tp, FOX_HEADS, HEAD_DIM)[:, :t_s],
            sm[:, :FOX_HEADS].reshape(bd, tp, FOX_HEADS)[:, :t_s],
            zr.reshape(bd, tp, RW_COLS)[:, t_s - 1],
            rw_st,
            conv_ext[:, t_s:],
            gd_s))

    p_out = [jnp.stack(s) for s in zip(*p_states)]
    for i in (0, 1):
        p_out[i] = jnp.transpose(p_out[i].reshape(depth, bp, FOX_HEADS, HEAD_DIM, seq), (0, 1, 4, 2, 3))
    s_out = [jnp.stack(s) for s in zip(*s_states)]
    y_prompt = xp.reshape(bp, seq, d)
    y_sample = xs.reshape(bd, tp, d)[:, :t_s]
    return (y_prompt, y_sample, *p_out, *s_out)
```

```python
import functools

import jax
import jax.numpy as jnp
from jax import lax
from jax.experimental import pallas as pl
from jax.experimental.pallas import tpu as pltpu

F32 = jnp.float32
BF16 = jnp.bfloat16

HEAD_DIM = 64
FOX_HEADS = 8
FOX_W = FOX_HEADS * HEAD_DIM
RW_HEADS = 8
RW_W = RW_HEADS * HEAD_DIM
RW_LORA_W = 64
RW_LORA_A = 64
RW_LORA_G = 128
RW_COLS = 3 * RW_W + RW_LORA_W + RW_LORA_A + RW_LORA_G
RW_GN_EPS = 64e-5
GD_HEADS = 4
GD_DK = 128
GD_DV = 128
GD_W = GD_HEADS * GD_DK
GD_QKV = 2 * GD_W + GD_HEADS * GD_DV
GD_CONV = 4
N_BRANCH = 3
EPS = 1e-6
L2_EPS = 1e-6
N_SMALL = 16
SUBLANES = 8
GROUP = 4
RW_GROUPS = RW_HEADS // GROUP
RW_GW = GROUP * HEAD_DIM
VMEM_LIMIT = 56 * 1024 * 1024


def _cparams(*sem):
    return pltpu.CompilerParams(dimension_semantics=sem, vmem_limit_bytes=VMEM_LIMIT)


def _const_spec(shape, index_map):
    return pl.BlockSpec(shape, index_map, pipeline_mode=pl.Buffered(1))


def _dot(a, b):
    return jnp.dot(a, b, preferred_element_type=F32)


def _dot_nt(a, b):
    return lax.dot_general(a, b, (((1,), (1,)), ((), ())), preferred_element_type=F32)


def _dot_tn(a, b):
    return lax.dot_general(a, b, (((0,), (0,)), ((), ())), preferred_element_type=F32)


def _split2(x):
    hi = x.astype(BF16)
    lo = (x - hi.astype(F32)).astype(BF16)
    return hi, lo


def _split3(x):
    hi = x.astype(BF16)
    r = x - hi.astype(F32)
    mid = r.astype(BF16)
    lo = (r - mid.astype(F32)).astype(BF16)
    return hi, mid, lo


def _dot_sel_r(x, sel):
    hi, mid, lo = _split3(x)
    return _dot(hi, sel) + _dot(mid, sel) + _dot(lo, sel)


def _dot_sel_l(sel, x):
    return _dot(jnp.concatenate([sel] * 3, axis=1), jnp.concatenate(_split3(x), axis=0))


def _seg_sum(x, seg2):
    return _dot(jnp.concatenate(_split2(x), axis=1), seg2)


def _softplus(z):
    return jnp.maximum(z, 0.0) + jnp.log(1.0 + jnp.exp(-jnp.abs(z)))


def _sigmoid(z):
    return 1.0 / (1.0 + jnp.exp(-z))


def _rms(x, g):
    return x * lax.rsqrt(jnp.mean(x * x, axis=-1, keepdims=True) + EPS) * g


def _tri(n, kind):
    r = lax.broadcasted_iota(jnp.int32, (n, n), 0)
    c = lax.broadcasted_iota(jnp.int32, (n, n), 1)
    return {"incl": r >= c, "strict": r > c, "upper_incl": r <= c}[kind]


def _block_tri(n, blk, strict, reps=1):
    r = lax.broadcasted_iota(jnp.int32, (n, n * reps), 0)
    c = lax.broadcasted_iota(jnp.int32, (n, n * reps), 1) % n
    same = (r // blk) == (c // blk)
    return same & ((r > c) if strict else (r >= c))


def _head_mask(rows, row_blk, cols, col_blk):
    r = lax.broadcasted_iota(jnp.int32, (rows, cols), 0) // row_blk
    c = lax.broadcasted_iota(jnp.int32, (rows, cols), 1) // col_blk
    return r == c


def _expand(x, mask):
    return jnp.where(mask, jnp.concatenate([x] * GROUP, axis=0), jnp.zeros((), x.dtype))


def _collapse(x, t):
    out = x[0:t]
    for h in range(1, GROUP):
        out = out + x[h * t:(h + 1) * t]
    return out


def _unit_lower_inverse_minus_eye(m_wide, order, amask):
    width = m_wide.shape[1]
    npow = -m_wide
    t = npow
    if order <= 2:
        return t
    nb = npow.astype(BF16)
    npow = _dot(nb, _expand(nb, amask))
    yield
    span = 4
    while span <= order:
        nb = npow.astype(BF16)
        tb = _expand(t.astype(BF16), amask)
        if span < order:
            prod = _dot(nb, jnp.concatenate([tb, _expand(nb, amask)], axis=1))
            t = t + npow + prod[:, :width]
            npow = prod[:, width:]
        else:
            t = t + npow + _dot(nb, tb)
        yield
        span *= 2
    return t


def _run_tasks(tasks):
    done, started, active = {}, set(), []
    pending = list(tasks)
    while pending or active:
        for task in list(pending):
            name, needs_done, needs_started, fn = task
            if all(n in done for n in needs_done) and all(n in started for n in needs_started):
                pending.remove(task)
                started.add(name)
                active.append((name, fn(done)))
        assert active, "task dependencies can never be met"
        for entry in list(active):
            name, gen = entry
            try:
                next(gen)
            except StopIteration as stop:
                done[name] = stop.value
                active.remove(entry)
    return done


def _ffn_body(x_ref, g_ref, wg_ref, wu_ref, wd_ref, fg_ref, o_ref, *, ff_chunk, final_norm):
    x = x_ref[...]
    h = _rms(x, g_ref[...]).astype(BF16)
    d_ff = wg_ref.shape[1]
    acc = jnp.zeros_like(x)
    for c0 in range(0, d_ff, ff_chunk):
        gate = _dot(h, wg_ref[:, c0:c0 + ff_chunk])
        up = _dot(h, wu_ref[:, c0:c0 + ff_chunk])
        act = (gate * _sigmoid(gate) * up).astype(BF16)
        acc = acc + _dot(act, wd_ref[c0:c0 + ff_chunk, :])
    y = x + 0.5 * acc
    if final_norm:
        y = _rms(y, fg_ref[...])
    o_ref[...] = y


def _ffn(x, norm_g, wg, wu, wd, final_g, layer, *, tm, final_norm=False):
    m, d = x.shape
    d_ff = wg.shape[2]
    ff_chunk = d_ff // 2 if (d_ff // 2) % 128 == 0 else d_ff
    body = functools.partial(_ffn_body, ff_chunk=ff_chunk, final_norm=final_norm)
    return pl.pallas_call(
        body,
        grid=(m // tm,),
        in_specs=[
            pl.BlockSpec((tm, d), lambda i: (i, 0)),
            _const_spec((None, 1, d), lambda i: (layer, 0, 0)),
            _const_spec((None, d, d_ff), lambda i: (layer, 0, 0)),
            _const_spec((None, d, d_ff), lambda i: (layer, 0, 0)),
            _const_spec((None, d_ff, d), lambda i: (layer, 0, 0)),
            _const_spec((1, d), lambda i: (0, 0)),
        ],
        out_specs=pl.BlockSpec((tm, d), lambda i: (i, 0)),
        out_shape=jax.ShapeDtypeStruct((m, d), F32),
        compiler_params=_cparams("parallel"),
        name="ffn",
    )(x, norm_g, wg, wu, wd, final_g)


def _small_fn(z, is_logf, is_beta, neg_exp_a):
    return jnp.where(is_logf, -_softplus(-z), jnp.where(is_beta, _sigmoid(z), neg_exp_a * _softplus(z)))


def _inproj_body(x_ref, g_ref, wfox_ref, wrw_ref, wgd_ref, wgate_ref, wsm_ref, wkvt_ref, pc_ref,
                 qkvb_ref, fk_ref, fv_ref, kt_ref, vt_ref, zr_ref, gqkv_ref, gz_ref, zg_ref, sm_ref):
    h = _rms(x_ref[...], g_ref[...]).astype(BF16)
    fox = _dot(h, wfox_ref[...])
    qkvb_ref[...] = fox.astype(BF16)
    fk_ref[...] = fox[:, FOX_W:2 * FOX_W]
    fv_ref[...] = fox[:, 2 * FOX_W:]
    kvt = _dot_nt(wkvt_ref[...], h)
    kt_ref[...] = kvt[:FOX_W]
    vt_ref[...] = kvt[FOX_W:]
    zr_ref[...] = _dot(h, wrw_ref[...])
    gd = _dot(h, wgd_ref[...])
    gqkv_ref[...] = gd[:, :GD_QKV]
    gz_ref[...] = gd[:, GD_QKV:]
    zg_ref[...] = _dot(h, wgate_ref[...])
    pc = pc_ref[...]
    col = lax.broadcasted_iota(jnp.int32, (1, N_SMALL), 1)
    sm = _dot(h, wsm_ref[...]) + pc[0:1]
    sm_ref[...] = _small_fn(sm, col < 8, col < 12, -jnp.exp(pc[1:2]))


def _inproj(x, norm_g, w, layer, *, tm, seq):
    m, d = x.shape
    n_gate = w["gate"].shape[2]
    n_gd = w["gd"].shape[2]
    t_batch, t_seq = (m // seq, seq) if seq % tm == 0 else (1, m)
    tiles_per_seq = t_seq // tm
    row_blk = lambda n: pl.BlockSpec((tm, n), lambda i: (i, 0))
    t_blk = pl.BlockSpec((None, FOX_W, tm), lambda i: (i // tiles_per_seq, 0, i % tiles_per_seq))
    wspec = lambda a: _const_spec((None,) + a.shape[1:], lambda i: (layer, 0, 0))
    out_shape = (
        jax.ShapeDtypeStruct((m, 3 * FOX_W), BF16),
        jax.ShapeDtypeStruct((m, FOX_W), F32),
        jax.ShapeDtypeStruct((m, FOX_W), F32),
        jax.ShapeDtypeStruct((t_batch, FOX_W, t_seq), F32),
        jax.ShapeDtypeStruct((t_batch, FOX_W, t_seq), F32),
        jax.ShapeDtypeStruct((m, RW_COLS), F32),
        jax.ShapeDtypeStruct((m, GD_QKV), F32),
        jax.ShapeDtypeStruct((m, n_gd - GD_QKV), F32),
        jax.ShapeDtypeStruct((m, n_gate), F32),
        jax.ShapeDtypeStruct((m, N_SMALL), F32),
    )
    out_specs = (
        row_blk(3 * FOX_W), row_blk(FOX_W), row_blk(FOX_W), t_blk, t_blk, row_blk(RW_COLS), row_blk(GD_QKV),
        row_blk(n_gd - GD_QKV), row_blk(n_gate), row_blk(N_SMALL),
    )
    return pl.pallas_call(
        _inproj_body,
        grid=(m // tm,),
        in_specs=[
            pl.BlockSpec((tm, d), lambda i: (i, 0)),
            _const_spec((None, 1, d), lambda i: (layer, 0, 0)),
            wspec(w["fox"]), wspec(w["rw"]), wspec(w["gd"]), wspec(w["gate"]),
            wspec(w["small"]), wspec(w["kv_t"]), wspec(w["small_pc"]),
        ],
        out_specs=out_specs,
        out_shape=out_shape,
        compiler_params=_cparams("parallel"),
        name="inproj",
    )(x, norm_g, w["fox"], w["rw"], w["gd"], w["gate"], w["small"], w["kv_t"], w["small_pc"])


FOX_AW = 2 * HEAD_DIM
N_PIECES = 3


def _eye(n):
    return (lax.broadcasted_iota(jnp.int32, (n, n), 0) == lax.broadcasted_iota(jnp.int32, (n, n), 1)).astype(BF16)


def _fox_prompt_body(q_ref, k_ref, vt_in_ref, lf_ref, pk_ref, pc_ref, one_ref, o_ref,
                     kaug_ref, vt_ref, qaug_ref, m_ref, l_ref, acc_ref, *, tq, cb):
    i = pl.program_id(1)
    seq = k_ref.shape[0]

    @pl.when(i == 0)
    def _():
        lower = _tri(cb, "incl").astype(BF16)
        carry = jnp.zeros((1, N_SMALL), F32)
        for b0 in range(0, seq, cb):
            cc = _dot_sel_l(lower, lf_ref[b0:b0 + cb, :]) + carry
            carry = cc[cb - 1:cb, :]
            pieces = jnp.concatenate(_split3(-cc), axis=1)
            kaug_ref[b0:b0 + cb, :] = (_dot(k_ref[b0:b0 + cb, :], pk_ref[...])
                                       + _dot(pieces, pc_ref[...])).astype(BF16)
            vt_ref[:, b0:b0 + cb] = vt_in_ref[:, b0:b0 + cb].astype(BF16)

    qaug_ref[...] = (_dot(q_ref[...], pk_ref[...]) + one_ref[...]).astype(BF16)
    m_ref[...] = jnp.full(m_ref.shape, -jnp.inf, F32)
    l_ref[...] = jnp.zeros(l_ref.shape, F32)
    acc_ref[...] = jnp.zeros(acc_ref.shape, F32)

    def tile(j, diagonal):
        k0 = pl.multiple_of(j * tq, tq)
        if diagonal:
            visible = _tri(tq, "upper_incl")
        scores = [_dot_nt(kaug_ref[pl.ds(k0, tq), h * FOX_AW:(h + 1) * FOX_AW],
                          qaug_ref[:, h * FOX_AW:(h + 1) * FOX_AW]) for h in range(FOX_HEADS)]
        for h in range(FOX_HEADS):
            hs = slice(h * HEAD_DIM, (h + 1) * HEAD_DIM)
            st = scores[h]
            if diagonal:
                st = jnp.where(visible, st, -jnp.inf)
            m_old = m_ref[h]
            m_new = jnp.maximum(m_old, jnp.max(st, axis=0, keepdims=True))
            alpha = jnp.exp(m_old - m_new)
            p = jnp.exp(st - m_new)
            m_ref[h] = m_new
            l_ref[h] = alpha * l_ref[h] + jnp.sum(p, axis=0, keepdims=True)
            acc_ref[hs, :] = alpha * acc_ref[hs, :] + _dot(vt_ref[hs, pl.ds(k0, tq)], p.astype(BF16))

    def body(j, carry):
        tile(j, False)
        return carry

    lax.fori_loop(0, i, body, 0)
    tile(i, True)
    out_t = jnp.concatenate(
        [acc_ref[h * HEAD_DIM:(h + 1) * HEAD_DIM, :] / l_ref[h] for h in range(FOX_HEADS)], axis=0)
    o_ref[...] = _dot_nt(_eye(tq), out_t.astype(BF16)).astype(BF16)


def _fox_prompt(qkvb, vt, sm, consts, batch, seq, *, tq):
    m = batch * seq
    nq = seq // tq
    cb = min(512, seq)
    body = functools.partial(_fox_prompt_body, tq=tq, cb=cb)
    aug_w = FOX_HEADS * FOX_AW
    return pl.pallas_call(
        body,
        grid=(batch, nq),
        in_specs=[
            pl.BlockSpec((tq, FOX_W), lambda b, i: (b * nq + i, 0)),
            pl.BlockSpec((seq, FOX_W), lambda b, i: (b, 1)),
            pl.BlockSpec((None, FOX_W, seq), lambda b, i: (b, 0, 0)),
            pl.BlockSpec((seq, N_SMALL), lambda b, i: (b, 0)),
            _const_spec((FOX_W, aug_w), lambda b, i: (0, 0)),
            _const_spec((N_PIECES * N_SMALL, aug_w), lambda b, i: (0, 0)),
            _const_spec((1, aug_w), lambda b, i: (0, 0)),
        ],
        out_specs=pl.BlockSpec((tq, FOX_W), lambda b, i: (b * nq + i, 0)),
        out_shape=jax.ShapeDtypeStruct((m, FOX_W), BF16),
        scratch_shapes=[pltpu.VMEM((seq, aug_w), BF16), pltpu.VMEM((FOX_W, seq), BF16),
                        pltpu.VMEM((tq, aug_w), BF16),
                        pltpu.VMEM((FOX_HEADS, 1, tq), F32), pltpu.VMEM((FOX_HEADS, 1, tq), F32),
                        pltpu.VMEM((FOX_W, tq), F32)],
        compiler_params=_cparams("parallel", "arbitrary"),
        name="fox_prompt",
    )(qkvb, qkvb, vt, sm, consts["place_k"], consts["place_c"], consts["ones"])


def _fox_prompt_consts():
    aug_w = FOX_HEADS * FOX_AW
    lane = jnp.arange(aug_w)
    src = jnp.arange(FOX_W)
    place_k = (lane[None, :] == (src // HEAD_DIM * FOX_AW + src % HEAD_DIM)[:, None]).astype(BF16)
    col = jnp.arange(N_PIECES * N_SMALL)
    tgt = jnp.where(col % N_SMALL < FOX_HEADS, (col % N_SMALL) * FOX_AW + HEAD_DIM + col // N_SMALL, -1)
    place_c = (lane[None, :] == tgt[:, None]).astype(BF16)
    ones = ((lane % FOX_AW >= HEAD_DIM) & (lane % FOX_AW < HEAD_DIM + N_PIECES)).astype(F32)[None, :]
    return {"place_k": place_k, "place_c": place_c, "ones": ones}


def _fox_sample_body(pt_ref, q_ref, kn_ref, vn_ref, lfn_ref, *rest, t_valid, pps):
    kp_refs = rest[0:pps]
    vp_refs = rest[pps:2 * pps]
    lf_refs = rest[2 * pps:3 * pps]
    o_ref, qbd_ref, m_ref, l_ref, acc_ref, tail_ref = rest[3 * pps:]
    s_idx = pl.program_id(1)
    tp = SUBLANES
    nrow = t_valid * FOX_HEADS
    page = kp_refs[0].shape[1]
    diag = (lax.broadcasted_iota(jnp.int32, (nrow, FOX_W), 0) % FOX_HEADS
            == lax.broadcasted_iota(jnp.int32, (nrow, FOX_W), 1) // HEAD_DIM)

    def expand_rows(x):
        return jnp.concatenate([x] * t_valid, axis=0)

    @pl.when(s_idx == 0)
    def _():
        qf = q_ref[...].astype(F32)
        q_rows = jnp.concatenate([jnp.broadcast_to(qf[t:t + 1], (FOX_HEADS, FOX_W)) for t in range(t_valid)],
                                 axis=0)
        qbd = jnp.where(diag, q_rows, 0.0).astype(BF16)
        qbd_ref[...] = qbd
        upper = _tri(tp, "upper_incl").astype(BF16)
        lf_hi, lf_mid, lf_lo = _split3(lfn_ref[...])
        cn = (_dot_tn(lf_hi, upper) + _dot_tn(lf_mid, upper) + _dot_tn(lf_lo, upper))[:FOX_HEADS]
        s = _dot_nt(qbd, kn_ref[...]) - expand_rows(cn)
        t_q = lax.broadcasted_iota(jnp.int32, (nrow, tp), 0) // FOX_HEADS
        u_k = lax.broadcasted_iota(jnp.int32, (nrow, tp), 1)
        s = jnp.where((u_k <= t_q) & (u_k < t_valid), s, -jnp.inf)
        m = jnp.max(s, axis=-1, keepdims=True)
        p = jnp.exp(s - m)
        m_ref[...] = m
        l_ref[...] = jnp.sum(p, axis=-1, keepdims=True)
        acc_ref[...] = _dot(p.astype(BF16), vn_ref[...])
        tail_ref[...] = jnp.zeros_like(tail_ref)

    @pl.when(s_idx > 0)
    def _():
        later3 = jnp.concatenate([_tri(page, "strict").astype(BF16)] * 3, axis=0)
        qbd = qbd_ref[...]
        tail = tail_ref[...]
        scores = [None] * pps
        for r in reversed(range(pps)):
            lf = lf_refs[r][...]
            suf = _dot(jnp.concatenate(_split3(lf), axis=1), later3) + tail
            tail = tail + jnp.sum(lf, axis=-1, keepdims=True)
            scores[r] = _dot(qbd, kp_refs[r][...].astype(BF16)) + expand_rows(suf)
        tail_ref[...] = tail
        m_old = m_ref[...]
        m_new = m_old
        for r in range(pps):
            m_new = jnp.maximum(m_new, jnp.max(scores[r], axis=-1, keepdims=True))
        alpha = jnp.exp(m_old - m_new)
        l_new = alpha * l_ref[...]
        acc = alpha * acc_ref[...]
        for r in range(pps):
            p = jnp.exp(scores[r] - m_new)
            l_new = l_new + jnp.sum(p, axis=-1, keepdims=True)
            acc = acc + _dot_nt(p.astype(BF16), vp_refs[r][...].astype(BF16))
        m_ref[...] = m_new
        l_ref[...] = l_new
        acc_ref[...] = acc

    @pl.when(s_idx == pl.num_programs(1) - 1)
    def _():
        o = jnp.where(diag, acc_ref[...] / l_ref[...], 0.0).astype(BF16)
        pick = (lax.broadcasted_iota(jnp.int32, (tp, nrow), 1) // FOX_HEADS
                == lax.broadcasted_iota(jnp.int32, (tp, nrow), 0)).astype(BF16)
        o_ref[...] = _dot(pick, o).astype(BF16)


def _fox_sample(qkvb, sm, cache_kt, cache_vt, cache_lft, page_table, layer, *, t_valid, pps):
    tp = SUBLANES
    bd, n_pages = page_table.shape
    page = cache_kt.shape[3]
    nrow = t_valid * FOX_HEADS
    assert n_pages % pps == 0

    def page_spec(r, blk):
        def index_map(b, s, pt):
            return (layer, pt[b, n_pages - jnp.maximum(s, 1) * pps + r]) + (0,) * (len(blk) - 2)
        return pl.BlockSpec(blk, index_map)

    kv_blk = (None, None, FOX_W, page)
    lf_blk = (None, None, FOX_HEADS, page)
    grid_spec = pltpu.PrefetchScalarGridSpec(
        num_scalar_prefetch=1,
        grid=(bd, n_pages // pps + 1),
        in_specs=[
            pl.BlockSpec((tp, FOX_W), lambda b, s, pt: (b, 0)),
            pl.BlockSpec((tp, FOX_W), lambda b, s, pt: (b, 1)),
            pl.BlockSpec((tp, FOX_W), lambda b, s, pt: (b, 2)),
            pl.BlockSpec((tp, N_SMALL), lambda b, s, pt: (b, 0)),
            *[page_spec(r, kv_blk) for r in range(pps)],
            *[page_spec(r, kv_blk) for r in range(pps)],
            *[page_spec(r, lf_blk) for r in range(pps)],
        ],
        out_specs=pl.BlockSpec((tp, FOX_W), lambda b, s, pt: (b, 0)),
        scratch_shapes=[
            pltpu.VMEM((nrow, FOX_W), BF16),
            pltpu.VMEM((nrow, 1), F32),
            pltpu.VMEM((nrow, 1), F32),
            pltpu.VMEM((nrow, FOX_W), F32),
            pltpu.VMEM((FOX_HEADS, 1), F32),
        ],
    )
    body = functools.partial(_fox_sample_body, t_valid=t_valid, pps=pps)
    return pl.pallas_call(
        body,
        grid_spec=grid_spec,
        out_shape=jax.ShapeDtypeStruct((bd * tp, FOX_W), BF16),
        compiler_params=_cparams("parallel", "arbitrary"),
        name="fox_sample",
    )(page_table, qkvb, qkvb, qkvb, sm, *([cache_kt] * pps), *([cache_vt] * pps), *([cache_lft] * pps))


def _rwkv_body(z_ref, shift_ref, s0_ref, mu_ref, pv_ref, wup_ref, aup_ref, gup_ref, seg_ref,
               o_ref, so_ref, sbd_ref, prev_ref, *, chunk, n_chunks, n_seq, t_valid):
    c = pl.program_id(1)
    tb = chunk * n_chunks
    rows_all = n_seq * tb
    rows_x = GROUP * chunk
    lane_mask = _head_mask(rows_x, chunk, RW_GW, HEAD_DIM)
    amask = _head_mask(rows_x, chunk, rows_x, chunk)
    state_mask = _head_mask(RW_GW, HEAD_DIM, RW_GW, HEAD_DIM)

    @pl.when(c == 0)
    def _():
        for i in range(n_seq):
            prev_ref[i, 0:1, :] = shift_ref[i]
            for g in range(RW_GROUPS):
                blocks = [jnp.concatenate([s0_ref[i, g * GROUP + h]] * GROUP, axis=1) for h in range(GROUP)]
                sbd_ref[i * RW_GROUPS + g] = jnp.where(state_mask, jnp.concatenate(blocks, axis=0), 0.0)

    z = z_ref[...].reshape(rows_all, RW_COLS)
    row = lax.broadcasted_iota(jnp.int32, (rows_all, 1), 0)
    z_prev = pltpu.roll(z, 1, axis=0)
    for i in range(n_seq):
        z_prev = jnp.where(row == i * tb, prev_ref[i, 0:1, :], z_prev)
        prev_ref[i, 0:1, :] = z[(i + 1) * tb - 1:(i + 1) * tb, :]
    zs = z + (z_prev - z) * mu_ref[...]
    r = zs[:, 0:RW_W]
    k = zs[:, RW_W:2 * RW_W]
    v = zs[:, 2 * RW_W:3 * RW_W]
    o_l = 3 * RW_W
    wd = zs[:, o_l:o_l + RW_LORA_W]
    ad = zs[:, o_l + RW_LORA_W:o_l + RW_LORA_W + RW_LORA_A]
    gd = zs[:, o_l + RW_LORA_W + RW_LORA_A:]
    pv = pv_ref[...]
    w_log = -_softplus(-(pv[0:1] + _dot(jnp.tanh(wd).astype(BF16), wup_ref[...]))) - 0.5
    lw = -jnp.exp(w_log)
    a = _sigmoid(pv[1:2] + _dot(ad.astype(BF16), aup_ref[...]))
    g_out = _dot(_sigmoid(gd).astype(BF16), gup_ref[...])
    kk = k * pv[2:3]
    k = k * (1.0 + (a - 1.0) * pv[3:4])
    seg2 = seg_ref[...]
    kk = kk * lax.rsqrt(_seg_sum(kk * kk, seg2) + L2_EPS)
    if t_valid < chunk:
        valid = row % tb < t_valid
        lw = jnp.where(valid, lw, 0.0)
        kk = jnp.where(valid, kk, 0.0)
        k = jnp.where(valid, k, 0.0)
    cs = _dot_sel_l(_block_tri(rows_all, chunk, strict=False).astype(BF16), lw)
    p_inv = jnp.exp(-cs)
    rp = (r * jnp.exp(cs)).astype(BF16)
    kkp = (kk * jnp.exp(cs - lw)).astype(BF16)
    bn = kk * a * p_inv
    kn = k * p_inv
    vb = v.astype(BF16)
    t_idx = lax.broadcasted_iota(jnp.int32, (chunk, 2 * rows_x), 0)
    j_idx = lax.broadcasted_iota(jnp.int32, (chunk, 2 * rows_x), 1) % chunk
    strict = (t_idx > j_idx)[:, :rows_x]
    incl2 = t_idx >= j_idx
    groups = range(RW_GROUPS)
    gsl = [slice(g * RW_GW, (g + 1) * RW_GW) for g in groups]
    bnb = bn.astype(BF16)
    knb = kn.astype(BF16)

    def prepare(ci, g):
        def gen(done):
            rs = slice(ci * chunk, (ci + 1) * chunk)
            bk_x = jnp.concatenate([_expand(bnb[rs, gsl[g]], lane_mask), _expand(knb[rs, gsl[g]], lane_mask)],
                                   axis=0)
            a_all = _dot_nt(jnp.concatenate([kkp[rs, gsl[g]], rp[rs, gsl[g]]], axis=0), bk_x)
            yield
            a_ak = jnp.where(strict, a_all[:chunk, rows_x:], 0.0).astype(BF16)
            a_r = jnp.where(incl2, a_all[chunk:, :], 0.0).astype(BF16)
            t_m1 = yield from _unit_lower_inverse_minus_eye(
                jnp.where(strict, a_all[:chunk, :rows_x], 0.0), chunk, amask)
            return a_ak, a_r, t_m1.astype(BF16)
        return gen

    def advance(ci):
        def gen(done):
            rs = slice(ci * chunk, (ci + 1) * chunk)
            pre = [done["prepare", ci, g] for g in groups]
            p_end = jnp.exp(cs[(ci + 1) * chunk - 1:(ci + 1) * chunk, :])
            bk_e = jnp.concatenate([(bn[rs] * p_end).astype(BF16), (kn[rs] * p_end).astype(BF16)], axis=0)
            v_xs = [_expand(vb[rs, gsl[g]], lane_mask) for g in groups]
            s_base = ci // n_chunks * RW_GROUPS
            sts = [sbd_ref[s_base + g] for g in groups]
            stbs = [st.astype(BF16) for st in sts]
            rhs = [_dot_nt(kkp[rs, gsl[g]], stbs[g]) + _dot(pre[g][0], v_xs[g]) for g in groups]
            yield
            sabs = [(-(rhs[g] + _dot(pre[g][2], _expand(rhs[g].astype(BF16), lane_mask)))).astype(BF16)
                    for g in groups]
            yield
            y_c = jnp.concatenate(
                [_dot_nt(rp[rs, gsl[g]], stbs[g])
                 + _dot(pre[g][1], jnp.concatenate([_expand(sabs[g], lane_mask), v_xs[g]], axis=0))
                 for g in groups], axis=1)
            yield
            for g in groups:
                outer = _dot_tn(jnp.concatenate([sabs[g], vb[rs, gsl[g]]], axis=0), bk_e[:, gsl[g]])
                sbd_ref[s_base + g] = sts[g] * p_end[:, gsl[g]] + jnp.where(state_mask, outer, 0.0)
            yield
            return y_c
        return gen

    tasks = []
    for ci in range(n_seq * n_chunks):
        tasks.append((("advance", ci),
                      [("prepare", ci, g) for g in groups] + ([("advance", ci - 1)] if ci % n_chunks else []), [],
                      advance(ci)))
        for g in groups:
            tasks.append((("prepare", ci, g), [], [], prepare(ci, g)))
    done = _run_tasks(tasks)
    y_rows = [done["advance", ci] for ci in range(n_seq * n_chunks)]
    y = y_rows[0] if len(y_rows) == 1 else jnp.concatenate(y_rows, axis=0)
    inv_n = 1.0 / HEAD_DIM
    mean = _seg_sum(y, seg2) * inv_n
    yc = y - mean
    var = _seg_sum(yc * yc, seg2) * inv_n
    yn = yc * lax.rsqrt(var + RW_GN_EPS) * pv[5:6] + pv[6:7]
    bonus = _seg_sum(r * k * pv[4:5], seg2) * v
    o_ref[...] = ((yn + bonus) * g_out).reshape(n_seq, tb, RW_W).astype(BF16)

    @pl.when(c == pl.num_programs(1) - 1)
    def _():
        for i in range(n_seq):
            for g in range(RW_GROUPS):
                st = sbd_ref[i * RW_GROUPS + g]
                for h in range(GROUP):
                    hs = slice(h * HEAD_DIM, (h + 1) * HEAD_DIM)
                    so_ref[i, g * GROUP + h] = st[hs, hs]


def _rwkv(zr, shift0, s0, w, layer, batch, seq, *, chunk, n_chunks, n_seq, t_valid):
    tb = chunk * n_chunks
    nc = seq // tb
    assert batch % n_seq == 0
    body = functools.partial(_rwkv_body, chunk=chunk, n_chunks=n_chunks, n_seq=n_seq, t_valid=t_valid)
    lspec = lambda a: _const_spec((None,) + a.shape[1:], lambda b, c: (layer, 0, 0))
    state_spec = pl.BlockSpec((n_seq, RW_HEADS, HEAD_DIM, HEAD_DIM), lambda b, c: (b, 0, 0, 0))
    s0_layer = layer if s0.shape[0] > 1 else 0
    o_rw, state = pl.pallas_call(
        body,
        grid=(batch // n_seq, nc),
        in_specs=[
            pl.BlockSpec((n_seq, tb, RW_COLS), lambda b, c: (b, c, 0)),
            pl.BlockSpec((n_seq, 1, RW_COLS), lambda b, c: (b, 0, 0)),
            pl.BlockSpec((None, n_seq, RW_HEADS, HEAD_DIM, HEAD_DIM), lambda b, c: (s0_layer, b, 0, 0, 0)),
            lspec(w["mu"]), lspec(w["pv"]), lspec(w["w_up"]), lspec(w["a_up"]), lspec(w["g_up"]),
            _const_spec((2 * RW_W, RW_W), lambda b, c: (0, 0)),
        ],
        out_specs=(pl.BlockSpec((n_seq, tb, RW_W), lambda b, c: (b, c, 0)), state_spec),
        out_shape=(
            jax.ShapeDtypeStruct((batch, seq, RW_W), BF16),
            jax.ShapeDtypeStruct((batch, RW_HEADS, HEAD_DIM, HEAD_DIM), F32),
        ),
        scratch_shapes=[pltpu.VMEM((n_seq * RW_GROUPS, RW_GW, RW_GW), F32),
                        pltpu.VMEM((n_seq, SUBLANES, RW_COLS), F32)],
        compiler_params=_cparams("parallel", "arbitrary"),
        name="rwkv",
    )(zr.reshape(batch, seq, RW_COLS), shift0, s0, w["mu"], w["pv"], w["w_up"], w["a_up"], w["g_up"], w["seg2"])
    return o_rw.reshape(batch * seq, RW_W), state


def _gdn_body(x_ref, gz_ref, sm_ref, conv0_ref, s0_ref, cw_ref, ng_ref, seg_ref,
              o_ref, so_ref, prev_ref, *, chunk, n_chunks, n_seq, t_valid):
    c = pl.program_id(1)
    tb = chunk * n_chunks
    rows_all = n_seq * tb
    rows_x = GD_HEADS * chunk
    lane_mask = _head_mask(rows_x, chunk, GD_W, GD_DK)
    amask = _head_mask(rows_x, chunk, rows_x, chunk)

    @pl.when(c == 0)
    def _():
        prev_ref[...] = conv0_ref[...]
        so_ref[...] = s0_ref[...]

    x = x_ref[...].reshape(rows_all, GD_QKV)
    prevs = [prev_ref[i] for i in range(n_seq)]
    for i in range(n_seq):
        prev_ref[i] = x[(i + 1) * tb - SUBLANES:(i + 1) * tb, :]
    cw = cw_ref[...]
    row8 = lax.broadcasted_iota(jnp.int32, (SUBLANES, 1), 0)
    conv = x * cw[GD_CONV - 1:GD_CONV]
    for sft in range(1, GD_CONV):
        xr = pltpu.roll(x, sft, axis=0)
        parts = []
        for i in range(n_seq):
            parts.append(jnp.where(row8 < sft, pltpu.roll(prevs[i], sft, axis=0), xr[i * tb:i * tb + SUBLANES]))
            if tb > SUBLANES:
                parts.append(xr[i * tb + SUBLANES:(i + 1) * tb])
        xs = parts[0] if len(parts) == 1 else jnp.concatenate(parts, axis=0)
        conv = conv + xs * cw[GD_CONV - 1 - sft:GD_CONV - sft]
    act = conv * _sigmoid(conv)
    seg2 = seg_ref[...]
    q = act[:, :GD_W]
    k = act[:, GD_W:2 * GD_W]
    v = act[:, 2 * GD_W:]
    q = q * lax.rsqrt(_seg_sum(q * q, seg2) + L2_EPS) * (GD_DK ** -0.5)
    k = k * lax.rsqrt(_seg_sum(k * k, seg2) + L2_EPS)
    sm = sm_ref[...].reshape(rows_all, N_SMALL)
    beta = sm[:, 8:8 + GD_HEADS]
    g = sm[:, 12:12 + GD_HEADS]
    if t_valid < chunk:
        rowc = lax.broadcasted_iota(jnp.int32, (rows_all, 1), 0) % tb
        beta = jnp.where(rowc < t_valid, beta, 0.0)
        g = jnp.where(rowc < t_valid, g, 0.0)
    gc = _dot_sel_l(_block_tri(rows_all, chunk, strict=False).astype(BF16), g)
    head_lanes = _head_mask(GD_HEADS, 1, GD_W, GD_DK).astype(BF16)
    beta_l = _dot_sel_r(beta, head_lanes)
    gc_l = _dot_sel_r(gc, head_lanes)
    egc_l = jnp.exp(gc_l)
    kb = k * beta_l
    vbeta = (v * beta_l).astype(BF16)
    kbg = (kb * egc_l).astype(BF16)
    qg = (q * egc_l).astype(BF16)
    kb = kb.astype(BF16)
    qb = q.astype(BF16)
    kbf = k.astype(BF16)
    t_idx = lax.broadcasted_iota(jnp.int32, (chunk, rows_x), 0)
    s_idx = lax.broadcasted_iota(jnp.int32, (chunk, rows_x), 1) % chunk
    strict = t_idx > s_idx
    incl = t_idx >= s_idx
    diag = t_idx == s_idx
    head_cols = _head_mask(GD_HEADS, 1, rows_x, chunk).astype(BF16)
    ones_cc = jnp.ones((chunk, chunk), BF16)
    hsl = [slice(h * GD_DK, (h + 1) * GD_DK) for h in range(GD_HEADS)]

    def prepare(ci):
        def gen(done):
            rs = slice(ci * chunk, (ci + 1) * chunk)
            g_t = _dot_sel_r(gc[rs], head_cols)
            yield
            g_s = _dot_sel_l(ones_cc, jnp.where(diag, g_t, 0.0))
            qk = _dot_nt(jnp.concatenate([kb[rs], qb[rs]], axis=0), _expand(kbf[rs], lane_mask))
            yield
            decay = jnp.exp(jnp.minimum(g_t - g_s, 0.0))
            amat = jnp.where(incl, qk[chunk:] * decay, 0.0).astype(BF16)
            t_m1 = yield from _unit_lower_inverse_minus_eye(
                jnp.where(strict, qk[:chunk] * decay, 0.0), chunk, amask)
            vk = jnp.concatenate([vbeta[rs], kbg[rs]], axis=1)
            vk_x = jnp.concatenate([_expand(vbeta[rs], lane_mask), _expand(kbg[rs], lane_mask)], axis=1)
            uw = vk.astype(F32) + _dot(t_m1.astype(BF16), vk_x)
            yield
            return amat, uw
        return gen

    def advance(ci):
        def gen(done):
            rs = slice(ci * chunk, (ci + 1) * chunk)
            amat, uw = done["prepare", ci]
            glast_l = gc_l[(ci + 1) * chunk - 1:(ci + 1) * chunk]
            kg = (k[rs] * jnp.exp(glast_l - gc_l[rs])).astype(BF16)
            wb = uw[:, GD_W:].astype(BF16)
            i_seq = ci // n_chunks
            sts = [so_ref[i_seq, h] for h in range(GD_HEADS)]
            stbs = [st.astype(BF16) for st in sts]
            v_news = [uw[:, hs] - _dot(wb[:, hs], stb) for hs, stb in zip(hsl, stbs)]
            yield
            vnb = jnp.concatenate(v_news, axis=1).astype(BF16)
            for hs, st, h in zip(hsl, sts, range(GD_HEADS)):
                so_ref[i_seq, h] = st * jnp.exp(glast_l[:, hs]) + _dot_tn(kg[:, hs], vnb[:, hs])
            yield
            o_c = (jnp.concatenate([_dot(qg[rs, hs], stb) for hs, stb in zip(hsl, stbs)], axis=1)
                   + _dot(amat, _expand(vnb, lane_mask)))
            yield
            return o_c
        return gen

    tasks = []
    for ci in range(n_seq * n_chunks):
        tasks.append((("advance", ci), [("prepare", ci)] + ([("advance", ci - 1)] if ci % n_chunks else []), [],
                      advance(ci)))
        tasks.append((("prepare", ci), [], [], prepare(ci)))
    done = _run_tasks(tasks)
    o_rows = [done["advance", ci] for ci in range(n_seq * n_chunks)]
    o = o_rows[0] if len(o_rows) == 1 else jnp.concatenate(o_rows, axis=0)
    o = o * lax.rsqrt(_seg_sum(o * o, seg2) * (1.0 / GD_DV) + EPS) * ng_ref[...]
    gz = gz_ref[...].reshape(rows_all, GD_HEADS * GD_DV)
    o_ref[...] = (o * (gz * _sigmoid(gz))).reshape(n_seq, tb, GD_HEADS * GD_DV).astype(BF16)


def _gdn(gqkv, gz, sm, conv0, s0, w, layer, batch, seq, *, chunk, n_chunks, n_seq, t_valid):
    tb = chunk * n_chunks
    nc = seq // tb
    assert batch % n_seq == 0
    body = functools.partial(_gdn_body, chunk=chunk, n_chunks=n_chunks, n_seq=n_seq, t_valid=t_valid)
    lspec = lambda a: _const_spec((None,) + a.shape[1:], lambda b, c: (layer, 0, 0))
    state_spec = pl.BlockSpec((n_seq, GD_HEADS, GD_DK, GD_DV), lambda b, c: (b, 0, 0, 0))
    s0_layer = layer if s0.shape[0] > 1 else 0
    seq_blk = lambda n: pl.BlockSpec((n_seq, tb, n), lambda b, c: (b, c, 0))
    d_v = GD_HEADS * GD_DV
    o_gd, state = pl.pallas_call(
        body,
        grid=(batch // n_seq, nc),
        in_specs=[
            seq_blk(GD_QKV), seq_blk(d_v), seq_blk(N_SMALL),
            pl.BlockSpec((n_seq, SUBLANES, GD_QKV), lambda b, c: (b, 0, 0)),
            pl.BlockSpec((None, n_seq, GD_HEADS, GD_DK, GD_DV), lambda b, c: (s0_layer, b, 0, 0, 0)),
            lspec(w["conv"]), lspec(w["norm_g"]),
            _const_spec((2 * GD_W, GD_W), lambda b, c: (0, 0)),
        ],
        out_specs=(seq_blk(d_v), state_spec),
        out_shape=(
            jax.ShapeDtypeStruct((batch, seq, d_v), BF16),
            jax.ShapeDtypeStruct((batch, GD_HEADS, GD_DK, GD_DV), F32),
        ),
        scratch_shapes=[pltpu.VMEM((n_seq, SUBLANES, GD_QKV), F32)],
        compiler_params=_cparams("parallel", "arbitrary"),
        name="gdn",
    )(gqkv.reshape(batch, seq, GD_QKV), gz.reshape(batch, seq, d_v), sm.reshape(batch, seq, N_SMALL),
      conv0, s0, w["conv"], w["norm_g"], w["seg2"])
    return o_gd.reshape(batch * seq, d_v), state


def _merge_body(x_ref, of_ref, or_ref, og_ref, zg_ref, wf_ref, wr_ref, wg_ref, wo_ref, o_ref):
    d = x_ref.shape[1]
    merged = _sigmoid(zg_ref[:, 0:d]) * _dot(of_ref[...], wf_ref[...])
    merged = merged + _sigmoid(zg_ref[:, d:2 * d]) * _dot(or_ref[...], wr_ref[...])
    merged = merged + _sigmoid(zg_ref[:, 2 * d:3 * d]) * _dot(og_ref[...], wg_ref[...])
    o_ref[...] = x_ref[...] + _dot(merged.astype(BF16), wo_ref[...])


def _merge(x, o_fox, o_rw, o_gd, zg, w, layer, *, tm):
    m, d = x.shape
    row_blk = lambda n: pl.BlockSpec((tm, n), lambda i: (i, 0))
    wspec = lambda a: _const_spec((None,) + a.shape[1:], lambda i: (layer, 0, 0))
    return pl.pallas_call(
        _merge_body,
        grid=(m // tm,),
        in_specs=[row_blk(d), row_blk(FOX_W), row_blk(RW_W), row_blk(GD_HEADS * GD_DV),
                  row_blk(N_BRANCH * d),
                  wspec(w["br_fox"]), wspec(w["br_rw"]), wspec(w["br_gd"]), wspec(w["out"])],
        out_specs=row_blk(d),
        out_shape=jax.ShapeDtypeStruct((m, d), F32),
        compiler_params=_cparams("parallel"),
        name="merge",
    )(x, o_fox, o_rw, o_gd, zg, w["br_fox"], w["br_rw"], w["br_gd"], w["out"])


def _same_segment2(n, width):
    idx = jnp.arange(n) // width
    seg = (idx[:, None] == idx[None, :]).astype(BF16)
    return jnp.concatenate([seg, seg], axis=0)


def _row_tile(m):
    for tm in (512, 256, 128, 64, 32, 16, 8):
        if m % tm == 0:
            return tm
    raise ValueError(f"row count {m} is not a multiple of 8")


def _pages_per_step(n_pages):
    for pps in (16, 8, 4, 2, 1):
        if n_pages % pps == 0:
            return pps


def kernel(x_prompt, x_sample, cache_k, cache_v, cache_logf, state_rwkv_shift, state_rwkv,
           state_gdn_conv, state_gdn, page_table, norm_ffn1, ffn1_wg, ffn1_wu, ffn1_wd, norm_mix,
           w_in, fox_fb, rw_mu, rw_w0, rw_w_up, rw_a0, rw_a_up, rw_g_up, rw_kk, rw_ka, rw_rk,
           rw_ln_g, rw_ln_b, gd_conv, gd_a_log, gd_dt_bias, gd_norm_g, w_br_fox, w_br_rw, w_br_gd,
           w_out, norm_ffn2, ffn2_wg, ffn2_wu, ffn2_wd, final_norm):
    bp, seq, d = x_prompt.shape
    bd, t_s, _ = x_sample.shape
    depth = w_in.shape[0]
    tp = SUBLANES
    chunk_p = 64
    n_chunks_p = 4 if seq % (4 * chunk_p) == 0 else 1
    n_seq_p = 2 if bp % 2 == 0 else 1
    n_seq_s = 4 if bd % 4 == 0 else 1
    assert t_s <= tp and seq % chunk_p == 0

    o_rw_c = 3 * FOX_W + FOX_HEADS
    o_gd_c = o_rw_c + RW_COLS
    o_gz = o_gd_c + GD_QKV
    o_gb = o_gz + GD_HEADS * GD_DV
    o_ga = o_gb + GD_HEADS
    o_gate = o_ga + GD_HEADS
    scale = HEAD_DIM ** -0.5
    w_small = jnp.concatenate([w_in[:, :, 3 * FOX_W:o_rw_c], w_in[:, :, o_gb:o_gate]], axis=-1)
    zeros4 = jnp.zeros((depth, GD_HEADS), F32)
    small_bias = jnp.concatenate([fox_fb, zeros4, gd_dt_bias], axis=-1)
    small_alog = jnp.concatenate([jnp.zeros((depth, 8), F32), zeros4, gd_a_log], axis=-1)
    w_proj = {
        "fox": jnp.concatenate([w_in[:, :, :FOX_W] * scale, w_in[:, :, FOX_W:3 * FOX_W]], axis=-1).astype(BF16),
        "rw": w_in[:, :, o_rw_c:o_gd_c].astype(BF16),
        "gd": w_in[:, :, o_gd_c:o_gb].astype(BF16),
        "gate": w_in[:, :, o_gate:].astype(BF16),
        "small": w_small.astype(BF16),
        "kv_t": jnp.swapaxes(w_in[:, :, FOX_W:3 * FOX_W], 1, 2).astype(BF16),
        "small_pc": jnp.stack([small_bias, small_alog], axis=1),
    }
    w_rw = {
        "mu": rw_mu[:, None, :],
        "pv": jnp.stack([rw_w0, rw_a0, rw_kk, rw_ka, rw_rk.reshape(depth, RW_W), rw_ln_g, rw_ln_b,
                         jnp.zeros_like(rw_w0)], axis=1),
        "w_up": rw_w_up.astype(BF16), "a_up": rw_a_up.astype(BF16), "g_up": rw_g_up.astype(BF16),
        "seg2": _same_segment2(RW_W, HEAD_DIM),
    }
    w_gd = {
        "conv": jnp.pad(gd_conv, ((0, 0), (0, SUBLANES - GD_CONV), (0, 0))),
        "norm_g": jnp.tile(gd_norm_g, (1, GD_HEADS))[:, None, :],
        "seg2": _same_segment2(GD_W, GD_DK),
    }
    w_mg = {"br_fox": w_br_fox.astype(BF16), "br_rw": w_br_rw.astype(BF16),
            "br_gd": w_br_gd.astype(BF16), "out": w_out.astype(BF16)}
    ffn1 = (norm_ffn1[:, None, :], ffn1_wg.astype(BF16), ffn1_wu.astype(BF16), ffn1_wd.astype(BF16))
    ffn2 = (norm_ffn2[:, None, :], ffn2_wg.astype(BF16), ffn2_wu.astype(BF16), ffn2_wd.astype(BF16))
    norm_mix3 = norm_mix[:, None, :]
    final_g = final_norm[None, :]
    fox_consts = _fox_prompt_consts()

    n_pool, page = cache_k.shape[1], cache_k.shape[2]
    ckt = jnp.transpose(cache_k, (0, 1, 3, 4, 2)).reshape(depth, n_pool, FOX_W, page)
    cvt = jnp.transpose(cache_v, (0, 1, 3, 4, 2)).reshape(depth, n_pool, FOX_W, page)
    clft = jnp.swapaxes(cache_logf, 2, 3)
    pps = _pages_per_step(page_table.shape[1])
    gd_conv0_s = jnp.pad(state_gdn_conv, ((0, 0), (0, 0), (SUBLANES - (GD_CONV - 1), 0), (0, 0)))
    zeros_p = {
        "shift": jnp.zeros((bp, 1, RW_COLS), F32),
        "rw_s": jnp.zeros((1, bp, RW_HEADS, HEAD_DIM, HEAD_DIM), F32),
        "conv": jnp.zeros((bp, SUBLANES, GD_QKV), F32),
        "gd_s": jnp.zeros((1, bp, GD_HEADS, GD_DK, GD_DV), F32),
    }

    xp = x_prompt.reshape(bp * seq, d)
    xs = jnp.pad(x_sample, ((0, 0), (0, tp - t_s), (0, 0))).reshape(bd * tp, d)
    tm_p, tm_s = _row_tile(bp * seq), _row_tile(bd * tp)
    tm_in_p = min(tm_p, 256)
    tq = 512 if seq % 512 == 0 else min(256, seq)

    def layer_fn(x, l, *, batch, t_len, tm, tm_in, fox_fn, shift0, rw_s0, conv0, gd_s0, chunk, n_chunks,
                 n_seq, t_valid, last):
        x = _ffn(x, *ffn1, final_g, l, tm=tm)
        qkvb, fk, fv, kt, vt, zr, gqkv, gz, zg, sm = _inproj(x, norm_mix3, w_proj, l, tm=tm_in, seq=t_len)
        o_fox = fox_fn(qkvb, vt, sm, l)
        o_rw, rw_st = _rwkv(zr, shift0, rw_s0, w_rw, l, batch, t_len, chunk=chunk, n_chunks=n_chunks,
                            n_seq=n_seq, t_valid=t_valid)
        o_gd, gd_s = _gdn(gqkv, gz, sm, conv0, gd_s0, w_gd, l, batch, t_len, chunk=chunk, n_chunks=n_chunks,
                          n_seq=n_seq, t_valid=t_valid)
        x = _merge(x, o_fox, o_rw, o_gd, zg, w_mg, l, tm=tm)
        x = _ffn(x, *ffn2, final_g, l, tm=tm, final_norm=last)
        return x, (fk, fv, kt, vt, sm, zr, rw_st, gqkv, gd_s)

    p_states, s_states = [], []
    for l in range(depth):
        last = l == depth - 1
        xp, st = layer_fn(
            xp, l, batch=bp, t_len=seq, tm=tm_p, tm_in=tm_in_p,
            fox_fn=lambda qkvb, vt, sm, l: _fox_prompt(qkvb, vt, sm, fox_consts, bp, seq, tq=tq),
            shift0=zeros_p["shift"], rw_s0=zeros_p["rw_s"], conv0=zeros_p["conv"], gd_s0=zeros_p["gd_s"],
            chunk=chunk_p, n_chunks=n_chunks_p, n_seq=n_seq_p, t_valid=chunk_p, last=last)
        _, _, kt, vt, sm, zr, rw_st, gqkv, gd_s = st
        p_states.append((
            kt, vt,
            sm[:, :FOX_HEADS].reshape(bp, seq, FOX_HEADS),
            zr.reshape(bp, seq, RW_COLS)[:, seq - 1],
            rw_st,
            gqkv.reshape(bp, seq, GD_QKV)[:, seq - (GD_CONV - 1):],
            gd_s))
        xs, st = layer_fn(
            xs, l, batch=bd, t_len=tp, tm=tm_s, tm_in=tm_s,
            fox_fn=lambda qkvb, vt, sm, l: _fox_sample(qkvb, sm, ckt, cvt, clft, page_table, l,
                                                       t_valid=t_s, pps=pps),
            shift0=state_rwkv_shift[l][:, None, :], rw_s0=state_rwkv, conv0=gd_conv0_s[l],
            gd_s0=state_gdn, chunk=tp, n_chunks=1, n_seq=n_seq_s, t_valid=t_s, last=last)
        fk, fv, _, _, sm, zr, rw_st, gqkv, gd_s = st
        conv_ext = jnp.concatenate([state_gdn_conv[l], gqkv.reshape(bd, tp, GD_QKV)[:, :t_s]], axis=1)
        s_states.append((
            fk.reshape(bd, tp, FOX_HEADS, HEAD_DIM)[:, :t_s], fv.reshape(bd, tp, FOX_HEADS, HEAD_DIM)[:, :t_s],
            sm[:, :FOX_HEADS].reshape(bd, tp, FOX_HEADS)[:, :t_s],
            zr.reshape(bd, tp, RW_COLS)[:, t_s - 1],
            rw_st,
            conv_ext[:, t_s:],
            gd_s))

    p_out = [jnp.stack(s) for s in zip(*p_states)]
    for i in (0, 1):
        p_out[i] = jnp.transpose(p_out[i].reshape(depth, bp, FOX_HEADS, HEAD_DIM, seq), (0, 1, 4, 2, 3))
    s_out = [jnp.stack(s) for s in zip(*s_states)]
    y_prompt = xp.reshape(bp, seq, d)
    y_sample = xs.reshape(bd, tp, d)[:, :t_s]
    return (y_prompt, y_sample, *p_out, *s_out)
```

```python
import functools

import jax
import jax.numpy as jnp
from jax import lax
from jax.experimental import pallas as pl
from jax.experimental.pallas import tpu as pltpu

F32 = jnp.float32
BF16 = jnp.bfloat16

HEAD_DIM = 64
FOX_HEADS = 8
FOX_W = FOX_HEADS * HEAD_DIM
RW_HEADS = 8
RW_W = RW_HEADS * HEAD_DIM
RW_LORA_W = 64
RW_LORA_A = 64
RW_LORA_G = 128
RW_COLS = 3 * RW_W + RW_LORA_W + RW_LORA_A + RW_LORA_G
RW_GN_EPS = 64e-5
GD_HEADS = 4
GD_DK = 128
GD_DV = 128
GD_W = GD_HEADS * GD_DK
GD_QKV = 2 * GD_W + GD_HEADS * GD_DV
GD_CONV = 4
N_BRANCH = 3
EPS = 1e-6
L2_EPS = 1e-6
N_SMALL = 16
SUBLANES = 8
GROUP = 4
RW_GROUPS = RW_HEADS // GROUP
RW_GW = GROUP * HEAD_DIM
VMEM_LIMIT = 56 * 1024 * 1024


def _cparams(*sem):
    return pltpu.CompilerParams(dimension_semantics=sem, vmem_limit_bytes=VMEM_LIMIT)


def _const_spec(shape, index_map):
    return pl.BlockSpec(shape, index_map, pipeline_mode=pl.Buffered(1))


def _dot(a, b):
    return jnp.dot(a, b, preferred_element_type=F32)


def _dot_nt(a, b):
    return lax.dot_general(a, b, (((1,), (1,)), ((), ())), preferred_element_type=F32)


def _dot_tn(a, b):
    return lax.dot_general(a, b, (((0,), (0,)), ((), ())), preferred_element_type=F32)


def _split2(x):
    hi = x.astype(BF16)
    lo = (x - hi.astype(F32)).astype(BF16)
    return hi, lo


def _split3(x):
    hi = x.astype(BF16)
    r = x - hi.astype(F32)
    mid = r.astype(BF16)
    lo = (r - mid.astype(F32)).astype(BF16)
    return hi, mid, lo


def _dot_sel_r(x, sel):
    hi, mid, lo = _split3(x)
    return _dot(hi, sel) + _dot(mid, sel) + _dot(lo, sel)


def _dot_sel_l(sel, x):
    return _dot(jnp.concatenate([sel] * 3, axis=1), jnp.concatenate(_split3(x), axis=0))


def _seg_sum(x, seg2):
    return _dot(jnp.concatenate(_split2(x), axis=1), seg2)


def _softplus(z):
    return jnp.maximum(z, 0.0) + jnp.log(1.0 + jnp.exp(-jnp.abs(z)))


def _sigmoid(z):
    return 1.0 / (1.0 + jnp.exp(-z))


def _rms(x, g):
    return x * lax.rsqrt(jnp.mean(x * x, axis=-1, keepdims=True) + EPS) * g


def _tri(n, kind):
    r = lax.broadcasted_iota(jnp.int32, (n, n), 0)
    c = lax.broadcasted_iota(jnp.int32, (n, n), 1)
    return {"incl": r >= c, "strict": r > c, "upper_incl": r <= c}[kind]


def _block_tri(n, blk, strict):
    r = lax.broadcasted_iota(jnp.int32, (n, n), 0)
    c = lax.broadcasted_iota(jnp.int32, (n, n), 1)
    same = (r // blk) == (c // blk)
    return same & ((r > c) if strict else (r >= c))


def _head_mask(rows, row_blk, cols, col_blk):
    r = lax.broadcasted_iota(jnp.int32, (rows, cols), 0) // row_blk
    c = lax.broadcasted_iota(jnp.int32, (rows, cols), 1) // col_blk
    return r == c


def _expand(x, mask):
    return jnp.where(mask, jnp.concatenate([x] * GROUP, axis=0), jnp.zeros((), x.dtype))


def _unit_lower_inverse_minus_eye(m_wide, order, amask):
    width = m_wide.shape[1]
    npow = -m_wide
    t = npow
    if order <= 2:
        return t
    nb = npow.astype(BF16)
    npow = _dot(nb, _expand(nb, amask))
    yield
    span = 4
    while span <= order:
        nb = npow.astype(BF16)
        tb = _expand(t.astype(BF16), amask)
        if span < order:
            prod = _dot(nb, jnp.concatenate([tb, _expand(nb, amask)], axis=1))
            t = t + npow + prod[:, :width]
            npow = prod[:, width:]
        else:
            t = t + npow + _dot(nb, tb)
        yield
        span *= 2
    return t


def _run_tasks(tasks):
    done, active = {}, []
    pending = list(tasks)
    while pending or active:
        for task in list(pending):
            name, needs_done, fn = task
            if all(n in done for n in needs_done):
                pending.remove(task)
                active.append((name, fn(done)))
        assert active, "task dependencies can never be met"
        for entry in list(active):
            name, gen = entry
            try:
                next(gen)
            except StopIteration as stop:
                done[name] = stop.value
                active.remove(entry)
    return done


def _ffn_body(x_ref, g_ref, wg_ref, wu_ref, wd_ref, fg_ref, o_ref, *, ff_chunk, final_norm):
    x = x_ref[...]
    h = _rms(x, g_ref[...]).astype(BF16)
    d_ff = wg_ref.shape[1]
    acc = jnp.zeros_like(x)
    for c0 in range(0, d_ff, ff_chunk):
        gate = _dot(h, wg_ref[:, c0:c0 + ff_chunk])
        up = _dot(h, wu_ref[:, c0:c0 + ff_chunk])
        act = (gate * _sigmoid(gate) * up).astype(BF16)
        acc = acc + _dot(act, wd_ref[c0:c0 + ff_chunk, :])
    y = x + 0.5 * acc
    if final_norm:
        y = _rms(y, fg_ref[...])
    o_ref[...] = y


def _ffn(x, norm_g, wg, wu, wd, final_g, layer, *, tm, final_norm=False):
    m, d = x.shape
    d_ff = wg.shape[2]
    ff_chunk = d_ff // 2 if (d_ff // 2) % 128 == 0 else d_ff
    body = functools.partial(_ffn_body, ff_chunk=ff_chunk, final_norm=final_norm)
    return pl.pallas_call(
        body,
        grid=(m // tm,),
        in_specs=[
            pl.BlockSpec((tm, d), lambda i: (i, 0)),
            _const_spec((None, 1, d), lambda i: (layer, 0, 0)),
            _const_spec((None, d, d_ff), lambda i: (layer, 0, 0)),
            _const_spec((None, d, d_ff), lambda i: (layer, 0, 0)),
            _const_spec((None, d_ff, d), lambda i: (layer, 0, 0)),
            _const_spec((1, d), lambda i: (0, 0)),
        ],
        out_specs=pl.BlockSpec((tm, d), lambda i: (i, 0)),
        out_shape=jax.ShapeDtypeStruct((m, d), F32),
        compiler_params=_cparams("parallel"),
        name="ffn",
    )(x, norm_g, wg, wu, wd, final_g)


def _small_fn(z, is_logf, is_beta, neg_exp_a):
    return jnp.where(is_logf, -_softplus(-z), jnp.where(is_beta, _sigmoid(z), neg_exp_a * _softplus(z)))


def _inproj_body(x_ref, g_ref, wfox_ref, wrw_ref, wgd_ref, wgate_ref, wsm_ref, wkvt_ref, pc_ref,
                 qkvb_ref, fk_ref, fv_ref, kt_ref, vt_ref, zr_ref, gqkv_ref, gz_ref, zg_ref, sm_ref):
    h = _rms(x_ref[...], g_ref[...]).astype(BF16)
    fox = _dot(h, wfox_ref[...])
    qkvb_ref[...] = fox.astype(BF16)
    fk_ref[...] = fox[:, FOX_W:2 * FOX_W]
    fv_ref[...] = fox[:, 2 * FOX_W:]
    kvt = _dot_nt(wkvt_ref[...], h)
    kt_ref[...] = kvt[:FOX_W]
    vt_ref[...] = kvt[FOX_W:]
    zr_ref[...] = _dot(h, wrw_ref[...])
    gd = _dot(h, wgd_ref[...])
    gqkv_ref[...] = gd[:, :GD_QKV]
    gz_ref[...] = gd[:, GD_QKV:]
    zg_ref[...] = _dot(h, wgate_ref[...])
    pc = pc_ref[...]
    col = lax.broadcasted_iota(jnp.int32, (1, N_SMALL), 1)
    sm = _dot(h, wsm_ref[...]) + pc[0:1]
    sm_ref[...] = _small_fn(sm, col < 8, col < 12, -jnp.exp(pc[1:2]))


def _inproj(x, norm_g, w, layer, *, tm, seq):
    m, d = x.shape
    n_gate = w["gate"].shape[2]
    n_gd = w["gd"].shape[2]
    t_batch, t_seq = (m // seq, seq) if seq % tm == 0 else (1, m)
    tiles_per_seq = t_seq // tm
    row_blk = lambda n: pl.BlockSpec((tm, n), lambda i: (i, 0))
    t_blk = pl.BlockSpec((None, FOX_W, tm), lambda i: (i // tiles_per_seq, 0, i % tiles_per_seq))
    wspec = lambda a: _const_spec((None,) + a.shape[1:], lambda i: (layer, 0, 0))
    out_shape = (
        jax.ShapeDtypeStruct((m, 3 * FOX_W), BF16),
        jax.ShapeDtypeStruct((m, FOX_W), F32),
        jax.ShapeDtypeStruct((m, FOX_W), F32),
        jax.ShapeDtypeStruct((t_batch, FOX_W, t_seq), F32),
        jax.ShapeDtypeStruct((t_batch, FOX_W, t_seq), F32),
        jax.ShapeDtypeStruct((m, RW_COLS), F32),
        jax.ShapeDtypeStruct((m, GD_QKV), F32),
        jax.ShapeDtypeStruct((m, n_gd - GD_QKV), F32),
        jax.ShapeDtypeStruct((m, n_gate), F32),
        jax.ShapeDtypeStruct((m, N_SMALL), F32),
    )
    out_specs = (
        row_blk(3 * FOX_W), row_blk(FOX_W), row_blk(FOX_W), t_blk, t_blk, row_blk(RW_COLS), row_blk(GD_QKV),
        row_blk(n_gd - GD_QKV), row_blk(n_gate), row_blk(N_SMALL),
    )
    return pl.pallas_call(
        _inproj_body,
        grid=(m // tm,),
        in_specs=[
            pl.BlockSpec((tm, d), lambda i: (i, 0)),
            _const_spec((None, 1, d), lambda i: (layer, 0, 0)),
            wspec(w["fox"]), wspec(w["rw"]), wspec(w["gd"]), wspec(w["gate"]),
            wspec(w["small"]), wspec(w["kv_t"]), wspec(w["small_pc"]),
        ],
        out_specs=out_specs,
        out_shape=out_shape,
        compiler_params=_cparams("parallel"),
        name="inproj",
    )(x, norm_g, w["fox"], w["rw"], w["gd"], w["gate"], w["small"], w["kv_t"], w["small_pc"])


FOX_AW = 2 * HEAD_DIM
N_PIECES = 3


def _eye(n):
    return (lax.broadcasted_iota(jnp.int32, (n, n), 0) == lax.broadcasted_iota(jnp.int32, (n, n), 1)).astype(BF16)


def _fox_prompt_body(q_ref, k_ref, vt_in_ref, lf_ref, pk_ref, pc_ref, one_ref, o_ref,
                     kaug_ref, vt_ref, qaug_ref, m_ref, l_ref, acc_ref, *, tq, cb):
    i = pl.program_id(1)
    seq = k_ref.shape[0]

    @pl.when(i == 0)
    def _():
        lower = _tri(cb, "incl").astype(BF16)
        carry = jnp.zeros((1, N_SMALL), F32)
        for b0 in range(0, seq, cb):
            cc = _dot_sel_l(lower, lf_ref[b0:b0 + cb, :]) + carry
            carry = cc[cb - 1:cb, :]
            pieces = jnp.concatenate(_split3(-cc), axis=1)
            kaug_ref[b0:b0 + cb, :] = (_dot(k_ref[b0:b0 + cb, :], pk_ref[...])
                                       + _dot(pieces, pc_ref[...])).astype(BF16)
            vt_ref[:, b0:b0 + cb] = vt_in_ref[:, b0:b0 + cb].astype(BF16)

    qaug_ref[...] = (_dot(q_ref[...], pk_ref[...]) + one_ref[...]).astype(BF16)
    m_ref[...] = jnp.full(m_ref.shape, -jnp.inf, F32)
    l_ref[...] = jnp.zeros(l_ref.shape, F32)
    acc_ref[...] = jnp.zeros(acc_ref.shape, F32)

    def tile(j, diagonal):
        k0 = pl.multiple_of(j * tq, tq)
        if diagonal:
            visible = _tri(tq, "upper_incl")
        scores = [_dot_nt(kaug_ref[pl.ds(k0, tq), h * FOX_AW:(h + 1) * FOX_AW],
                          qaug_ref[:, h * FOX_AW:(h + 1) * FOX_AW]) for h in range(FOX_HEADS)]
        for h in range(FOX_HEADS):
            hs = slice(h * HEAD_DIM, (h + 1) * HEAD_DIM)
            st = scores[h]
            if diagonal:
                st = jnp.where(visible, st, -jnp.inf)
            m_old = m_ref[h]
            m_new = jnp.maximum(m_old, jnp.max(st, axis=0, keepdims=True))
            alpha = jnp.exp(m_old - m_new)
            p = jnp.exp(st - m_new)
            m_ref[h] = m_new
            l_ref[h] = alpha * l_ref[h] + jnp.sum(p, axis=0, keepdims=True)
            acc_ref[hs, :] = alpha * acc_ref[hs, :] + _dot(vt_ref[hs, pl.ds(k0, tq)], p.astype(BF16))

    def body(j, carry):
        tile(j, False)
        return carry

    lax.fori_loop(0, i, body, 0)
    tile(i, True)
    out_t = jnp.concatenate(
        [acc_ref[h * HEAD_DIM:(h + 1) * HEAD_DIM, :] / l_ref[h] for h in range(FOX_HEADS)], axis=0)
    o_ref[...] = _dot_nt(_eye(tq), out_t.astype(BF16)).astype(BF16)


def _fox_prompt(qkvb, vt, sm, consts, batch, seq, *, tq):
    m = batch * seq
    nq = seq // tq
    cb = min(512, seq)
    body = functools.partial(_fox_prompt_body, tq=tq, cb=cb)
    aug_w = FOX_HEADS * FOX_AW
    return pl.pallas_call(
        body,
        grid=(batch, nq),
        in_specs=[
            pl.BlockSpec((tq, FOX_W), lambda b, i: (b * nq + i, 0)),
            pl.BlockSpec((seq, FOX_W), lambda b, i: (b, 1)),
            pl.BlockSpec((None, FOX_W, seq), lambda b, i: (b, 0, 0)),
            pl.BlockSpec((seq, N_SMALL), lambda b, i: (b, 0)),
            _const_spec((FOX_W, aug_w), lambda b, i: (0, 0)),
            _const_spec((N_PIECES * N_SMALL, aug_w), lambda b, i: (0, 0)),
            _const_spec((1, aug_w), lambda b, i: (0, 0)),
        ],
        out_specs=pl.BlockSpec((tq, FOX_W), lambda b, i: (b * nq + i, 0)),
        out_shape=jax.ShapeDtypeStruct((m, FOX_W), BF16),
        scratch_shapes=[pltpu.VMEM((seq, aug_w), BF16), pltpu.VMEM((FOX_W, seq), BF16),
                        pltpu.VMEM((tq, aug_w), BF16),
                        pltpu.VMEM((FOX_HEADS, 1, tq), F32), pltpu.VMEM((FOX_HEADS, 1, tq), F32),
                        pltpu.VMEM((FOX_W, tq), F32)],
        compiler_params=_cparams("parallel", "arbitrary"),
        name="fox_prompt",
    )(qkvb, qkvb, vt, sm, consts["place_k"], consts["place_c"], consts["ones"])


def _fox_prompt_consts():
    aug_w = FOX_HEADS * FOX_AW
    lane = jnp.arange(aug_w)
    src = jnp.arange(FOX_W)
    place_k = (lane[None, :] == (src // HEAD_DIM * FOX_AW + src % HEAD_DIM)[:, None]).astype(BF16)
    col = jnp.arange(N_PIECES * N_SMALL)
    tgt = jnp.where(col % N_SMALL < FOX_HEADS, (col % N_SMALL) * FOX_AW + HEAD_DIM + col // N_SMALL, -1)
    place_c = (lane[None, :] == tgt[:, None]).astype(BF16)
    ones = ((lane % FOX_AW >= HEAD_DIM) & (lane % FOX_AW < HEAD_DIM + N_PIECES)).astype(F32)[None, :]
    return {"place_k": place_k, "place_c": place_c, "ones": ones}


def _fox_sample_body(pt_ref, q_ref, kn_ref, vn_ref, lfn_ref, *rest, t_valid, pps):
    kp_refs = rest[0:pps]
    vp_refs = rest[pps:2 * pps]
    lf_refs = rest[2 * pps:3 * pps]
    o_ref, qbd_ref, m_ref, l_ref, acc_ref, tail_ref = rest[3 * pps:]
    s_idx = pl.program_id(1)
    tp = SUBLANES
    nrow = t_valid * FOX_HEADS
    page = kp_refs[0].shape[1]
    diag = (lax.broadcasted_iota(jnp.int32, (nrow, FOX_W), 0) % FOX_HEADS
            == lax.broadcasted_iota(jnp.int32, (nrow, FOX_W), 1) // HEAD_DIM)

    def expand_rows(x):
        return jnp.concatenate([x] * t_valid, axis=0)

    @pl.when(s_idx == 0)
    def _():
        qf = q_ref[...].astype(F32)
        q_rows = jnp.concatenate([jnp.broadcast_to(qf[t:t + 1], (FOX_HEADS, FOX_W)) for t in range(t_valid)],
                                 axis=0)
        qbd = jnp.where(diag, q_rows, 0.0).astype(BF16)
        qbd_ref[...] = qbd
        upper = _tri(tp, "upper_incl").astype(BF16)
        lf_hi, lf_mid, lf_lo = _split3(lfn_ref[...])
        cn = (_dot_tn(lf_hi, upper) + _dot_tn(lf_mid, upper) + _dot_tn(lf_lo, upper))[:FOX_HEADS]
        s = _dot_nt(qbd, kn_ref[...]) - expand_rows(cn)
        t_q = lax.broadcasted_iota(jnp.int32, (nrow, tp), 0) // FOX_HEADS
        u_k = lax.broadcasted_iota(jnp.int32, (nrow, tp), 1)
        s = jnp.where((u_k <= t_q) & (u_k < t_valid), s, -jnp.inf)
        m = jnp.max(s, axis=-1, keepdims=True)
        p = jnp.exp(s - m)
        m_ref[...] = m
        l_ref[...] = jnp.sum(p, axis=-1, keepdims=True)
        acc_ref[...] = _dot(p.astype(BF16), vn_ref[...])
        tail_ref[...] = jnp.zeros_like(tail_ref)

    @pl.when(s_idx > 0)
    def _():
        later3 = jnp.concatenate([_tri(page, "strict").astype(BF16)] * 3, axis=0)
        qbd = qbd_ref[...]
        tail = tail_ref[...]
        scores = [None] * pps
        for r in reversed(range(pps)):
            lf = lf_refs[r][...]
            suf = _dot(jnp.concatenate(_split3(lf), axis=1), later3) + tail
            tail = tail + jnp.sum(lf, axis=-1, keepdims=True)
            scores[r] = _dot(qbd, kp_refs[r][...].astype(BF16)) + expand_rows(suf)
        tail_ref[...] = tail
        m_old = m_ref[...]
        m_new = m_old
        for r in range(pps):
            m_new = jnp.maximum(m_new, jnp.max(scores[r], axis=-1, keepdims=True))
        alpha = jnp.exp(m_old - m_new)
        l_new = alpha * l_ref[...]
        acc = alpha * acc_ref[...]
        for r in range(pps):
            p = jnp.exp(scores[r] - m_new)
            l_new = l_new + jnp.sum(p, axis=-1, keepdims=True)
            acc = acc + _dot_nt(p.astype(BF16), vp_refs[r][...].astype(BF16))
        m_ref[...] = m_new
        l_ref[...] = l_new
        acc_ref[...] = acc

    @pl.when(s_idx == pl.num_programs(1) - 1)
    def _():
        o = jnp.where(diag, acc_ref[...] / l_ref[...], 0.0).astype(BF16)
        pick = (lax.broadcasted_iota(jnp.int32, (tp, nrow), 1) // FOX_HEADS
                == lax.broadcasted_iota(jnp.int32, (tp, nrow), 0)).astype(BF16)
        o_ref[...] = _dot(pick, o).astype(BF16)


def _fox_sample(qkvb, sm, cache_kt, cache_vt, cache_lft, page_table, layer, *, t_valid, pps):
    tp = SUBLANES
    bd, n_pages = page_table.shape
    page = cache_kt.shape[3]
    nrow = t_valid * FOX_HEADS
    assert n_pages % pps == 0

    def page_spec(r, blk):
        def index_map(b, s, pt):
            return (layer, pt[b, n_pages - jnp.maximum(s, 1) * pps + r]) + (0,) * (len(blk) - 2)
        return pl.BlockSpec(blk, index_map)

    kv_blk = (None, None, FOX_W, page)
    lf_blk = (None, None, FOX_HEADS, page)
    grid_spec = pltpu.PrefetchScalarGridSpec(
        num_scalar_prefetch=1,
        grid=(bd, n_pages // pps + 1),
        in_specs=[
            pl.BlockSpec((tp, FOX_W), lambda b, s, pt: (b, 0)),
            pl.BlockSpec((tp, FOX_W), lambda b, s, pt: (b, 1)),
            pl.BlockSpec((tp, FOX_W), lambda b, s, pt: (b, 2)),
            pl.BlockSpec((tp, N_SMALL), lambda b, s, pt: (b, 0)),
            *[page_spec(r, kv_blk) for r in range(pps)],
            *[page_spec(r, kv_blk) for r in range(pps)],
            *[page_spec(r, lf_blk) for r in range(pps)],
        ],
        out_specs=pl.BlockSpec((tp, FOX_W), lambda b, s, pt: (b, 0)),
        scratch_shapes=[
            pltpu.VMEM((nrow, FOX_W), BF16),
            pltpu.VMEM((nrow, 1), F32),
            pltpu.VMEM((nrow, 1), F32),
            pltpu.VMEM((nrow, FOX_W), F32),
            pltpu.VMEM((FOX_HEADS, 1), F32),
        ],
    )
    body = functools.partial(_fox_sample_body, t_valid=t_valid, pps=pps)
    return pl.pallas_call(
        body,
        grid_spec=grid_spec,
        out_shape=jax.ShapeDtypeStruct((bd * tp, FOX_W), BF16),
        compiler_params=_cparams("parallel", "arbitrary"),
        name="fox_sample",
    )(page_table, qkvb, qkvb, qkvb, sm, *([cache_kt] * pps), *([cache_vt] * pps), *([cache_lft] * pps))


def _rwkv_body(z_ref, shift_ref, s0_ref, mu_ref, pv_ref, wup_ref, aup_ref, gup_ref, seg_ref,
               o_ref, so_ref, sbd_ref, prev_ref, *, chunk, n_chunks, n_seq, t_valid):
    c = pl.program_id(1)
    tb = chunk * n_chunks
    rows_all = n_seq * tb
    rows_x = GROUP * chunk
    lane_mask = _head_mask(rows_x, chunk, RW_GW, HEAD_DIM)
    amask = _head_mask(rows_x, chunk, rows_x, chunk)
    state_mask = _head_mask(RW_GW, HEAD_DIM, RW_GW, HEAD_DIM)

    @pl.when(c == 0)
    def _():
        for i in range(n_seq):
            prev_ref[i, 0:1, :] = shift_ref[i]
            for g in range(RW_GROUPS):
                blocks = [jnp.concatenate([s0_ref[i, g * GROUP + h]] * GROUP, axis=1) for h in range(GROUP)]
                sbd_ref[i * RW_GROUPS + g] = jnp.where(state_mask, jnp.concatenate(blocks, axis=0), 0.0)

    z = z_ref[...].reshape(rows_all, RW_COLS)
    row = lax.broadcasted_iota(jnp.int32, (rows_all, 1), 0)
    z_prev = pltpu.roll(z, 1, axis=0)
    for i in range(n_seq):
        z_prev = jnp.where(row == i * tb, prev_ref[i, 0:1, :], z_prev)
        prev_ref[i, 0:1, :] = z[(i + 1) * tb - 1:(i + 1) * tb, :]
    zs = z + (z_prev - z) * mu_ref[...]
    r = zs[:, 0:RW_W]
    k = zs[:, RW_W:2 * RW_W]
    v = zs[:, 2 * RW_W:3 * RW_W]
    o_l = 3 * RW_W
    wd = zs[:, o_l:o_l + RW_LORA_W]
    ad = zs[:, o_l + RW_LORA_W:o_l + RW_LORA_W + RW_LORA_A]
    gd = zs[:, o_l + RW_LORA_W + RW_LORA_A:]
    pv = pv_ref[...]
    w_log = -_softplus(-(pv[0:1] + _dot(jnp.tanh(wd).astype(BF16), wup_ref[...]))) - 0.5
    lw = -jnp.exp(w_log)
    a = _sigmoid(pv[1:2] + _dot(ad.astype(BF16), aup_ref[...]))
    g_out = _dot(_sigmoid(gd).astype(BF16), gup_ref[...])
    kk = k * pv[2:3]
    k = k * (1.0 + (a - 1.0) * pv[3:4])
    seg2 = seg_ref[...]
    kk = kk * lax.rsqrt(_seg_sum(kk * kk, seg2) + L2_EPS)
    if t_valid < chunk:
        valid = row % tb < t_valid
        lw = jnp.where(valid, lw, 0.0)
        kk = jnp.where(valid, kk, 0.0)
        k = jnp.where(valid, k, 0.0)
    cs = _dot_sel_l(_block_tri(rows_all, chunk, strict=False).astype(BF16), lw)
    p_inv = jnp.exp(-cs)
    rp = (r * jnp.exp(cs)).astype(BF16)
    kkp = (kk * jnp.exp(cs - lw)).astype(BF16)
    bn = kk * a * p_inv
    kn = k * p_inv
    vb = v.astype(BF16)
    t_idx = lax.broadcasted_iota(jnp.int32, (chunk, 2 * rows_x), 0)
    j_idx = lax.broadcasted_iota(jnp.int32, (chunk, 2 * rows_x), 1) % chunk
    strict = (t_idx > j_idx)[:, :rows_x]
    incl2 = t_idx >= j_idx
    groups = range(RW_GROUPS)
    gsl = [slice(g * RW_GW, (g + 1) * RW_GW) for g in groups]
    bnb = bn.astype(BF16)
    knb = kn.astype(BF16)

    def prepare(ci, g):
        def gen(done):
            rs = slice(ci * chunk, (ci + 1) * chunk)
            bk_x = jnp.concatenate([_expand(bnb[rs, gsl[g]], lane_mask), _expand(knb[rs, gsl[g]], lane_mask)],
                                   axis=0)
            a_all = _dot_nt(jnp.concatenate([kkp[rs, gsl[g]], rp[rs, gsl[g]]], axis=0), bk_x)
            yield
            a_ak = jnp.where(strict, a_all[:chunk, rows_x:], 0.0).astype(BF16)
            a_r = jnp.where(incl2, a_all[chunk:, :], 0.0).astype(BF16)
            t_m1 = yield from _unit_lower_inverse_minus_eye(
                jnp.where(strict, a_all[:chunk, :rows_x], 0.0), chunk, amask)
            return a_ak, a_r, t_m1.astype(BF16)
        return gen

    def advance(ci):
        def gen(done):
            rs = slice(ci * chunk, (ci + 1) * chunk)
            pre = [done["prepare", ci, g] for g in groups]
            p_end = jnp.exp(cs[(ci + 1) * chunk - 1:(ci + 1) * chunk, :])
            bk_e = jnp.concatenate([(bn[rs] * p_end).astype(BF16), (kn[rs] * p_end).astype(BF16)], axis=0)
            v_xs = [_expand(vb[rs, gsl[g]], lane_mask) for g in groups]
            s_base = ci // n_chunks * RW_GROUPS
            sts = [sbd_ref[s_base + g] for g in groups]
            stbs = [st.astype(BF16) for st in sts]
            rhs = [_dot_nt(kkp[rs, gsl[g]], stbs[g]) + _dot(pre[g][0], v_xs[g]) for g in groups]
            yield
            sabs = [(-(rhs[g] + _dot(pre[g][2], _expand(rhs[g].astype(BF16), lane_mask)))).astype(BF16)
                    for g in groups]
            yield
            y_c = jnp.concatenate(
                [_dot_nt(rp[rs, gsl[g]], stbs[g])
                 + _dot(pre[g][1], jnp.concatenate([_expand(sabs[g], lane_mask), v_xs[g]], axis=0))
                 for g in groups], axis=1)
            yield
            for g in groups:
                outer = _dot_tn(jnp.concatenate([sabs[g], vb[rs, gsl[g]]], axis=0), bk_e[:, gsl[g]])
                sbd_ref[s_base + g] = sts[g] * p_end[:, gsl[g]] + jnp.where(state_mask, outer, 0.0)
            yield
            return y_c
        return gen

    tasks = []
    for ci in range(n_seq * n_chunks):
        tasks.append((("advance", ci),
                      [("prepare", ci, g) for g in groups] + ([("advance", ci - 1)] if ci % n_chunks else []),
                      advance(ci)))
        for g in groups:
            tasks.append((("prepare", ci, g), [], prepare(ci, g)))
    done = _run_tasks(tasks)
    y_rows = [done["advance", ci] for ci in range(n_seq * n_chunks)]
    y = y_rows[0] if len(y_rows) == 1 else jnp.concatenate(y_rows, axis=0)
    inv_n = 1.0 / HEAD_DIM
    mean = _seg_sum(y, seg2) * inv_n
    yc = y - mean
    var = _seg_sum(yc * yc, seg2) * inv_n
    yn = yc * lax.rsqrt(var + RW_GN_EPS) * pv[5:6] + pv[6:7]
    bonus = _seg_sum(r * k * pv[4:5], seg2) * v
    o_ref[...] = ((yn + bonus) * g_out).reshape(n_seq, tb, RW_W).astype(BF16)

    @pl.when(c == pl.num_programs(1) - 1)
    def _():
        for i in range(n_seq):
            for g in range(RW_GROUPS):
                st = sbd_ref[i * RW_GROUPS + g]
                for h in range(GROUP):
                    hs = slice(h * HEAD_DIM, (h + 1) * HEAD_DIM)
                    so_ref[i, g * GROUP + h] = st[hs, hs]


def _rwkv(zr, shift0, s0, w, layer, batch, seq, *, chunk, n_chunks, n_seq, t_valid):
    tb = chunk * n_chunks
    nc = seq // tb
    assert batch % n_seq == 0
    body = functools.partial(_rwkv_body, chunk=chunk, n_chunks=n_chunks, n_seq=n_seq, t_valid=t_valid)
    lspec = lambda a: _const_spec((None,) + a.shape[1:], lambda b, c: (layer, 0, 0))
    state_spec = pl.BlockSpec((n_seq, RW_HEADS, HEAD_DIM, HEAD_DIM), lambda b, c: (b, 0, 0, 0))
    s0_layer = layer if s0.shape[0] > 1 else 0
    o_rw, state = pl.pallas_call(
        body,
        grid=(batch // n_seq, nc),
        in_specs=[
            pl.BlockSpec((n_seq, tb, RW_COLS), lambda b, c: (b, c, 0)),
            pl.BlockSpec((n_seq, 1, RW_COLS), lambda b, c: (b, 0, 0)),
            pl.BlockSpec((None, n_seq, RW_HEADS, HEAD_DIM, HEAD_DIM), lambda b, c: (s0_layer, b, 0, 0, 0)),
            lspec(w["mu"]), lspec(w["pv"]), lspec(w["w_up"]), lspec(w["a_up"]), lspec(w["g_up"]),
            _const_spec((2 * RW_W, RW_W), lambda b, c: (0, 0)),
        ],
        out_specs=(pl.BlockSpec((n_seq, tb, RW_W), lambda b, c: (b, c, 0)), state_spec),
        out_shape=(
            jax.ShapeDtypeStruct((batch, seq, RW_W), BF16),
            jax.ShapeDtypeStruct((batch, RW_HEADS, HEAD_DIM, HEAD_DIM), F32),
        ),
        scratch_shapes=[pltpu.VMEM((n_seq * RW_GROUPS, RW_GW, RW_GW), F32),
                        pltpu.VMEM((n_seq, SUBLANES, RW_COLS), F32)],
        compiler_params=_cparams("parallel", "arbitrary"),
        name="rwkv",
    )(zr.reshape(batch, seq, RW_COLS), shift0, s0, w["mu"], w["pv"], w["w_up"], w["a_up"], w["g_up"], w["seg2"])
    return o_rw.reshape(batch * seq, RW_W), state


def _gdn_body(x_ref, gz_ref, sm_ref, conv0_ref, s0_ref, cw_ref, ng_ref, seg_ref,
              o_ref, so_ref, prev_ref, *, chunk, n_chunks, n_seq, t_valid):
    c = pl.program_id(1)
    tb = chunk * n_chunks
    rows_all = n_seq * tb
    rows_x = GD_HEADS * chunk
    lane_mask = _head_mask(rows_x, chunk, GD_W, GD_DK)
    amask = _head_mask(rows_x, chunk, rows_x, chunk)

    @pl.when(c == 0)
    def _():
        prev_ref[...] = conv0_ref[...]
        so_ref[...] = s0_ref[...]

    x = x_ref[...].reshape(rows_all, GD_QKV)
    prevs = [prev_ref[i] for i in range(n_seq)]
    for i in range(n_seq):
        prev_ref[i] = x[(i + 1) * tb - SUBLANES:(i + 1) * tb, :]
    cw = cw_ref[...]
    row8 = lax.broadcasted_iota(jnp.int32, (SUBLANES, 1), 0)
    conv = x * cw[GD_CONV - 1:GD_CONV]
    for sft in range(1, GD_CONV):
        xr = pltpu.roll(x, sft, axis=0)
        parts = []
        for i in range(n_seq):
            parts.append(jnp.where(row8 < sft, pltpu.roll(prevs[i], sft, axis=0), xr[i * tb:i * tb + SUBLANES]))
            if tb > SUBLANES:
                parts.append(xr[i * tb + SUBLANES:(i + 1) * tb])
        xs = parts[0] if len(parts) == 1 else jnp.concatenate(parts, axis=0)
        conv = conv + xs * cw[GD_CONV - 1 - sft:GD_CONV - sft]
    act = conv * _sigmoid(conv)
    seg2 = seg_ref[...]
    q = act[:, :GD_W]
    k = act[:, GD_W:2 * GD_W]
    v = act[:, 2 * GD_W:]
    q = q * lax.rsqrt(_seg_sum(q * q, seg2) + L2_EPS) * (GD_DK ** -0.5)
    k = k * lax.rsqrt(_seg_sum(k * k, seg2) + L2_EPS)
    sm = sm_ref[...].reshape(rows_all, N_SMALL)
    beta = sm[:, 8:8 + GD_HEADS]
    g = sm[:, 12:12 + GD_HEADS]
    if t_valid < chunk:
        rowc = lax.broadcasted_iota(jnp.int32, (rows_all, 1), 0) % tb
        beta = jnp.where(rowc < t_valid, beta, 0.0)
        g = jnp.where(rowc < t_valid, g, 0.0)
    gc = _dot_sel_l(_block_tri(rows_all, chunk, strict=False).astype(BF16), g)
    head_lanes = _head_mask(GD_HEADS, 1, GD_W, GD_DK).astype(BF16)
    beta_l = _dot_sel_r(beta, head_lanes)
    gc_l = _dot_sel_r(gc, head_lanes)
    egc_l = jnp.exp(gc_l)
    kb = k * beta_l
    vbeta = (v * beta_l).astype(BF16)
    kbg = (kb * egc_l).astype(BF16)
    qg = (q * egc_l).astype(BF16)
    kb = kb.astype(BF16)
    qb = q.astype(BF16)
    kbf = k.astype(BF16)
    t_idx = lax.broadcasted_iota(jnp.int32, (chunk, rows_x), 0)
    s_idx = lax.broadcasted_iota(jnp.int32, (chunk, rows_x), 1) % chunk
    strict = t_idx > s_idx
    incl = t_idx >= s_idx
    diag = t_idx == s_idx
    head_cols = _head_mask(GD_HEADS, 1, rows_x, chunk).astype(BF16)
    ones_cc = jnp.ones((chunk, chunk), BF16)
    hsl = [slice(h * GD_DK, (h + 1) * GD_DK) for h in range(GD_HEADS)]

    def prepare(ci):
        def gen(done):
            rs = slice(ci * chunk, (ci + 1) * chunk)
            g_t = _dot_sel_r(gc[rs], head_cols)
            yield
            g_s = _dot_sel_l(ones_cc, jnp.where(diag, g_t, 0.0))
            qk = _dot_nt(jnp.concatenate([kb[rs], qb[rs]], axis=0), _expand(kbf[rs], lane_mask))
            yield
            decay = jnp.exp(jnp.minimum(g_t - g_s, 0.0))
            amat = jnp.where(incl, qk[chunk:] * decay, 0.0).astype(BF16)
            t_m1 = yield from _unit_lower_inverse_minus_eye(
                jnp.where(strict, qk[:chunk] * decay, 0.0), chunk, amask)
            vk = jnp.concatenate([vbeta[rs], kbg[rs]], axis=1)
            vk_x = jnp.concatenate([_expand(vbeta[rs], lane_mask), _expand(kbg[rs], lane_mask)], axis=1)
            uw = vk.astype(F32) + _dot(t_m1.astype(BF16), vk_x)
            yield
            return amat, uw
        return gen

    def advance(ci):
        def gen(done):
            rs = slice(ci * chunk, (ci + 1) * chunk)
            amat, uw = done["prepare", ci]
            glast_l = gc_l[(ci + 1) * chunk - 1:(ci + 1) * chunk]
            kg = (k[rs] * jnp.exp(glast_l - gc_l[rs])).astype(BF16)
            wb = uw[:, GD_W:].astype(BF16)
            i_seq = ci // n_chunks
            sts = [so_ref[i_seq, h] for h in range(GD_HEADS)]
            stbs = [st.astype(BF16) for st in sts]
            v_news = [uw[:, hs] - _dot(wb[:, hs], stb) for hs, stb in zip(hsl, stbs)]
            yield
            vnb = jnp.concatenate(v_news, axis=1).astype(BF16)
            for hs, st, h in zip(hsl, sts, range(GD_HEADS)):
                so_ref[i_seq, h] = st * jnp.exp(glast_l[:, hs]) + _dot_tn(kg[:, hs], vnb[:, hs])
            yield
            o_c = (jnp.concatenate([_dot(qg[rs, hs], stb) for hs, stb in zip(hsl, stbs)], axis=1)
                   + _dot(amat, _expand(vnb, lane_mask)))
            yield
            return o_c
        return gen

    tasks = []
    for ci in range(n_seq * n_chunks):
        tasks.append((("advance", ci), [("prepare", ci)] + ([("advance", ci - 1)] if ci % n_chunks else []),
                      advance(ci)))
        tasks.append((("prepare", ci), [], prepare(ci)))
    done = _run_tasks(tasks)
    o_rows = [done["advance", ci] for ci in range(n_seq * n_chunks)]
    o = o_rows[0] if len(o_rows) == 1 else jnp.concatenate(o_rows, axis=0)
    o = o * lax.rsqrt(_seg_sum(o * o, seg2) * (1.0 / GD_DV) + EPS) * ng_ref[...]
    gz = gz_ref[...].reshape(rows_all, GD_HEADS * GD_DV)
    o_ref[...] = (o * (gz * _sigmoid(gz))).reshape(n_seq, tb, GD_HEADS * GD_DV).astype(BF16)


def _gdn(gqkv, gz, sm, conv0, s0, w, layer, batch, seq, *, chunk, n_chunks, n_seq, t_valid):
    tb = chunk * n_chunks
    nc = seq // tb
    assert batch % n_seq == 0
    body = functools.partial(_gdn_body, chunk=chunk, n_chunks=n_chunks, n_seq=n_seq, t_valid=t_valid)
    lspec = lambda a: _const_spec((None,) + a.shape[1:], lambda b, c: (layer, 0, 0))
    state_spec = pl.BlockSpec((n_seq, GD_HEADS, GD_DK, GD_DV), lambda b, c: (b, 0, 0, 0))
    s0_layer = layer if s0.shape[0] > 1 else 0
    seq_blk = lambda n: pl.BlockSpec((n_seq, tb, n), lambda b, c: (b, c, 0))
    d_v = GD_HEADS * GD_DV
    o_gd, state = pl.pallas_call(
        body,
        grid=(batch // n_seq, nc),
        in_specs=[
            seq_blk(GD_QKV), seq_blk(d_v), seq_blk(N_SMALL),
            pl.BlockSpec((n_seq, SUBLANES, GD_QKV), lambda b, c: (b, 0, 0)),
            pl.BlockSpec((None, n_seq, GD_HEADS, GD_DK, GD_DV), lambda b, c: (s0_layer, b, 0, 0, 0)),
            lspec(w["conv"]), lspec(w["norm_g"]),
            _const_spec((2 * GD_W, GD_W), lambda b, c: (0, 0)),
        ],
        out_specs=(seq_blk(d_v), state_spec),
        out_shape=(
            jax.ShapeDtypeStruct((batch, seq, d_v), BF16),
            jax.ShapeDtypeStruct((batch, GD_HEADS, GD_DK, GD_DV), F32),
        ),
        scratch_shapes=[pltpu.VMEM((n_seq, SUBLANES, GD_QKV), F32)],
        compiler_params=_cparams("parallel", "arbitrary"),
        name="gdn",
    )(gqkv.reshape(batch, seq, GD_QKV), gz.reshape(batch, seq, d_v), sm.reshape(batch, seq, N_SMALL),
      conv0, s0, w["conv"], w["norm_g"], w["seg2"])
    return o_gd.reshape(batch * seq, d_v), state


def _merge_body(x_ref, of_ref, or_ref, og_ref, zg_ref, wf_ref, wr_ref, wg_ref, wo_ref, o_ref):
    d = x_ref.shape[1]
    merged = _sigmoid(zg_ref[:, 0:d]) * _dot(of_ref[...], wf_ref[...])
    merged = merged + _sigmoid(zg_ref[:, d:2 * d]) * _dot(or_ref[...], wr_ref[...])
    merged = merged + _sigmoid(zg_ref[:, 2 * d:3 * d]) * _dot(og_ref[...], wg_ref[...])
    o_ref[...] = x_ref[...] + _dot(merged.astype(BF16), wo_ref[...])


def _merge(x, o_fox, o_rw, o_gd, zg, w, layer, *, tm):
    m, d = x.shape
    row_blk = lambda n: pl.BlockSpec((tm, n), lambda i: (i, 0))
    wspec = lambda a: _const_spec((None,) + a.shape[1:], lambda i: (layer, 0, 0))
    return pl.pallas_call(
        _merge_body,
        grid=(m // tm,),
        in_specs=[row_blk(d), row_blk(FOX_W), row_blk(RW_W), row_blk(GD_HEADS * GD_DV),
                  row_blk(N_BRANCH * d),
                  wspec(w["br_fox"]), wspec(w["br_rw"]), wspec(w["br_gd"]), wspec(w["out"])],
        out_specs=row_blk(d),
        out_shape=jax.ShapeDtypeStruct((m, d), F32),
        compiler_params=_cparams("parallel"),
        name="merge",
    )(x, o_fox, o_rw, o_gd, zg, w["br_fox"], w["br_rw"], w["br_gd"], w["out"])


def _same_segment2(n, width):
    idx = jnp.arange(n) // width
    seg = (idx[:, None] == idx[None, :]).astype(BF16)
    return jnp.concatenate([seg, seg], axis=0)


def _row_tile(m):
    for tm in (512, 256, 128, 64, 32, 16, 8):
        if m % tm == 0:
            return tm
    raise ValueError(f"row count {m} is not a multiple of 8")


def _pages_per_step(n_pages):
    for pps in (16, 8, 4, 2, 1):
        if n_pages % pps == 0:
            return pps


def kernel(x_prompt, x_sample, cache_k, cache_v, cache_logf, state_rwkv_shift, state_rwkv,
           state_gdn_conv, state_gdn, page_table, norm_ffn1, ffn1_wg, ffn1_wu, ffn1_wd, norm_mix,
           w_in, fox_fb, rw_mu, rw_w0, rw_w_up, rw_a0, rw_a_up, rw_g_up, rw_kk, rw_ka, rw_rk,
           rw_ln_g, rw_ln_b, gd_conv, gd_a_log, gd_dt_bias, gd_norm_g, w_br_fox, w_br_rw, w_br_gd,
           w_out, norm_ffn2, ffn2_wg, ffn2_wu, ffn2_wd, final_norm):
    bp, seq, d = x_prompt.shape
    bd, t_s, _ = x_sample.shape
    depth = w_in.shape[0]
    tp = SUBLANES
    chunk_p = 64
    n_chunks_p = 4 if seq % (4 * chunk_p) == 0 else 1
    n_seq_p = 2 if bp % 2 == 0 else 1
    n_seq_s = 8 if bd % 8 == 0 else 1
    assert t_s <= tp and seq % chunk_p == 0

    o_rw_c = 3 * FOX_W + FOX_HEADS
    o_gd_c = o_rw_c + RW_COLS
    o_gz = o_gd_c + GD_QKV
    o_gb = o_gz + GD_HEADS * GD_DV
    o_ga = o_gb + GD_HEADS
    o_gate = o_ga + GD_HEADS
    scale = HEAD_DIM ** -0.5
    w_small = jnp.concatenate([w_in[:, :, 3 * FOX_W:o_rw_c], w_in[:, :, o_gb:o_gate]], axis=-1)
    zeros4 = jnp.zeros((depth, GD_HEADS), F32)
    small_bias = jnp.concatenate([fox_fb, zeros4, gd_dt_bias], axis=-1)
    small_alog = jnp.concatenate([jnp.zeros((depth, 8), F32), zeros4, gd_a_log], axis=-1)
    w_proj = {
        "fox": jnp.concatenate([w_in[:, :, :FOX_W] * scale, w_in[:, :, FOX_W:3 * FOX_W]], axis=-1).astype(BF16),
        "rw": w_in[:, :, o_rw_c:o_gd_c].astype(BF16),
        "gd": w_in[:, :, o_gd_c:o_gb].astype(BF16),
        "gate": w_in[:, :, o_gate:].astype(BF16),
        "small": w_small.astype(BF16),
        "kv_t": jnp.swapaxes(w_in[:, :, FOX_W:3 * FOX_W], 1, 2).astype(BF16),
        "small_pc": jnp.stack([small_bias, small_alog], axis=1),
    }
    w_rw = {
        "mu": rw_mu[:, None, :],
        "pv": jnp.stack([rw_w0, rw_a0, rw_kk, rw_ka, rw_rk.reshape(depth, RW_W), rw_ln_g, rw_ln_b,
                         jnp.zeros_like(rw_w0)], axis=1),
        "w_up": rw_w_up.astype(BF16), "a_up": rw_a_up.astype(BF16), "g_up": rw_g_up.astype(BF16),
        "seg2": _same_segment2(RW_W, HEAD_DIM),
    }
    w_gd = {
        "conv": jnp.pad(gd_conv, ((0, 0), (0, SUBLANES - GD_CONV), (0, 0))),
        "norm_g": jnp.tile(gd_norm_g, (1, GD_HEADS))[:, None, :],
        "seg2": _same_segment2(GD_W, GD_DK),
    }
    w_mg = {"br_fox": w_br_fox.astype(BF16), "br_rw": w_br_rw.astype(BF16),
            "br_gd": w_br_gd.astype(BF16), "out": w_out.astype(BF16)}
    ffn1 = (norm_ffn1[:, None, :], ffn1_wg.astype(BF16), ffn1_wu.astype(BF16), ffn1_wd.astype(BF16))
    ffn2 = (norm_ffn2[:, None, :], ffn2_wg.astype(BF16), ffn2_wu.astype(BF16), ffn2_wd.astype(BF16))
    norm_mix3 = norm_mix[:, None, :]
    final_g = final_norm[None, :]
    fox_consts = _fox_prompt_consts()

    n_pool, page = cache_k.shape[1], cache_k.shape[2]
    ckt = jnp.transpose(cache_k, (0, 1, 3, 4, 2)).reshape(depth, n_pool, FOX_W, page)
    cvt = jnp.transpose(cache_v, (0, 1, 3, 4, 2)).reshape(depth, n_pool, FOX_W, page)
    clft = jnp.swapaxes(cache_logf, 2, 3)
    pps = _pages_per_step(page_table.shape[1])
    gd_conv0_s = jnp.pad(state_gdn_conv, ((0, 0), (0, 0), (SUBLANES - (GD_CONV - 1), 0), (0, 0)))
    zeros_p = {
        "shift": jnp.zeros((bp, 1, RW_COLS), F32),
        "rw_s": jnp.zeros((1, bp, RW_HEADS, HEAD_DIM, HEAD_DIM), F32),
        "conv": jnp.zeros((bp, SUBLANES, GD_QKV), F32),
        "gd_s": jnp.zeros((1, bp, GD_HEADS, GD_DK, GD_DV), F32),
    }

    xp = x_prompt.reshape(bp * seq, d)
    xs = jnp.pad(x_sample, ((0, 0), (0, tp - t_s), (0, 0))).reshape(bd * tp, d)
    tm_p, tm_s = _row_tile(bp * seq), _row_tile(bd * tp)
    tm_in_p = min(tm_p, 256)
    tq = 512 if seq % 512 == 0 else min(256, seq)

    def layer_fn(x, l, *, batch, t_len, tm, tm_in, fox_fn, shift0, rw_s0, conv0, gd_s0, chunk, n_chunks,
                 n_seq, t_valid, last):
        x = _ffn(x, *ffn1, final_g, l, tm=tm)
        qkvb, fk, fv, kt, vt, zr, gqkv, gz, zg, sm = _inproj(x, norm_mix3, w_proj, l, tm=tm_in, seq=t_len)
        o_fox = fox_fn(qkvb, vt, sm, l)
        o_rw, rw_st = _rwkv(zr, shift0, rw_s0, w_rw, l, batch, t_len, chunk=chunk, n_chunks=n_chunks,
                            n_seq=n_seq, t_valid=t_valid)
        o_gd, gd_s = _gdn(gqkv, gz, sm, conv0, gd_s0, w_gd, l, batch, t_len, chunk=chunk, n_chunks=n_chunks,
                          n_seq=n_seq, t_valid=t_valid)
        x = _merge(x, o_fox, o_rw, o_gd, zg, w_mg, l, tm=tm)
        x = _ffn(x, *ffn2, final_g, l, tm=tm, final_norm=last)
        return x, (fk, fv, kt, vt, sm, zr, rw_st, gqkv, gd_s)

    p_states, s_states = [], []
    for l in range(depth):
        last = l == depth - 1
        xp, st = layer_fn(
            xp, l, batch=bp, t_len=seq, tm=tm_p, tm_in=tm_in_p,
            fox_fn=lambda qkvb, vt, sm, l: _fox_prompt(qkvb, vt, sm, fox_consts, bp, seq, tq=tq),
            shift0=zeros_p["shift"], rw_s0=zeros_p["rw_s"], conv0=zeros_p["conv"], gd_s0=zeros_p["gd_s"],
            chunk=chunk_p, n_chunks=n_chunks_p, n_seq=n_seq_p, t_valid=chunk_p, last=last)
        _, _, kt, vt, sm, zr, rw_st, gqkv, gd_s = st
        p_states.append((
            kt, vt,
            sm[:, :FOX_HEADS].reshape(bp, seq, FOX_HEADS),
            zr.reshape(bp, seq, RW_COLS)[:, seq - 1],
            rw_st,
            gqkv.reshape(bp, seq, GD_QKV)[:, seq - (GD_CONV - 1):],
            gd_s))
        xs, st = layer_fn(
            xs, l, batch=bd, t_len=tp, tm=tm_s, tm_in=tm_s,
            fox_fn=lambda qkvb, vt, sm, l: _fox_sample(qkvb, sm, ckt, cvt, clft, page_table, l,
                                                       t_valid=t_s, pps=pps),
            shift0=state_rwkv_shift[l][:, None, :], rw_s0=state_rwkv, conv0=gd_conv0_s[l],
            gd_s0=state_gdn, chunk=tp, n_chunks=1, n_seq=n_seq_s, t_valid=t_s, last=last)
        fk, fv, _, _, sm, zr, rw_st, gqkv, gd_s = st
        conv_ext = jnp.concatenate([state_gdn_conv[l], gqkv.reshape(bd, tp, GD_QKV)[:, :t_s]], axis=1)
        s_states.append((
            fk.reshape(bd, tp, FOX_HEADS, HEAD_DIM)[:, :t_s], fv.reshape(bd, tp, FOX_HEADS, HEAD_DIM)[:, :t_s],
            sm[:, :FOX_HEADS].reshape(bd, tp, FOX_HEADS)[:, :t_s],
            zr.reshape(bd, tp, RW_COLS)[:, t_s - 1],
            rw_st,
            conv_ext[:, t_s:],
            gd_s))

    p_out = [jnp.stack(s) for s in zip(*p_states)]
    for i in (0, 1):
        p_out[i] = jnp.transpose(p_out[i].reshape(depth, bp, FOX_HEADS, HEAD_DIM, seq), (0, 1, 4, 2, 3))
    s_out = [jnp.stack(s) for s in zip(*s_states)]
    y_prompt = xp.reshape(bp, seq, d)
    y_sample = xs.reshape(bd, tp, d)[:, :t_s]
    return (y_prompt, y_sample, *p_out, *s_out)
```

```python
import functools

import jax
import jax.numpy as jnp
from jax import lax
from jax.experimental import pallas as pl
from jax.experimental.pallas import tpu as pltpu

F32 = jnp.float32
BF16 = jnp.bfloat16

HEAD_DIM = 64
FOX_HEADS = 8
FOX_W = FOX_HEADS * HEAD_DIM
RW_HEADS = 8
RW_W = RW_HEADS * HEAD_DIM
RW_LORA_W = 64
RW_LORA_A = 64
RW_LORA_G = 128
RW_COLS = 3 * RW_W + RW_LORA_W + RW_LORA_A + RW_LORA_G
RW_GN_EPS = 64e-5
GD_HEADS = 4
GD_DK = 128
GD_DV = 128
GD_W = GD_HEADS * GD_DK
GD_QKV = 2 * GD_W + GD_HEADS * GD_DV
GD_CONV = 4
N_BRANCH = 3
EPS = 1e-6
L2_EPS = 1e-6
N_SMALL = 16
SUBLANES = 8
GROUP = 4
RW_GROUPS = RW_HEADS // GROUP
RW_GW = GROUP * HEAD_DIM
VMEM_LIMIT = 56 * 1024 * 1024


def _cparams(*sem):
    return pltpu.CompilerParams(dimension_semantics=sem, vmem_limit_bytes=VMEM_LIMIT)


def _const_spec(shape, index_map):
    return pl.BlockSpec(shape, index_map, pipeline_mode=pl.Buffered(1))


def _dot(a, b):
    return jnp.dot(a, b, preferred_element_type=F32)


def _dot_nt(a, b):
    return lax.dot_general(a, b, (((1,), (1,)), ((), ())), preferred_element_type=F32)


def _dot_tn(a, b):
    return lax.dot_general(a, b, (((0,), (0,)), ((), ())), preferred_element_type=F32)


def _split2(x):
    hi = x.astype(BF16)
    lo = (x - hi.astype(F32)).astype(BF16)
    return hi, lo


def _split3(x):
    hi = x.astype(BF16)
    r = x - hi.astype(F32)
    mid = r.astype(BF16)
    lo = (r - mid.astype(F32)).astype(BF16)
    return hi, mid, lo


def _dot_sel_r(x, sel):
    hi, mid, lo = _split3(x)
    return _dot(hi, sel) + _dot(mid, sel) + _dot(lo, sel)


def _dot_sel_l(sel, x):
    return _dot(jnp.concatenate([sel] * 3, axis=1), jnp.concatenate(_split3(x), axis=0))


def _seg_sum(x, seg2):
    return _dot(jnp.concatenate(_split2(x), axis=1), seg2)


def _softplus(z):
    return jnp.maximum(z, 0.0) + jnp.log(1.0 + jnp.exp(-jnp.abs(z)))


def _sigmoid(z):
    return 1.0 / (1.0 + jnp.exp(-z))


def _rms(x, g):
    return x * lax.rsqrt(jnp.mean(x * x, axis=-1, keepdims=True) + EPS) * g


def _tri(n, kind):
    r = lax.broadcasted_iota(jnp.int32, (n, n), 0)
    c = lax.broadcasted_iota(jnp.int32, (n, n), 1)
    return {"incl": r >= c, "strict": r > c, "upper_incl": r <= c}[kind]


def _block_tri(n, blk, strict):
    r = lax.broadcasted_iota(jnp.int32, (n, n), 0)
    c = lax.broadcasted_iota(jnp.int32, (n, n), 1)
    same = (r // blk) == (c // blk)
    return same & ((r > c) if strict else (r >= c))


def _head_mask(rows, row_blk, cols, col_blk):
    r = lax.broadcasted_iota(jnp.int32, (rows, cols), 0) // row_blk
    c = lax.broadcasted_iota(jnp.int32, (rows, cols), 1) // col_blk
    return r == c


def _expand(x, mask):
    return jnp.where(mask, jnp.concatenate([x] * GROUP, axis=0), jnp.zeros((), x.dtype))


def _unit_lower_inverse_minus_eye(m_wide, order, amask):
    width = m_wide.shape[1]
    npow = -m_wide
    t = npow
    if order <= 2:
        return t
    nb = npow.astype(BF16)
    npow = _dot(nb, _expand(nb, amask))
    yield
    span = 4
    while span <= order:
        nb = npow.astype(BF16)
        tb = _expand(t.astype(BF16), amask)
        if span < order:
            prod = _dot(nb, jnp.concatenate([tb, _expand(nb, amask)], axis=1))
            t = t + npow + prod[:, :width]
            npow = prod[:, width:]
        else:
            t = t + npow + _dot(nb, tb)
        yield
        span *= 2
    return t


def _run_tasks(tasks):
    done, active = {}, []
    pending = list(tasks)
    while pending or active:
        for task in list(pending):
            name, needs_done, fn = task
            if all(n in done for n in needs_done):
                pending.remove(task)
                active.append((name, fn(done)))
        assert active, "task dependencies can never be met"
        for entry in list(active):
            name, gen = entry
            try:
                next(gen)
            except StopIteration as stop:
                done[name] = stop.value
                active.remove(entry)
    return done


def _ffn_body(x_ref, g_ref, wg_ref, wu_ref, wd_ref, fg_ref, o_ref, *, ff_chunk, final_norm):
    x = x_ref[...]
    h = _rms(x, g_ref[...]).astype(BF16)
    d_ff = wg_ref.shape[1]
    acc = jnp.zeros_like(x)
    for c0 in range(0, d_ff, ff_chunk):
        gate = _dot(h, wg_ref[:, c0:c0 + ff_chunk])
        up = _dot(h, wu_ref[:, c0:c0 + ff_chunk])
        act = (gate * _sigmoid(gate) * up).astype(BF16)
        acc = acc + _dot(act, wd_ref[c0:c0 + ff_chunk, :])
    y = x + 0.5 * acc
    if final_norm:
        y = _rms(y, fg_ref[...])
    o_ref[...] = y


def _ffn(x, norm_g, wg, wu, wd, final_g, layer, *, tm, final_norm=False):
    m, d = x.shape
    d_ff = wg.shape[2]
    ff_chunk = d_ff // 4 if (d_ff // 4) % 128 == 0 else d_ff
    body = functools.partial(_ffn_body, ff_chunk=ff_chunk, final_norm=final_norm)
    return pl.pallas_call(
        body,
        grid=(m // tm,),
        in_specs=[
            pl.BlockSpec((tm, d), lambda i: (i, 0)),
            _const_spec((None, 1, d), lambda i: (layer, 0, 0)),
            _const_spec((None, d, d_ff), lambda i: (layer, 0, 0)),
            _const_spec((None, d, d_ff), lambda i: (layer, 0, 0)),
            _const_spec((None, d_ff, d), lambda i: (layer, 0, 0)),
            _const_spec((1, d), lambda i: (0, 0)),
        ],
        out_specs=pl.BlockSpec((tm, d), lambda i: (i, 0)),
        out_shape=jax.ShapeDtypeStruct((m, d), F32),
        compiler_params=_cparams("parallel"),
        name="ffn",
    )(x, norm_g, wg, wu, wd, final_g)


def _small_fn(z, is_logf, is_beta, neg_exp_a):
    return jnp.where(is_logf, -_softplus(-z), jnp.where(is_beta, _sigmoid(z), neg_exp_a * _softplus(z)))


def _inproj_body(x_ref, g_ref, wfox_ref, wrw_ref, wgd_ref, wgate_ref, wsm_ref, wkvt_ref, pc_ref,
                 qkvb_ref, fk_ref, fv_ref, kt_ref, vt_ref, zr_ref, gqkv_ref, gz_ref, zg_ref, sm_ref):
    h = _rms(x_ref[...], g_ref[...]).astype(BF16)
    fox = _dot(h, wfox_ref[...])
    qkvb_ref[...] = fox.astype(BF16)
    fk_ref[...] = fox[:, FOX_W:2 * FOX_W]
    fv_ref[...] = fox[:, 2 * FOX_W:]
    kvt = _dot_nt(wkvt_ref[...], h)
    kt_ref[...] = kvt[:FOX_W]
    vt_ref[...] = kvt[FOX_W:]
    zr_ref[...] = _dot(h, wrw_ref[...])
    gd = _dot(h, wgd_ref[...])
    gqkv_ref[...] = gd[:, :GD_QKV]
    gz_ref[...] = gd[:, GD_QKV:]
    zg_ref[...] = _dot(h, wgate_ref[...])
    pc = pc_ref[...]
    col = lax.broadcasted_iota(jnp.int32, (1, N_SMALL), 1)
    sm = _dot(h, wsm_ref[...]) + pc[0:1]
    sm_ref[...] = _small_fn(sm, col < 8, col < 12, -jnp.exp(pc[1:2]))


def _inproj(x, norm_g, w, layer, *, tm, seq):
    m, d = x.shape
    n_gate = w["gate"].shape[2]
    n_gd = w["gd"].shape[2]
    t_batch, t_seq = (m // seq, seq) if seq % tm == 0 else (1, m)
    tiles_per_seq = t_seq // tm
    row_blk = lambda n: pl.BlockSpec((tm, n), lambda i: (i, 0))
    t_blk = pl.BlockSpec((None, FOX_W, tm), lambda i: (i // tiles_per_seq, 0, i % tiles_per_seq))
    wspec = lambda a: _const_spec((None,) + a.shape[1:], lambda i: (layer, 0, 0))
    out_shape = (
        jax.ShapeDtypeStruct((m, 3 * FOX_W), BF16),
        jax.ShapeDtypeStruct((m, FOX_W), F32),
        jax.ShapeDtypeStruct((m, FOX_W), F32),
        jax.ShapeDtypeStruct((t_batch, FOX_W, t_seq), F32),
        jax.ShapeDtypeStruct((t_batch, FOX_W, t_seq), F32),
        jax.ShapeDtypeStruct((m, RW_COLS), F32),
        jax.ShapeDtypeStruct((m, GD_QKV), F32),
        jax.ShapeDtypeStruct((m, n_gd - GD_QKV), F32),
        jax.ShapeDtypeStruct((m, n_gate), F32),
        jax.ShapeDtypeStruct((m, N_SMALL), F32),
    )
    out_specs = (
        row_blk(3 * FOX_W), row_blk(FOX_W), row_blk(FOX_W), t_blk, t_blk, row_blk(RW_COLS), row_blk(GD_QKV),
        row_blk(n_gd - GD_QKV), row_blk(n_gate), row_blk(N_SMALL),
    )
    return pl.pallas_call(
        _inproj_body,
        grid=(m // tm,),
        in_specs=[
            pl.BlockSpec((tm, d), lambda i: (i, 0)),
            _const_spec((None, 1, d), lambda i: (layer, 0, 0)),
            wspec(w["fox"]), wspec(w["rw"]), wspec(w["gd"]), wspec(w["gate"]),
            wspec(w["small"]), wspec(w["kv_t"]), wspec(w["small_pc"]),
        ],
        out_specs=out_specs,
        out_shape=out_shape,
        compiler_params=_cparams("parallel"),
        name="inproj",
    )(x, norm_g, w["fox"], w["rw"], w["gd"], w["gate"], w["small"], w["kv_t"], w["small_pc"])


FOX_AW = 2 * HEAD_DIM
N_PIECES = 3


def _eye(n):
    return (lax.broadcasted_iota(jnp.int32, (n, n), 0) == lax.broadcasted_iota(jnp.int32, (n, n), 1)).astype(BF16)


def _fox_prompt_body(q_ref, k_ref, vt_in_ref, lf_ref, pk_ref, pc_ref, one_ref, o_ref,
                     kaug_ref, vt_ref, qaug_ref, m_ref, l_ref, acc_ref, *, tq, cb):
    i = pl.program_id(1)
    seq = k_ref.shape[0]

    @pl.when(i == 0)
    def _():
        lower = _tri(cb, "incl").astype(BF16)
        carry = jnp.zeros((1, N_SMALL), F32)
        for b0 in range(0, seq, cb):
            cc = _dot_sel_l(lower, lf_ref[b0:b0 + cb, :]) + carry
            carry = cc[cb - 1:cb, :]
            pieces = jnp.concatenate(_split3(-cc), axis=1)
            kaug_ref[b0:b0 + cb, :] = (_dot(k_ref[b0:b0 + cb, :], pk_ref[...])
                                       + _dot(pieces, pc_ref[...])).astype(BF16)
            vt_ref[:, b0:b0 + cb] = vt_in_ref[:, b0:b0 + cb].astype(BF16)

    qaug_ref[...] = (_dot(q_ref[...], pk_ref[...]) + one_ref[...]).astype(BF16)
    m_ref[...] = jnp.full(m_ref.shape, -jnp.inf, F32)
    l_ref[...] = jnp.zeros(l_ref.shape, F32)
    acc_ref[...] = jnp.zeros(acc_ref.shape, F32)

    def tile(j, diagonal):
        k0 = pl.multiple_of(j * tq, tq)
        if diagonal:
            visible = _tri(tq, "upper_incl")
        scores = [_dot_nt(kaug_ref[pl.ds(k0, tq), h * FOX_AW:(h + 1) * FOX_AW],
                          qaug_ref[:, h * FOX_AW:(h + 1) * FOX_AW]) for h in range(FOX_HEADS)]
        for h in range(FOX_HEADS):
            hs = slice(h * HEAD_DIM, (h + 1) * HEAD_DIM)
            st = scores[h]
            if diagonal:
                st = jnp.where(visible, st, -jnp.inf)
            m_old = m_ref[h]
            m_new = jnp.maximum(m_old, jnp.max(st, axis=0, keepdims=True))
            alpha = jnp.exp(m_old - m_new)
            p = jnp.exp(st - m_new)
            m_ref[h] = m_new
            l_ref[h] = alpha * l_ref[h] + jnp.sum(p, axis=0, keepdims=True)
            acc_ref[hs, :] = alpha * acc_ref[hs, :] + _dot(vt_ref[hs, pl.ds(k0, tq)], p.astype(BF16))

    def body(j, carry):
        tile(j, False)
        return carry

    lax.fori_loop(0, i, body, 0)
    tile(i, True)
    out_t = jnp.concatenate(
        [acc_ref[h * HEAD_DIM:(h + 1) * HEAD_DIM, :] / l_ref[h] for h in range(FOX_HEADS)], axis=0)
    o_ref[...] = _dot_nt(_eye(tq), out_t.astype(BF16)).astype(BF16)


def _fox_prompt(qkvb, vt, sm, consts, batch, seq, *, tq):
    m = batch * seq
    nq = seq // tq
    cb = min(512, seq)
    body = functools.partial(_fox_prompt_body, tq=tq, cb=cb)
    aug_w = FOX_HEADS * FOX_AW
    return pl.pallas_call(
        body,
        grid=(batch, nq),
        in_specs=[
            pl.BlockSpec((tq, FOX_W), lambda b, i: (b * nq + i, 0)),
            pl.BlockSpec((seq, FOX_W), lambda b, i: (b, 1)),
            pl.BlockSpec((None, FOX_W, seq), lambda b, i: (b, 0, 0)),
            pl.BlockSpec((seq, N_SMALL), lambda b, i: (b, 0)),
            _const_spec((FOX_W, aug_w), lambda b, i: (0, 0)),
            _const_spec((N_PIECES * N_SMALL, aug_w), lambda b, i: (0, 0)),
            _const_spec((1, aug_w), lambda b, i: (0, 0)),
        ],
        out_specs=pl.BlockSpec((tq, FOX_W), lambda b, i: (b * nq + i, 0)),
        out_shape=jax.ShapeDtypeStruct((m, FOX_W), BF16),
        scratch_shapes=[pltpu.VMEM((seq, aug_w), BF16), pltpu.VMEM((FOX_W, seq), BF16),
                        pltpu.VMEM((tq, aug_w), BF16),
                        pltpu.VMEM((FOX_HEADS, 1, tq), F32), pltpu.VMEM((FOX_HEADS, 1, tq), F32),
                        pltpu.VMEM((FOX_W, tq), F32)],
        compiler_params=_cparams("parallel", "arbitrary"),
        name="fox_prompt",
    )(qkvb, qkvb, vt, sm, consts["place_k"], consts["place_c"], consts["ones"])


def _fox_prompt_consts():
    aug_w = FOX_HEADS * FOX_AW
    lane = jnp.arange(aug_w)
    src = jnp.arange(FOX_W)
    place_k = (lane[None, :] == (src // HEAD_DIM * FOX_AW + src % HEAD_DIM)[:, None]).astype(BF16)
    col = jnp.arange(N_PIECES * N_SMALL)
    tgt = jnp.where(col % N_SMALL < FOX_HEADS, (col % N_SMALL) * FOX_AW + HEAD_DIM + col // N_SMALL, -1)
    place_c = (lane[None, :] == tgt[:, None]).astype(BF16)
    ones = ((lane % FOX_AW >= HEAD_DIM) & (lane % FOX_AW < HEAD_DIM + N_PIECES)).astype(F32)[None, :]
    return {"place_k": place_k, "place_c": place_c, "ones": ones}


def _fox_sample_body(pt_ref, q_ref, kn_ref, vn_ref, lfn_ref, *rest, t_valid, pps):
    kp_refs = rest[0:pps]
    vp_refs = rest[pps:2 * pps]
    lf_refs = rest[2 * pps:3 * pps]
    o_ref, qbd_ref, m_ref, l_ref, acc_ref, tail_ref = rest[3 * pps:]
    s_idx = pl.program_id(1)
    tp = SUBLANES
    nrow = t_valid * FOX_HEADS
    page = kp_refs[0].shape[1]
    diag = (lax.broadcasted_iota(jnp.int32, (nrow, FOX_W), 0) % FOX_HEADS
            == lax.broadcasted_iota(jnp.int32, (nrow, FOX_W), 1) // HEAD_DIM)

    def expand_rows(x):
        return jnp.concatenate([x] * t_valid, axis=0)

    @pl.when(s_idx == 0)
    def _():
        qf = q_ref[...].astype(F32)
        q_rows = jnp.concatenate([jnp.broadcast_to(qf[t:t + 1], (FOX_HEADS, FOX_W)) for t in range(t_valid)],
                                 axis=0)
        qbd = jnp.where(diag, q_rows, 0.0).astype(BF16)
        qbd_ref[...] = qbd
        upper = _tri(tp, "upper_incl").astype(BF16)
        lf_hi, lf_mid, lf_lo = _split3(lfn_ref[...])
        cn = (_dot_tn(lf_hi, upper) + _dot_tn(lf_mid, upper) + _dot_tn(lf_lo, upper))[:FOX_HEADS]
        s = _dot_nt(qbd, kn_ref[...]) - expand_rows(cn)
        t_q = lax.broadcasted_iota(jnp.int32, (nrow, tp), 0) // FOX_HEADS
        u_k = lax.broadcasted_iota(jnp.int32, (nrow, tp), 1)
        s = jnp.where((u_k <= t_q) & (u_k < t_valid), s, -jnp.inf)
        m = jnp.max(s, axis=-1, keepdims=True)
        p = jnp.exp(s - m)
        m_ref[...] = m
        l_ref[...] = jnp.sum(p, axis=-1, keepdims=True)
        acc_ref[...] = _dot(p.astype(BF16), vn_ref[...])
        tail_ref[...] = jnp.zeros_like(tail_ref)

    @pl.when(s_idx > 0)
    def _():
        later3 = jnp.concatenate([_tri(page, "strict").astype(BF16)] * 3, axis=0)
        qbd = qbd_ref[...]
        tail = tail_ref[...]
        scores = [None] * pps
        for r in reversed(range(pps)):
            lf = lf_refs[r][...]
            suf = _dot(jnp.concatenate(_split3(lf), axis=1), later3) + tail
            tail = tail + jnp.sum(lf, axis=-1, keepdims=True)
            scores[r] = _dot(qbd, kp_refs[r][...].astype(BF16)) + expand_rows(suf)
        tail_ref[...] = tail
        m_old = m_ref[...]
        m_new = m_old
        for r in range(pps):
            m_new = jnp.maximum(m_new, jnp.max(scores[r], axis=-1, keepdims=True))
        alpha = jnp.exp(m_old - m_new)
        l_new = alpha * l_ref[...]
        acc = alpha * acc_ref[...]
        for r in range(pps):
            p = jnp.exp(scores[r] - m_new)
            l_new = l_new + jnp.sum(p, axis=-1, keepdims=True)
            acc = acc + _dot_nt(p.astype(BF16), vp_refs[r][...].astype(BF16))
        m_ref[...] = m_new
        l_ref[...] = l_new
        acc_ref[...] = acc

    @pl.when(s_idx == pl.num_programs(1) - 1)
    def _():
        o = jnp.where(diag, acc_ref[...] / l_ref[...], 0.0).astype(BF16)
        pick = (lax.broadcasted_iota(jnp.int32, (tp, nrow), 1) // FOX_HEADS
                == lax.broadcasted_iota(jnp.int32, (tp, nrow), 0)).astype(BF16)
        o_ref[...] = _dot(pick, o).astype(BF16)


def _fox_sample(qkvb, sm, cache_kt, cache_vt, cache_lft, page_table, layer, *, t_valid, pps):
    tp = SUBLANES
    bd, n_pages = page_table.shape
    page = cache_kt.shape[3]
    nrow = t_valid * FOX_HEADS
    assert n_pages % pps == 0

    def page_spec(r, blk):
        def index_map(b, s, pt):
            return (layer, pt[b, n_pages - jnp.maximum(s, 1) * pps + r]) + (0,) * (len(blk) - 2)
        return pl.BlockSpec(blk, index_map)

    kv_blk = (None, None, FOX_W, page)
    lf_blk = (None, None, FOX_HEADS, page)
    grid_spec = pltpu.PrefetchScalarGridSpec(
        num_scalar_prefetch=1,
        grid=(bd, n_pages // pps + 1),
        in_specs=[
            pl.BlockSpec((tp, FOX_W), lambda b, s, pt: (b, 0)),
            pl.BlockSpec((tp, FOX_W), lambda b, s, pt: (b, 1)),
            pl.BlockSpec((tp, FOX_W), lambda b, s, pt: (b, 2)),
            pl.BlockSpec((tp, N_SMALL), lambda b, s, pt: (b, 0)),
            *[page_spec(r, kv_blk) for r in range(pps)],
            *[page_spec(r, kv_blk) for r in range(pps)],
            *[page_spec(r, lf_blk) for r in range(pps)],
        ],
        out_specs=pl.BlockSpec((tp, FOX_W), lambda b, s, pt: (b, 0)),
        scratch_shapes=[
            pltpu.VMEM((nrow, FOX_W), BF16),
            pltpu.VMEM((nrow, 1), F32),
            pltpu.VMEM((nrow, 1), F32),
            pltpu.VMEM((nrow, FOX_W), F32),
            pltpu.VMEM((FOX_HEADS, 1), F32),
        ],
    )
    body = functools.partial(_fox_sample_body, t_valid=t_valid, pps=pps)
    return pl.pallas_call(
        body,
        grid_spec=grid_spec,
        out_shape=jax.ShapeDtypeStruct((bd * tp, FOX_W), BF16),
        compiler_params=_cparams("parallel", "arbitrary"),
        name="fox_sample",
    )(page_table, qkvb, qkvb, qkvb, sm, *([cache_kt] * pps), *([cache_vt] * pps), *([cache_lft] * pps))


def _rwkv_body(z_ref, shift_ref, s0_ref, mu_ref, pv_ref, wup_ref, aup_ref, gup_ref, seg_ref,
               o_ref, so_ref, sbd_ref, prev_ref, *, chunk, n_chunks, n_seq, t_valid):
    c = pl.program_id(1)
    tb = chunk * n_chunks
    rows_all = n_seq * tb
    rows_x = GROUP * chunk
    lane_mask = _head_mask(rows_x, chunk, RW_GW, HEAD_DIM)
    amask = _head_mask(rows_x, chunk, rows_x, chunk)
    state_mask = _head_mask(RW_GW, HEAD_DIM, RW_GW, HEAD_DIM)

    @pl.when(c == 0)
    def _():
        for i in range(n_seq):
            prev_ref[i, 0:1, :] = shift_ref[i]
            for g in range(RW_GROUPS):
                blocks = [jnp.concatenate([s0_ref[i, g * GROUP + h]] * GROUP, axis=1) for h in range(GROUP)]
                sbd_ref[i * RW_GROUPS + g] = jnp.where(state_mask, jnp.concatenate(blocks, axis=0), 0.0)

    z = z_ref[...].reshape(rows_all, RW_COLS)
    row = lax.broadcasted_iota(jnp.int32, (rows_all, 1), 0)
    z_prev = pltpu.roll(z, 1, axis=0)
    for i in range(n_seq):
        z_prev = jnp.where(row == i * tb, prev_ref[i, 0:1, :], z_prev)
        prev_ref[i, 0:1, :] = z[(i + 1) * tb - 1:(i + 1) * tb, :]
    zs = z + (z_prev - z) * mu_ref[...]
    r = zs[:, 0:RW_W]
    k = zs[:, RW_W:2 * RW_W]
    v = zs[:, 2 * RW_W:3 * RW_W]
    o_l = 3 * RW_W
    wd = zs[:, o_l:o_l + RW_LORA_W]
    ad = zs[:, o_l + RW_LORA_W:o_l + RW_LORA_W + RW_LORA_A]
    gd = zs[:, o_l + RW_LORA_W + RW_LORA_A:]
    pv = pv_ref[...]
    w_log = -_softplus(-(pv[0:1] + _dot(jnp.tanh(wd).astype(BF16), wup_ref[...]))) - 0.5
    lw = -jnp.exp(w_log)
    a = _sigmoid(pv[1:2] + _dot(ad.astype(BF16), aup_ref[...]))
    g_out = _dot(_sigmoid(gd).astype(BF16), gup_ref[...])
    kk = k * pv[2:3]
    k = k * (1.0 + (a - 1.0) * pv[3:4])
    seg2 = seg_ref[...]
    kk = kk * lax.rsqrt(_seg_sum(kk * kk, seg2) + L2_EPS)
    if t_valid < chunk:
        valid = row % tb < t_valid
        lw = jnp.where(valid, lw, 0.0)
        kk = jnp.where(valid, kk, 0.0)
        k = jnp.where(valid, k, 0.0)
    cs = _dot_sel_l(_block_tri(rows_all, chunk, strict=False).astype(BF16), lw)
    p_inv = jnp.exp(-cs)
    rp = (r * jnp.exp(cs)).astype(BF16)
    kkp = (kk * jnp.exp(cs - lw)).astype(BF16)
    bn = kk * a * p_inv
    kn = k * p_inv
    vb = v.astype(BF16)
    t_idx = lax.broadcasted_iota(jnp.int32, (chunk, 2 * rows_x), 0)
    j_idx = lax.broadcasted_iota(jnp.int32, (chunk, 2 * rows_x), 1) % chunk
    strict = (t_idx > j_idx)[:, :rows_x]
    incl2 = t_idx >= j_idx
    groups = range(RW_GROUPS)
    gsl = [slice(g * RW_GW, (g + 1) * RW_GW) for g in groups]
    bnb = bn.astype(BF16)
    knb = kn.astype(BF16)

    def prepare(ci, g):
        def gen(done):
            rs = slice(ci * chunk, (ci + 1) * chunk)
            bk_x = jnp.concatenate([_expand(bnb[rs, gsl[g]], lane_mask), _expand(knb[rs, gsl[g]], lane_mask)],
                                   axis=0)
            a_all = _dot_nt(jnp.concatenate([kkp[rs, gsl[g]], rp[rs, gsl[g]]], axis=0), bk_x)
            yield
            a_ak = jnp.where(strict, a_all[:chunk, rows_x:], 0.0).astype(BF16)
            a_r = jnp.where(incl2, a_all[chunk:, :], 0.0).astype(BF16)
            t_m1 = yield from _unit_lower_inverse_minus_eye(
                jnp.where(strict, a_all[:chunk, :rows_x], 0.0), chunk, amask)
            return a_ak, a_r, t_m1.astype(BF16)
        return gen

    def advance(ci):
        def gen(done):
            rs = slice(ci * chunk, (ci + 1) * chunk)
            pre = [done["prepare", ci, g] for g in groups]
            p_end = jnp.exp(cs[(ci + 1) * chunk - 1:(ci + 1) * chunk, :])
            bk_e = jnp.concatenate([(bn[rs] * p_end).astype(BF16), (kn[rs] * p_end).astype(BF16)], axis=0)
            v_xs = [_expand(vb[rs, gsl[g]], lane_mask) for g in groups]
            s_base = ci // n_chunks * RW_GROUPS
            sts = [sbd_ref[s_base + g] for g in groups]
            stbs = [st.astype(BF16) for st in sts]
            rhs = [_dot_nt(kkp[rs, gsl[g]], stbs[g]) + _dot(pre[g][0], v_xs[g]) for g in groups]
            yield
            sabs = [(-(rhs[g] + _dot(pre[g][2], _expand(rhs[g].astype(BF16), lane_mask)))).astype(BF16)
                    for g in groups]
            yield
            y_c = jnp.concatenate(
                [_dot_nt(rp[rs, gsl[g]], stbs[g])
                 + _dot(pre[g][1], jnp.concatenate([_expand(sabs[g], lane_mask), v_xs[g]], axis=0))
                 for g in groups], axis=1)
            yield
            for g in groups:
                outer = _dot_tn(jnp.concatenate([sabs[g], vb[rs, gsl[g]]], axis=0), bk_e[:, gsl[g]])
                sbd_ref[s_base + g] = sts[g] * p_end[:, gsl[g]] + jnp.where(state_mask, outer, 0.0)
            yield
            return y_c
        return gen

    tasks = []
    for ci in range(n_seq * n_chunks):
        tasks.append((("advance", ci),
                      [("prepare", ci, g) for g in groups] + ([("advance", ci - 1)] if ci % n_chunks else []),
                      advance(ci)))
        for g in groups:
            tasks.append((("prepare", ci, g), [], prepare(ci, g)))
    done = _run_tasks(tasks)
    y_rows = [done["advance", ci] for ci in range(n_seq * n_chunks)]
    y = y_rows[0] if len(y_rows) == 1 else jnp.concatenate(y_rows, axis=0)
    inv_n = 1.0 / HEAD_DIM
    mean = _seg_sum(y, seg2) * inv_n
    yc = y - mean
    var = _seg_sum(yc * yc, seg2) * inv_n
    yn = yc * lax.rsqrt(var + RW_GN_EPS) * pv[5:6] + pv[6:7]
    bonus = _seg_sum(r * k * pv[4:5], seg2) * v
    o_ref[...] = ((yn + bonus) * g_out).reshape(n_seq, tb, RW_W).astype(BF16)

    @pl.when(c == pl.num_programs(1) - 1)
    def _():
        for i in range(n_seq):
            for g in range(RW_GROUPS):
                st = sbd_ref[i * RW_GROUPS + g]
                for h in range(GROUP):
                    hs = slice(h * HEAD_DIM, (h + 1) * HEAD_DIM)
                    so_ref[i, g * GROUP + h] = st[hs, hs]


def _rwkv(zr, shift0, s0, w, layer, batch, seq, *, chunk, n_chunks, n_seq, t_valid):
    tb = chunk * n_chunks
    nc = seq // tb
    assert batch % n_seq == 0
    body = functools.partial(_rwkv_body, chunk=chunk, n_chunks=n_chunks, n_seq=n_seq, t_valid=t_valid)
    lspec = lambda a: _const_spec((None,) + a.shape[1:], lambda b, c: (layer, 0, 0))
    state_spec = pl.BlockSpec((n_seq, RW_HEADS, HEAD_DIM, HEAD_DIM), lambda b, c: (b, 0, 0, 0))
    s0_layer = layer if s0.shape[0] > 1 else 0
    o_rw, state = pl.pallas_call(
        body,
        grid=(batch // n_seq, nc),
        in_specs=[
            pl.BlockSpec((n_seq, tb, RW_COLS), lambda b, c: (b, c, 0)),
            pl.BlockSpec((n_seq, 1, RW_COLS), lambda b, c: (b, 0, 0)),
            pl.BlockSpec((None, n_seq, RW_HEADS, HEAD_DIM, HEAD_DIM), lambda b, c: (s0_layer, b, 0, 0, 0)),
            lspec(w["mu"]), lspec(w["pv"]), lspec(w["w_up"]), lspec(w["a_up"]), lspec(w["g_up"]),
            _const_spec((2 * RW_W, RW_W), lambda b, c: (0, 0)),
        ],
        out_specs=(pl.BlockSpec((n_seq, tb, RW_W), lambda b, c: (b, c, 0)), state_spec),
        out_shape=(
            jax.ShapeDtypeStruct((batch, seq, RW_W), BF16),
            jax.ShapeDtypeStruct((batch, RW_HEADS, HEAD_DIM, HEAD_DIM), F32),
        ),
        scratch_shapes=[pltpu.VMEM((n_seq * RW_GROUPS, RW_GW, RW_GW), F32),
                        pltpu.VMEM((n_seq, SUBLANES, RW_COLS), F32)],
        compiler_params=_cparams("parallel", "arbitrary"),
        name="rwkv",
    )(zr.reshape(batch, seq, RW_COLS), shift0, s0, w["mu"], w["pv"], w["w_up"], w["a_up"], w["g_up"], w["seg2"])
    return o_rw.reshape(batch * seq, RW_W), state


def _gdn_body(x_ref, gz_ref, sm_ref, conv0_ref, s0_ref, cw_ref, ng_ref, seg_ref,
              o_ref, so_ref, prev_ref, *, chunk, n_chunks, n_seq, t_valid):
    c = pl.program_id(1)
    tb = chunk * n_chunks
    rows_all = n_seq * tb
    rows_x = GD_HEADS * chunk
    lane_mask = _head_mask(rows_x, chunk, GD_W, GD_DK)
    amask = _head_mask(rows_x, chunk, rows_x, chunk)

    @pl.when(c == 0)
    def _():
        prev_ref[...] = conv0_ref[...]
        so_ref[...] = s0_ref[...]

    x = x_ref[...].reshape(rows_all, GD_QKV)
    prevs = [prev_ref[i] for i in range(n_seq)]
    for i in range(n_seq):
        prev_ref[i] = x[(i + 1) * tb - SUBLANES:(i + 1) * tb, :]
    cw = cw_ref[...]
    row8 = lax.broadcasted_iota(jnp.int32, (SUBLANES, 1), 0)
    conv = x * cw[GD_CONV - 1:GD_CONV]
    for sft in range(1, GD_CONV):
        xr = pltpu.roll(x, sft, axis=0)
        parts = []
        for i in range(n_seq):
            parts.append(jnp.where(row8 < sft, pltpu.roll(prevs[i], sft, axis=0), xr[i * tb:i * tb + SUBLANES]))
            if tb > SUBLANES:
                parts.append(xr[i * tb + SUBLANES:(i + 1) * tb])
        xs = parts[0] if len(parts) == 1 else jnp.concatenate(parts, axis=0)
        conv = conv + xs * cw[GD_CONV - 1 - sft:GD_CONV - sft]
    act = conv * _sigmoid(conv)
    seg2 = seg_ref[...]
    q = act[:, :GD_W]
    k = act[:, GD_W:2 * GD_W]
    v = act[:, 2 * GD_W:]
    q = q * lax.rsqrt(_seg_sum(q * q, seg2) + L2_EPS) * (GD_DK ** -0.5)
    k = k * lax.rsqrt(_seg_sum(k * k, seg2) + L2_EPS)
    sm = sm_ref[...].reshape(rows_all, N_SMALL)
    beta = sm[:, 8:8 + GD_HEADS]
    g = sm[:, 12:12 + GD_HEADS]
    if t_valid < chunk:
        rowc = lax.broadcasted_iota(jnp.int32, (rows_all, 1), 0) % tb
        beta = jnp.where(rowc < t_valid, beta, 0.0)
        g = jnp.where(rowc < t_valid, g, 0.0)
    gc = _dot_sel_l(_block_tri(rows_all, chunk, strict=False).astype(BF16), g)
    head_lanes = _head_mask(GD_HEADS, 1, GD_W, GD_DK).astype(BF16)
    beta_l = _dot_sel_r(beta, head_lanes)
    gc_l = _dot_sel_r(gc, head_lanes)
    egc_l = jnp.exp(gc_l)
    kb = k * beta_l
    vbeta = (v * beta_l).astype(BF16)
    kbg = (kb * egc_l).astype(BF16)
    qg = (q * egc_l).astype(BF16)
    kb = kb.astype(BF16)
    qb = q.astype(BF16)
    kbf = k.astype(BF16)
    t_idx = lax.broadcasted_iota(jnp.int32, (chunk, rows_x), 0)
    s_idx = lax.broadcasted_iota(jnp.int32, (chunk, rows_x), 1) % chunk
    strict = t_idx > s_idx
    incl = t_idx >= s_idx
    diag = t_idx == s_idx
    head_cols = _head_mask(GD_HEADS, 1, rows_x, chunk).astype(BF16)
    ones_cc = jnp.ones((chunk, chunk), BF16)
    hsl = [slice(h * GD_DK, (h + 1) * GD_DK) for h in range(GD_HEADS)]

    def prepare(ci):
        def gen(done):
            rs = slice(ci * chunk, (ci + 1) * chunk)
            g_t = _dot_sel_r(gc[rs], head_cols)
            yield
            g_s = _dot_sel_l(ones_cc, jnp.where(diag, g_t, 0.0))
            qk = _dot_nt(jnp.concatenate([kb[rs], qb[rs]], axis=0), _expand(kbf[rs], lane_mask))
            yield
            decay = jnp.exp(jnp.minimum(g_t - g_s, 0.0))
            amat = jnp.where(incl, qk[chunk:] * decay, 0.0).astype(BF16)
            t_m1 = yield from _unit_lower_inverse_minus_eye(
                jnp.where(strict, qk[:chunk] * decay, 0.0), chunk, amask)
            vk = jnp.concatenate([vbeta[rs], kbg[rs]], axis=1)
            vk_x = jnp.concatenate([_expand(vbeta[rs], lane_mask), _expand(kbg[rs], lane_mask)], axis=1)
            uw = vk.astype(F32) + _dot(t_m1.astype(BF16), vk_x)
            yield
            return amat, uw
        return gen

    def advance(ci):
        def gen(done):
            rs = slice(ci * chunk, (ci + 1) * chunk)
            amat, uw = done["prepare", ci]
            glast_l = gc_l[(ci + 1) * chunk - 1:(ci + 1) * chunk]
            kg = (k[rs] * jnp.exp(glast_l - gc_l[rs])).astype(BF16)
            wb = uw[:, GD_W:].astype(BF16)
            i_seq = ci // n_chunks
            sts = [so_ref[i_seq, h] for h in range(GD_HEADS)]
            stbs = [st.astype(BF16) for st in sts]
            v_news = [uw[:, hs] - _dot(wb[:, hs], stb) for hs, stb in zip(hsl, stbs)]
            yield
            vnb = jnp.concatenate(v_news, axis=1).astype(BF16)
            for hs, st, h in zip(hsl, sts, range(GD_HEADS)):
                so_ref[i_seq, h] = st * jnp.exp(glast_l[:, hs]) + _dot_tn(kg[:, hs], vnb[:, hs])
            yield
            o_c = (jnp.concatenate([_dot(qg[rs, hs], stb) for hs, stb in zip(hsl, stbs)], axis=1)
                   + _dot(amat, _expand(vnb, lane_mask)))
            yield
            return o_c
        return gen

    tasks = []
    for ci in range(n_seq * n_chunks):
        tasks.append((("advance", ci), [("prepare", ci)] + ([("advance", ci - 1)] if ci % n_chunks else []),
                      advance(ci)))
        tasks.append((("prepare", ci), [], prepare(ci)))
    done = _run_tasks(tasks)
    o_rows = [done["advance", ci] for ci in range(n_seq * n_chunks)]
    o = o_rows[0] if len(o_rows) == 1 else jnp.concatenate(o_rows, axis=0)
    o = o * lax.rsqrt(_seg_sum(o * o, seg2) * (1.0 / GD_DV) + EPS) * ng_ref[...]
    gz = gz_ref[...].reshape(rows_all, GD_HEADS * GD_DV)
    o_ref[...] = (o * (gz * _sigmoid(gz))).reshape(n_seq, tb, GD_HEADS * GD_DV).astype(BF16)


def _gdn(gqkv, gz, sm, conv0, s0, w, layer, batch, seq, *, chunk, n_chunks, n_seq, t_valid):
    tb = chunk * n_chunks
    nc = seq // tb
    assert batch % n_seq == 0
    body = functools.partial(_gdn_body, chunk=chunk, n_chunks=n_chunks, n_seq=n_seq, t_valid=t_valid)
    lspec = lambda a: _const_spec((None,) + a.shape[1:], lambda b, c: (layer, 0, 0))
    state_spec = pl.BlockSpec((n_seq, GD_HEADS, GD_DK, GD_DV), lambda b, c: (b, 0, 0, 0))
    s0_layer = layer if s0.shape[0] > 1 else 0
    seq_blk = lambda n: pl.BlockSpec((n_seq, tb, n), lambda b, c: (b, c, 0))
    d_v = GD_HEADS * GD_DV
    o_gd, state = pl.pallas_call(
        body,
        grid=(batch // n_seq, nc),
        in_specs=[
            seq_blk(GD_QKV), seq_blk(d_v), seq_blk(N_SMALL),
            pl.BlockSpec((n_seq, SUBLANES, GD_QKV), lambda b, c: (b, 0, 0)),
            pl.BlockSpec((None, n_seq, GD_HEADS, GD_DK, GD_DV), lambda b, c: (s0_layer, b, 0, 0, 0)),
            lspec(w["conv"]), lspec(w["norm_g"]),
            _const_spec((2 * GD_W, GD_W), lambda b, c: (0, 0)),
        ],
        out_specs=(seq_blk(d_v), state_spec),
        out_shape=(
            jax.ShapeDtypeStruct((batch, seq, d_v), BF16),
            jax.ShapeDtypeStruct((batch, GD_HEADS, GD_DK, GD_DV), F32),
        ),
        scratch_shapes=[pltpu.VMEM((n_seq, SUBLANES, GD_QKV), F32)],
        compiler_params=_cparams("parallel", "arbitrary"),
        name="gdn",
    )(gqkv.reshape(batch, seq, GD_QKV), gz.reshape(batch, seq, d_v), sm.reshape(batch, seq, N_SMALL),
      conv0, s0, w["conv"], w["norm_g"], w["seg2"])
    return o_gd.reshape(batch * seq, d_v), state


def _merge_body(x_ref, of_ref, or_ref, og_ref, zg_ref, wf_ref, wr_ref, wg_ref, wo_ref, o_ref):
    d = x_ref.shape[1]
    merged = _sigmoid(zg_ref[:, 0:d]) * _dot(of_ref[...], wf_ref[...])
    merged = merged + _sigmoid(zg_ref[:, d:2 * d]) * _dot(or_ref[...], wr_ref[...])
    merged = merged + _sigmoid(zg_ref[:, 2 * d:3 * d]) * _dot(og_ref[...], wg_ref[...])
    o_ref[...] = x_ref[...] + _dot(merged.astype(BF16), wo_ref[...])


def _merge(x, o_fox, o_rw, o_gd, zg, w, layer, *, tm):
    m, d = x.shape
    row_blk = lambda n: pl.BlockSpec((tm, n), lambda i: (i, 0))
    wspec = lambda a: _const_spec((None,) + a.shape[1:], lambda i: (layer, 0, 0))
    return pl.pallas_call(
        _merge_body,
        grid=(m // tm,),
        in_specs=[row_blk(d), row_blk(FOX_W), row_blk(RW_W), row_blk(GD_HEADS * GD_DV),
                  row_blk(N_BRANCH * d),
                  wspec(w["br_fox"]), wspec(w["br_rw"]), wspec(w["br_gd"]), wspec(w["out"])],
        out_specs=row_blk(d),
        out_shape=jax.ShapeDtypeStruct((m, d), F32),
        compiler_params=_cparams("parallel"),
        name="merge",
    )(x, o_fox, o_rw, o_gd, zg, w["br_fox"], w["br_rw"], w["br_gd"], w["out"])


def _same_segment2(n, width):
    idx = jnp.arange(n) // width
    seg = (idx[:, None] == idx[None, :]).astype(BF16)
    return jnp.concatenate([seg, seg], axis=0)


def _row_tile(m):
    for tm in (512, 256, 128, 64, 32, 16, 8):
        if m % tm == 0:
            return tm
    raise ValueError(f"row count {m} is not a multiple of 8")


def _pages_per_step(n_pages):
    for pps in (16, 8, 4, 2, 1):
        if n_pages % pps == 0:
            return pps


def kernel(x_prompt, x_sample, cache_k, cache_v, cache_logf, state_rwkv_shift, state_rwkv,
           state_gdn_conv, state_gdn, page_table, norm_ffn1, ffn1_wg, ffn1_wu, ffn1_wd, norm_mix,
           w_in, fox_fb, rw_mu, rw_w0, rw_w_up, rw_a0, rw_a_up, rw_g_up, rw_kk, rw_ka, rw_rk,
           rw_ln_g, rw_ln_b, gd_conv, gd_a_log, gd_dt_bias, gd_norm_g, w_br_fox, w_br_rw, w_br_gd,
           w_out, norm_ffn2, ffn2_wg, ffn2_wu, ffn2_wd, final_norm):
    bp, seq, d = x_prompt.shape
    bd, t_s, _ = x_sample.shape
    depth = w_in.shape[0]
    tp = SUBLANES
    chunk_p = 64
    n_chunks_p = 4 if seq % (4 * chunk_p) == 0 else 1
    n_seq_p = 2 if bp % 2 == 0 else 1
    n_seq_s = 8 if bd % 8 == 0 else 1
    assert t_s <= tp and seq % chunk_p == 0

    o_rw_c = 3 * FOX_W + FOX_HEADS
    o_gd_c = o_rw_c + RW_COLS
    o_gz = o_gd_c + GD_QKV
    o_gb = o_gz + GD_HEADS * GD_DV
    o_ga = o_gb + GD_HEADS
    o_gate = o_ga + GD_HEADS
    scale = HEAD_DIM ** -0.5
    w_small = jnp.concatenate([w_in[:, :, 3 * FOX_W:o_rw_c], w_in[:, :, o_gb:o_gate]], axis=-1)
    zeros4 = jnp.zeros((depth, GD_HEADS), F32)
    small_bias = jnp.concatenate([fox_fb, zeros4, gd_dt_bias], axis=-1)
    small_alog = jnp.concatenate([jnp.zeros((depth, 8), F32), zeros4, gd_a_log], axis=-1)
    w_proj = {
        "fox": jnp.concatenate([w_in[:, :, :FOX_W] * scale, w_in[:, :, FOX_W:3 * FOX_W]], axis=-1).astype(BF16),
        "rw": w_in[:, :, o_rw_c:o_gd_c].astype(BF16),
        "gd": w_in[:, :, o_gd_c:o_gb].astype(BF16),
        "gate": w_in[:, :, o_gate:].astype(BF16),
        "small": w_small.astype(BF16),
        "kv_t": jnp.swapaxes(w_in[:, :, FOX_W:3 * FOX_W], 1, 2).astype(BF16),
        "small_pc": jnp.stack([small_bias, small_alog], axis=1),
    }
    w_rw = {
        "mu": rw_mu[:, None, :],
        "pv": jnp.stack([rw_w0, rw_a0, rw_kk, rw_ka, rw_rk.reshape(depth, RW_W), rw_ln_g, rw_ln_b,
                         jnp.zeros_like(rw_w0)], axis=1),
        "w_up": rw_w_up.astype(BF16), "a_up": rw_a_up.astype(BF16), "g_up": rw_g_up.astype(BF16),
        "seg2": _same_segment2(RW_W, HEAD_DIM),
    }
    w_gd = {
        "conv": jnp.pad(gd_conv, ((0, 0), (0, SUBLANES - GD_CONV), (0, 0))),
        "norm_g": jnp.tile(gd_norm_g, (1, GD_HEADS))[:, None, :],
        "seg2": _same_segment2(GD_W, GD_DK),
    }
    w_mg = {"br_fox": w_br_fox.astype(BF16), "br_rw": w_br_rw.astype(BF16),
            "br_gd": w_br_gd.astype(BF16), "out": w_out.astype(BF16)}
    ffn1 = (norm_ffn1[:, None, :], ffn1_wg.astype(BF16), ffn1_wu.astype(BF16), ffn1_wd.astype(BF16))
    ffn2 = (norm_ffn2[:, None, :], ffn2_wg.astype(BF16), ffn2_wu.astype(BF16), ffn2_wd.astype(BF16))
    norm_mix3 = norm_mix[:, None, :]
    final_g = final_norm[None, :]
    fox_consts = _fox_prompt_consts()

    n_pool, page = cache_k.shape[1], cache_k.shape[2]
    ckt = jnp.transpose(cache_k, (0, 1, 3, 4, 2)).reshape(depth, n_pool, FOX_W, page)
    cvt = jnp.transpose(cache_v, (0, 1, 3, 4, 2)).reshape(depth, n_pool, FOX_W, page)
    clft = jnp.swapaxes(cache_logf, 2, 3)
    pps = _pages_per_step(page_table.shape[1])
    gd_conv0_s = jnp.pad(state_gdn_conv, ((0, 0), (0, 0), (SUBLANES - (GD_CONV - 1), 0), (0, 0)))
    zeros_p = {
        "shift": jnp.zeros((bp, 1, RW_COLS), F32),
        "rw_s": jnp.zeros((1, bp, RW_HEADS, HEAD_DIM, HEAD_DIM), F32),
        "conv": jnp.zeros((bp, SUBLANES, GD_QKV), F32),
        "gd_s": jnp.zeros((1, bp, GD_HEADS, GD_DK, GD_DV), F32),
    }

    xp = x_prompt.reshape(bp * seq, d)
    xs = jnp.pad(x_sample, ((0, 0), (0, tp - t_s), (0, 0))).reshape(bd * tp, d)
    tm_p, tm_s = _row_tile(bp * seq), _row_tile(bd * tp)
    tm_in_p = min(tm_p, 256)
    tq = 512 if seq % 512 == 0 else min(256, seq)

    def layer_fn(x, l, *, batch, t_len, tm, tm_in, fox_fn, shift0, rw_s0, conv0, gd_s0, chunk, n_chunks,
                 n_seq, t_valid, last):
        x = _ffn(x, *ffn1, final_g, l, tm=tm)
        qkvb, fk, fv, kt, vt, zr, gqkv, gz, zg, sm = _inproj(x, norm_mix3, w_proj, l, tm=tm_in, seq=t_len)
        o_fox = fox_fn(qkvb, vt, sm, l)
        o_rw, rw_st = _rwkv(zr, shift0, rw_s0, w_rw, l, batch, t_len, chunk=chunk, n_chunks=n_chunks,
                            n_seq=n_seq, t_valid=t_valid)
        o_gd, gd_s = _gdn(gqkv, gz, sm, conv0, gd_s0, w_gd, l, batch, t_len, chunk=chunk, n_chunks=n_chunks,
                          n_seq=n_seq, t_valid=t_valid)
        x = _merge(x, o_fox, o_rw, o_gd, zg, w_mg, l, tm=tm)
        x = _ffn(x, *ffn2, final_g, l, tm=tm, final_norm=last)
        return x, (fk, fv, kt, vt, sm, zr, rw_st, gqkv, gd_s)

    p_states, s_states = [], []
    for l in range(depth):
        last = l == depth - 1
        xp, st = layer_fn(
            xp, l, batch=bp, t_len=seq, tm=tm_p, tm_in=tm_in_p,
            fox_fn=lambda qkvb, vt, sm, l: _fox_prompt(qkvb, vt, sm, fox_consts, bp, seq, tq=tq),
            shift0=zeros_p["shift"], rw_s0=zeros_p["rw_s"], conv0=zeros_p["conv"], gd_s0=zeros_p["gd_s"],
            chunk=chunk_p, n_chunks=n_chunks_p, n_seq=n_seq_p, t_valid=chunk_p, last=last)
        _, _, kt, vt, sm, zr, rw_st, gqkv, gd_s = st
        p_states.append((
            kt, vt,
            sm[:, :FOX_HEADS].reshape(bp, seq, FOX_HEADS),
            zr.reshape(bp, seq, RW_COLS)[:, seq - 1],
            rw_st,
            gqkv.reshape(bp, seq, GD_QKV)[:, seq - (GD_CONV - 1):],
            gd_s))
        xs, st = layer_fn(
            xs, l, batch=bd, t_len=tp, tm=tm_s, tm_in=tm_s,
            fox_fn=lambda qkvb, vt, sm, l: _fox_sample(qkvb, sm, ckt, cvt, clft, page_table, l,
                                                       t_valid=t_s, pps=pps),
            shift0=state_rwkv_shift[l][:, None, :], rw_s0=state_rwkv, conv0=gd_conv0_s[l],
            gd_s0=state_gdn, chunk=tp, n_chunks=1, n_seq=n_seq_s, t_valid=t_s, last=last)
        fk, fv, _, _, sm, zr, rw_st, gqkv, gd_s = st
        conv_ext = jnp.concatenate([state_gdn_conv[l], gqkv.reshape(bd, tp, GD_QKV)[:, :t_s]], axis=1)
        s_states.append((
            fk.reshape(bd, tp, FOX_HEADS, HEAD_DIM)[:, :t_s], fv.reshape(bd, tp, FOX_HEADS, HEAD_DIM)[:, :t_s],
            sm[:, :FOX_HEADS].reshape(bd, tp, FOX_HEADS)[:, :t_s],
            zr.reshape(bd, tp, RW_COLS)[:, t_s - 1],
            rw_st,
            conv_ext[:, t_s:],
            gd_s))

    p_out = [jnp.stack(s) for s in zip(*p_states)]
    for i in (0, 1):
        p_out[i] = jnp.transpose(p_out[i].reshape(depth, bp, FOX_HEADS, HEAD_DIM, seq), (0, 1, 4, 2, 3))
    s_out = [jnp.stack(s) for s in zip(*s_states)]
    y_prompt = xp.reshape(bp, seq, d)
    y_sample = xs.reshape(bd, tp, d)[:, :t_s]
    return (y_prompt, y_sample, *p_out, *s_out)
```

```python
import functools

import jax
import jax.numpy as jnp
from jax import lax
from jax.experimental import pallas as pl
from jax.experimental.pallas import tpu as pltpu

F32 = jnp.float32
BF16 = jnp.bfloat16

HEAD_DIM = 64
FOX_HEADS = 8
FOX_W = FOX_HEADS * HEAD_DIM
RW_HEADS = 8
RW_W = RW_HEADS * HEAD_DIM
RW_LORA_W = 64
RW_LORA_A = 64
RW_LORA_G = 128
RW_COLS = 3 * RW_W + RW_LORA_W + RW_LORA_A + RW_LORA_G
RW_GN_EPS = 64e-5
GD_HEADS = 4
GD_DK = 128
GD_DV = 128
GD_W = GD_HEADS * GD_DK
GD_QKV = 2 * GD_W + GD_HEADS * GD_DV
GD_CONV = 4
N_BRANCH = 3
EPS = 1e-6
L2_EPS = 1e-6
N_SMALL = 16
SUBLANES = 8
GROUP = 4
RW_GROUPS = RW_HEADS // GROUP
RW_GW = GROUP * HEAD_DIM
VMEM_LIMIT = 56 * 1024 * 1024


def _cparams(*sem):
    return pltpu.CompilerParams(dimension_semantics=sem, vmem_limit_bytes=VMEM_LIMIT)


def _const_spec(shape, index_map):
    return pl.BlockSpec(shape, index_map, pipeline_mode=pl.Buffered(1))


def _dot(a, b):
    return jnp.dot(a, b, preferred_element_type=F32)


def _dot_nt(a, b):
    return lax.dot_general(a, b, (((1,), (1,)), ((), ())), preferred_element_type=F32)


def _dot_tn(a, b):
    return lax.dot_general(a, b, (((0,), (0,)), ((), ())), preferred_element_type=F32)


def _split2(x):
    hi = x.astype(BF16)
    lo = (x - hi.astype(F32)).astype(BF16)
    return hi, lo


def _split3(x):
    hi = x.astype(BF16)
    r = x - hi.astype(F32)
    mid = r.astype(BF16)
    lo = (r - mid.astype(F32)).astype(BF16)
    return hi, mid, lo


def _dot_sel_r(x, sel):
    hi, mid, lo = _split3(x)
    return _dot(hi, sel) + _dot(mid, sel) + _dot(lo, sel)


def _dot_sel_l(sel, x):
    return _dot(jnp.concatenate([sel] * 3, axis=1), jnp.concatenate(_split3(x), axis=0))


def _seg_sum(x, seg2):
    return _dot(jnp.concatenate(_split2(x), axis=1), seg2)


def _softplus(z):
    return jnp.maximum(z, 0.0) + jnp.log(1.0 + jnp.exp(-jnp.abs(z)))


def _sigmoid(z):
    return 1.0 / (1.0 + jnp.exp(-z))


def _rms(x, g):
    return x * lax.rsqrt(jnp.mean(x * x, axis=-1, keepdims=True) + EPS) * g


def _tri(n, kind):
    r = lax.broadcasted_iota(jnp.int32, (n, n), 0)
    c = lax.broadcasted_iota(jnp.int32, (n, n), 1)
    return {"incl": r >= c, "strict": r > c, "upper_incl": r <= c}[kind]


def _block_tri(n, blk, strict):
    r = lax.broadcasted_iota(jnp.int32, (n, n), 0)
    c = lax.broadcasted_iota(jnp.int32, (n, n), 1)
    same = (r // blk) == (c // blk)
    return same & ((r > c) if strict else (r >= c))


def _head_mask(rows, row_blk, cols, col_blk):
    r = lax.broadcasted_iota(jnp.int32, (rows, cols), 0) // row_blk
    c = lax.broadcasted_iota(jnp.int32, (rows, cols), 1) // col_blk
    return r == c


def _expand(x, mask):
    return jnp.where(mask, jnp.concatenate([x] * GROUP, axis=0), jnp.zeros((), x.dtype))


def _unit_lower_inverse_minus_eye(m_wide, order, amask):
    width = m_wide.shape[1]
    npow = -m_wide
    t = npow
    if order <= 2:
        return t
    nb = npow.astype(BF16)
    npow = _dot(nb, _expand(nb, amask))
    yield
    span = 4
    while span <= order:
        nb = npow.astype(BF16)
        tb = _expand(t.astype(BF16), amask)
        if span < order:
            prod = _dot(nb, jnp.concatenate([tb, _expand(nb, amask)], axis=1))
            t = t + npow + prod[:, :width]
            npow = prod[:, width:]
        else:
            t = t + npow + _dot(nb, tb)
        yield
        span *= 2
    return t


def _run_tasks(tasks):
    done, active = {}, []
    pending = list(tasks)
    while pending or active:
        for task in list(pending):
            name, needs_done, fn = task
            if all(n in done for n in needs_done):
                pending.remove(task)
                active.append((name, fn(done)))
        assert active, "task dependencies can never be met"
        for entry in list(active):
            name, gen = entry
            try:
                next(gen)
            except StopIteration as stop:
                done[name] = stop.value
                active.remove(entry)
    return done


def _ffn_body(x_ref, g_ref, wg_ref, wu_ref, wd_ref, fg_ref, o_ref, *, ff_chunk, final_norm):
    x = x_ref[...]
    h = _rms(x, g_ref[...]).astype(BF16)
    d_ff = wg_ref.shape[1]
    acc = jnp.zeros_like(x)
    for c0 in range(0, d_ff, ff_chunk):
        gate = _dot(h, wg_ref[:, c0:c0 + ff_chunk])
        up = _dot(h, wu_ref[:, c0:c0 + ff_chunk])
        act = (gate * _sigmoid(gate) * up).astype(BF16)
        acc = acc + _dot(act, wd_ref[c0:c0 + ff_chunk, :])
    y = x + 0.5 * acc
    if final_norm:
        y = _rms(y, fg_ref[...])
    o_ref[...] = y


def _ffn(x, norm_g, wg, wu, wd, final_g, layer, *, tm, final_norm=False):
    m, d = x.shape
    d_ff = wg.shape[2]
    ff_chunk = d_ff // 4 if (d_ff // 4) % 128 == 0 else d_ff
    body = functools.partial(_ffn_body, ff_chunk=ff_chunk, final_norm=final_norm)
    return pl.pallas_call(
        body,
        grid=(m // tm,),
        in_specs=[
            pl.BlockSpec((tm, d), lambda i: (i, 0)),
            _const_spec((None, 1, d), lambda i: (layer, 0, 0)),
            _const_spec((None, d, d_ff), lambda i: (layer, 0, 0)),
            _const_spec((None, d, d_ff), lambda i: (layer, 0, 0)),
            _const_spec((None, d_ff, d), lambda i: (layer, 0, 0)),
            _const_spec((1, d), lambda i: (0, 0)),
        ],
        out_specs=pl.BlockSpec((tm, d), lambda i: (i, 0)),
        out_shape=jax.ShapeDtypeStruct((m, d), F32),
        compiler_params=_cparams("parallel"),
        name="ffn",
    )(x, norm_g, wg, wu, wd, final_g)


def _small_fn(z, is_logf, is_beta, neg_exp_a):
    return jnp.where(is_logf, -_softplus(-z), jnp.where(is_beta, _sigmoid(z), neg_exp_a * _softplus(z)))


def _inproj_body(x_ref, g_ref, wfox_ref, wrw_ref, wgd_ref, wgate_ref, wsm_ref, wkvt_ref, pc_ref,
                 qkvb_ref, fk_ref, fv_ref, kt_ref, vt_ref, zr_ref, gqkv_ref, gz_ref, zg_ref, sm_ref):
    h = _rms(x_ref[...], g_ref[...]).astype(BF16)
    fox = _dot(h, wfox_ref[...])
    qkvb_ref[...] = fox.astype(BF16)
    fk_ref[...] = fox[:, FOX_W:2 * FOX_W]
    fv_ref[...] = fox[:, 2 * FOX_W:]
    kvt = _dot_nt(wkvt_ref[...], h)
    kt_ref[...] = kvt[:FOX_W]
    vt_ref[...] = kvt[FOX_W:]
    zr_ref[...] = _dot(h, wrw_ref[...])
    gd = _dot(h, wgd_ref[...])
    gqkv_ref[...] = gd[:, :GD_QKV]
    gz_ref[...] = gd[:, GD_QKV:]
    zg_ref[...] = _dot(h, wgate_ref[...])
    pc = pc_ref[...]
    col = lax.broadcasted_iota(jnp.int32, (1, N_SMALL), 1)
    sm = _dot(h, wsm_ref[...]) + pc[0:1]
    sm_ref[...] = _small_fn(sm, col < 8, col < 12, -jnp.exp(pc[1:2]))


def _inproj(x, norm_g, w, layer, *, tm, seq):
    m, d = x.shape
    n_gate = w["gate"].shape[2]
    n_gd = w["gd"].shape[2]
    t_batch, t_seq = (m // seq, seq) if seq % tm == 0 else (1, m)
    tiles_per_seq = t_seq // tm
    row_blk = lambda n: pl.BlockSpec((tm, n), lambda i: (i, 0))
    t_blk = pl.BlockSpec((None, FOX_W, tm), lambda i: (i // tiles_per_seq, 0, i % tiles_per_seq))
    wspec = lambda a: _const_spec((None,) + a.shape[1:], lambda i: (layer, 0, 0))
    out_shape = (
        jax.ShapeDtypeStruct((m, 3 * FOX_W), BF16),
        jax.ShapeDtypeStruct((m, FOX_W), F32),
        jax.ShapeDtypeStruct((m, FOX_W), F32),
        jax.ShapeDtypeStruct((t_batch, FOX_W, t_seq), F32),
        jax.ShapeDtypeStruct((t_batch, FOX_W, t_seq), F32),
        jax.ShapeDtypeStruct((m, RW_COLS), F32),
        jax.ShapeDtypeStruct((m, GD_QKV), F32),
        jax.ShapeDtypeStruct((m, n_gd - GD_QKV), F32),
        jax.ShapeDtypeStruct((m, n_gate), F32),
        jax.ShapeDtypeStruct((m, N_SMALL), F32),
    )
    out_specs = (
        row_blk(3 * FOX_W), row_blk(FOX_W), row_blk(FOX_W), t_blk, t_blk, row_blk(RW_COLS), row_blk(GD_QKV),
        row_blk(n_gd - GD_QKV), row_blk(n_gate), row_blk(N_SMALL),
    )
    return pl.pallas_call(
        _inproj_body,
        grid=(m // tm,),
        in_specs=[
            pl.BlockSpec((tm, d), lambda i: (i, 0)),
            _const_spec((None, 1, d), lambda i: (layer, 0, 0)),
            wspec(w["fox"]), wspec(w["rw"]), wspec(w["gd"]), wspec(w["gate"]),
            wspec(w["small"]), wspec(w["kv_t"]), wspec(w["small_pc"]),
        ],
        out_specs=out_specs,
        out_shape=out_shape,
        compiler_params=_cparams("parallel"),
        name="inproj",
    )(x, norm_g, w["fox"], w["rw"], w["gd"], w["gate"], w["small"], w["kv_t"], w["small_pc"])


FOX_AW = 2 * HEAD_DIM
N_PIECES = 3
LOG2E = 1.4426950408889634


def _eye(n):
    return (lax.broadcasted_iota(jnp.int32, (n, n), 0) == lax.broadcasted_iota(jnp.int32, (n, n), 1)).astype(BF16)


def _fox_prompt_body(q_ref, k_ref, vt_in_ref, lf_ref, pk_ref, pc_ref, one_ref, o_ref,
                     kaug_ref, vt_ref, qaug_ref, m_ref, l_ref, acc_ref, *, tq, cb):
    i = pl.program_id(1)
    seq = k_ref.shape[0]

    @pl.when(i == 0)
    def _():
        lower = _tri(cb, "incl").astype(BF16)
        carry = jnp.zeros((1, N_SMALL), F32)
        for b0 in range(0, seq, cb):
            cc = _dot_sel_l(lower, lf_ref[b0:b0 + cb, :]) + carry
            carry = cc[cb - 1:cb, :]
            pieces = jnp.concatenate(_split3(-cc * LOG2E), axis=1)
            kaug_ref[b0:b0 + cb, :] = (_dot(k_ref[b0:b0 + cb, :], pk_ref[...])
                                       + _dot(pieces, pc_ref[...])).astype(BF16)
            vt_ref[:, b0:b0 + cb] = vt_in_ref[:, b0:b0 + cb].astype(BF16)

    qaug_ref[...] = (_dot(q_ref[...], pk_ref[...]) * LOG2E + one_ref[...]).astype(BF16)
    m_ref[...] = jnp.full(m_ref.shape, -jnp.inf, F32)
    l_ref[...] = jnp.zeros(l_ref.shape, F32)
    acc_ref[...] = jnp.zeros(acc_ref.shape, F32)

    def tile(j, diagonal):
        k0 = pl.multiple_of(j * tq, tq)
        if diagonal:
            visible = _tri(tq, "upper_incl")
        scores = [_dot_nt(kaug_ref[pl.ds(k0, tq), h * FOX_AW:(h + 1) * FOX_AW],
                          qaug_ref[:, h * FOX_AW:(h + 1) * FOX_AW]) for h in range(FOX_HEADS)]
        for h in range(FOX_HEADS):
            hs = slice(h * HEAD_DIM, (h + 1) * HEAD_DIM)
            st = scores[h]
            if diagonal:
                st = jnp.where(visible, st, -jnp.inf)
            m_old = m_ref[h]
            m_new = jnp.maximum(m_old, jnp.max(st, axis=0, keepdims=True))
            alpha = jnp.exp2(m_old - m_new)
            p = jnp.exp2(st - m_new)
            m_ref[h] = m_new
            l_ref[h] = alpha * l_ref[h] + jnp.sum(p, axis=0, keepdims=True)
            acc_ref[hs, :] = alpha * acc_ref[hs, :] + _dot(vt_ref[hs, pl.ds(k0, tq)], p.astype(BF16))

    def body(j, carry):
        tile(j, False)
        return carry

    lax.fori_loop(0, i, body, 0)
    tile(i, True)
    out_t = jnp.concatenate(
        [acc_ref[h * HEAD_DIM:(h + 1) * HEAD_DIM, :] / l_ref[h] for h in range(FOX_HEADS)], axis=0)
    o_ref[...] = _dot_nt(_eye(tq), out_t.astype(BF16)).astype(BF16)


def _fox_prompt(qkvb, vt, sm, consts, batch, seq, *, tq):
    m = batch * seq
    nq = seq // tq
    cb = min(512, seq)
    body = functools.partial(_fox_prompt_body, tq=tq, cb=cb)
    aug_w = FOX_HEADS * FOX_AW
    return pl.pallas_call(
        body,
        grid=(batch, nq),
        in_specs=[
            pl.BlockSpec((tq, FOX_W), lambda b, i: (b * nq + i, 0)),
            pl.BlockSpec((seq, FOX_W), lambda b, i: (b, 1)),
            pl.BlockSpec((None, FOX_W, seq), lambda b, i: (b, 0, 0)),
            pl.BlockSpec((seq, N_SMALL), lambda b, i: (b, 0)),
            _const_spec((FOX_W, aug_w), lambda b, i: (0, 0)),
            _const_spec((N_PIECES * N_SMALL, aug_w), lambda b, i: (0, 0)),
            _const_spec((1, aug_w), lambda b, i: (0, 0)),
        ],
        out_specs=pl.BlockSpec((tq, FOX_W), lambda b, i: (b * nq + i, 0)),
        out_shape=jax.ShapeDtypeStruct((m, FOX_W), BF16),
        scratch_shapes=[pltpu.VMEM((seq, aug_w), BF16), pltpu.VMEM((FOX_W, seq), BF16),
                        pltpu.VMEM((tq, aug_w), BF16),
                        pltpu.VMEM((FOX_HEADS, 1, tq), F32), pltpu.VMEM((FOX_HEADS, 1, tq), F32),
                        pltpu.VMEM((FOX_W, tq), F32)],
        compiler_params=_cparams("parallel", "arbitrary"),
        name="fox_prompt",
    )(qkvb, qkvb, vt, sm, consts["place_k"], consts["place_c"], consts["ones"])


def _fox_prompt_consts():
    aug_w = FOX_HEADS * FOX_AW
    lane = jnp.arange(aug_w)
    src = jnp.arange(FOX_W)
    place_k = (lane[None, :] == (src // HEAD_DIM * FOX_AW + src % HEAD_DIM)[:, None]).astype(BF16)
    col = jnp.arange(N_PIECES * N_SMALL)
    tgt = jnp.where(col % N_SMALL < FOX_HEADS, (col % N_SMALL) * FOX_AW + HEAD_DIM + col // N_SMALL, -1)
    place_c = (lane[None, :] == tgt[:, None]).astype(BF16)
    ones = ((lane % FOX_AW >= HEAD_DIM) & (lane % FOX_AW < HEAD_DIM + N_PIECES)).astype(F32)[None, :]
    return {"place_k": place_k, "place_c": place_c, "ones": ones}


def _fox_sample_body(pt_ref, q_ref, kn_ref, vn_ref, lfn_ref, *rest, t_valid, pps):
    kp_refs = rest[0:pps]
    vp_refs = rest[pps:2 * pps]
    lf_refs = rest[2 * pps:3 * pps]
    o_ref, qbd_ref, m_ref, l_ref, acc_ref, tail_ref = rest[3 * pps:]
    s_idx = pl.program_id(1)
    tp = SUBLANES
    nrow = t_valid * FOX_HEADS
    page = kp_refs[0].shape[1]
    diag = (lax.broadcasted_iota(jnp.int32, (nrow, FOX_W), 0) % FOX_HEADS
            == lax.broadcasted_iota(jnp.int32, (nrow, FOX_W), 1) // HEAD_DIM)

    def expand_rows(x):
        return jnp.concatenate([x] * t_valid, axis=0)

    @pl.when(s_idx == 0)
    def _():
        qf = q_ref[...].astype(F32)
        q_rows = jnp.concatenate([jnp.broadcast_to(qf[t:t + 1], (FOX_HEADS, FOX_W)) for t in range(t_valid)],
                                 axis=0)
        qbd = jnp.where(diag, q_rows, 0.0).astype(BF16)
        qbd_ref[...] = qbd
        upper = _tri(tp, "upper_incl").astype(BF16)
        lf_hi, lf_mid, lf_lo = _split3(lfn_ref[...])
        cn = (_dot_tn(lf_hi, upper) + _dot_tn(lf_mid, upper) + _dot_tn(lf_lo, upper))[:FOX_HEADS]
        s = _dot_nt(qbd, kn_ref[...]) - expand_rows(cn)
        t_q = lax.broadcasted_iota(jnp.int32, (nrow, tp), 0) // FOX_HEADS
        u_k = lax.broadcasted_iota(jnp.int32, (nrow, tp), 1)
        s = jnp.where((u_k <= t_q) & (u_k < t_valid), s, -jnp.inf)
        m = jnp.max(s, axis=-1, keepdims=True)
        p = jnp.exp(s - m)
        m_ref[...] = m
        l_ref[...] = jnp.sum(p, axis=-1, keepdims=True)
        acc_ref[...] = _dot(p.astype(BF16), vn_ref[...])
        tail_ref[...] = jnp.zeros_like(tail_ref)

    @pl.when(s_idx > 0)
    def _():
        later3 = jnp.concatenate([_tri(page, "strict").astype(BF16)] * 3, axis=0)
        qbd = qbd_ref[...]
        tail = tail_ref[...]
        scores = [None] * pps
        for r in reversed(range(pps)):
            lf = lf_refs[r][...]
            suf = _dot(jnp.concatenate(_split3(lf), axis=1), later3) + tail
            tail = tail + jnp.sum(lf, axis=-1, keepdims=True)
            scores[r] = _dot(qbd, kp_refs[r][...].astype(BF16)) + expand_rows(suf)
        tail_ref[...] = tail
        m_old = m_ref[...]
        m_new = m_old
        for r in range(pps):
            m_new = jnp.maximum(m_new, jnp.max(scores[r], axis=-1, keepdims=True))
        alpha = jnp.exp(m_old - m_new)
        l_new = alpha * l_ref[...]
        acc = alpha * acc_ref[...]
        for r in range(pps):
            p = jnp.exp(scores[r] - m_new)
            l_new = l_new + jnp.sum(p, axis=-1, keepdims=True)
            acc = acc + _dot_nt(p.astype(BF16), vp_refs[r][...].astype(BF16))
        m_ref[...] = m_new
        l_ref[...] = l_new
        acc_ref[...] = acc

    @pl.when(s_idx == pl.num_programs(1) - 1)
    def _():
        o = jnp.where(diag, acc_ref[...] / l_ref[...], 0.0).astype(BF16)
        pick = (lax.broadcasted_iota(jnp.int32, (tp, nrow), 1) // FOX_HEADS
                == lax.broadcasted_iota(jnp.int32, (tp, nrow), 0)).astype(BF16)
        o_ref[...] = _dot(pick, o).astype(BF16)


def _fox_sample(qkvb, sm, cache_kt, cache_vt, cache_lft, page_table, layer, *, t_valid, pps):
    tp = SUBLANES
    bd, n_pages = page_table.shape
    page = cache_kt.shape[3]
    nrow = t_valid * FOX_HEADS
    assert n_pages % pps == 0

    def page_spec(r, blk):
        def index_map(b, s, pt):
            return (layer, pt[b, n_pages - jnp.maximum(s, 1) * pps + r]) + (0,) * (len(blk) - 2)
        return pl.BlockSpec(blk, index_map)

    kv_blk = (None, None, FOX_W, page)
    lf_blk = (None, None, FOX_HEADS, page)
    grid_spec = pltpu.PrefetchScalarGridSpec(
        num_scalar_prefetch=1,
        grid=(bd, n_pages // pps + 1),
        in_specs=[
            pl.BlockSpec((tp, FOX_W), lambda b, s, pt: (b, 0)),
            pl.BlockSpec((tp, FOX_W), lambda b, s, pt: (b, 1)),
            pl.BlockSpec((tp, FOX_W), lambda b, s, pt: (b, 2)),
            pl.BlockSpec((tp, N_SMALL), lambda b, s, pt: (b, 0)),
            *[page_spec(r, kv_blk) for r in range(pps)],
            *[page_spec(r, kv_blk) for r in range(pps)],
            *[page_spec(r, lf_blk) for r in range(pps)],
        ],
        out_specs=pl.BlockSpec((tp, FOX_W), lambda b, s, pt: (b, 0)),
        scratch_shapes=[
            pltpu.VMEM((nrow, FOX_W), BF16),
            pltpu.VMEM((nrow, 1), F32),
            pltpu.VMEM((nrow, 1), F32),
            pltpu.VMEM((nrow, FOX_W), F32),
            pltpu.VMEM((FOX_HEADS, 1), F32),
        ],
    )
    body = functools.partial(_fox_sample_body, t_valid=t_valid, pps=pps)
    return pl.pallas_call(
        body,
        grid_spec=grid_spec,
        out_shape=jax.ShapeDtypeStruct((bd * tp, FOX_W), BF16),
        compiler_params=_cparams("parallel", "arbitrary"),
        name="fox_sample",
    )(page_table, qkvb, qkvb, qkvb, sm, *([cache_kt] * pps), *([cache_vt] * pps), *([cache_lft] * pps))


def _rwkv_body(z_ref, shift_ref, s0_ref, mu_ref, pv_ref, wup_ref, aup_ref, gup_ref, seg_ref,
               o_ref, so_ref, sbd_ref, prev_ref, *, chunk, n_chunks, n_seq, t_valid):
    c = pl.program_id(1)
    tb = chunk * n_chunks
    rows_all = n_seq * tb
    rows_x = GROUP * chunk
    lane_mask = _head_mask(rows_x, chunk, RW_GW, HEAD_DIM)
    amask = _head_mask(rows_x, chunk, rows_x, chunk)
    state_mask = _head_mask(RW_GW, HEAD_DIM, RW_GW, HEAD_DIM)

    @pl.when(c == 0)
    def _():
        for i in range(n_seq):
            prev_ref[i, 0:1, :] = shift_ref[i]
            for g in range(RW_GROUPS):
                blocks = [jnp.concatenate([s0_ref[i, g * GROUP + h]] * GROUP, axis=1) for h in range(GROUP)]
                sbd_ref[i * RW_GROUPS + g] = jnp.where(state_mask, jnp.concatenate(blocks, axis=0), 0.0)

    z = z_ref[...].reshape(rows_all, RW_COLS)
    row = lax.broadcasted_iota(jnp.int32, (rows_all, 1), 0)
    z_prev = pltpu.roll(z, 1, axis=0)
    for i in range(n_seq):
        z_prev = jnp.where(row == i * tb, prev_ref[i, 0:1, :], z_prev)
        prev_ref[i, 0:1, :] = z[(i + 1) * tb - 1:(i + 1) * tb, :]
    zs = z + (z_prev - z) * mu_ref[...]
    r = zs[:, 0:RW_W]
    k = zs[:, RW_W:2 * RW_W]
    v = zs[:, 2 * RW_W:3 * RW_W]
    o_l = 3 * RW_W
    wd = zs[:, o_l:o_l + RW_LORA_W]
    ad = zs[:, o_l + RW_LORA_W:o_l + RW_LORA_W + RW_LORA_A]
    gd = zs[:, o_l + RW_LORA_W + RW_LORA_A:]
    pv = pv_ref[...]
    w_log = -_softplus(-(pv[0:1] + _dot(jnp.tanh(wd).astype(BF16), wup_ref[...]))) - 0.5
    lw = -jnp.exp(w_log)
    a = _sigmoid(pv[1:2] + _dot(ad.astype(BF16), aup_ref[...]))
    g_out = _dot(_sigmoid(gd).astype(BF16), gup_ref[...])
    kk = k * pv[2:3]
    k = k * (1.0 + (a - 1.0) * pv[3:4])
    seg2 = seg_ref[...]
    kk = kk * lax.rsqrt(_seg_sum(kk * kk, seg2) + L2_EPS)
    if t_valid < chunk:
        valid = row % tb < t_valid
        lw = jnp.where(valid, lw, 0.0)
        kk = jnp.where(valid, kk, 0.0)
        k = jnp.where(valid, k, 0.0)
    cs = _dot_sel_l(_block_tri(rows_all, chunk, strict=False).astype(BF16), lw)
    p_inv = jnp.exp(-cs)
    rp = (r * jnp.exp(cs)).astype(BF16)
    kkp = (kk * jnp.exp(cs - lw)).astype(BF16)
    bn = kk * a * p_inv
    kn = k * p_inv
    vb = v.astype(BF16)
    t_idx = lax.broadcasted_iota(jnp.int32, (chunk, 2 * rows_x), 0)
    j_idx = lax.broadcasted_iota(jnp.int32, (chunk, 2 * rows_x), 1) % chunk
    strict = (t_idx > j_idx)[:, :rows_x]
    incl2 = t_idx >= j_idx
    groups = range(RW_GROUPS)
    gsl = [slice(g * RW_GW, (g + 1) * RW_GW) for g in groups]
    bnb = bn.astype(BF16)
    knb = kn.astype(BF16)

    def prepare(ci, g):
        def gen(done):
            rs = slice(ci * chunk, (ci + 1) * chunk)
            bk_x = jnp.concatenate([_expand(bnb[rs, gsl[g]], lane_mask), _expand(knb[rs, gsl[g]], lane_mask)],
                                   axis=0)
            a_all = _dot_nt(jnp.concatenate([kkp[rs, gsl[g]], rp[rs, gsl[g]]], axis=0), bk_x)
            yield
            a_ak = jnp.where(strict, a_all[:chunk, rows_x:], 0.0).astype(BF16)
            a_r = jnp.where(incl2, a_all[chunk:, :], 0.0).astype(BF16)
            t_m1 = yield from _unit_lower_inverse_minus_eye(
                jnp.where(strict, a_all[:chunk, :rows_x], 0.0), chunk, amask)
            return a_ak, a_r, t_m1.astype(BF16)
        return gen

    def advance(ci):
        def gen(done):
            rs = slice(ci * chunk, (ci + 1) * chunk)
            pre = [done["prepare", ci, g] for g in groups]
            p_end = jnp.exp(cs[(ci + 1) * chunk - 1:(ci + 1) * chunk, :])
            bk_e = jnp.concatenate([(bn[rs] * p_end).astype(BF16), (kn[rs] * p_end).astype(BF16)], axis=0)
            v_xs = [_expand(vb[rs, gsl[g]], lane_mask) for g in groups]
            s_base = ci // n_chunks * RW_GROUPS
            sts = [sbd_ref[s_base + g] for g in groups]
            stbs = [st.astype(BF16) for st in sts]
            rhs = [_dot_nt(kkp[rs, gsl[g]], stbs[g]) + _dot(pre[g][0], v_xs[g]) for g in groups]
            yield
            sabs = [(-(rhs[g] + _dot(pre[g][2], _expand(rhs[g].astype(BF16), lane_mask)))).astype(BF16)
                    for g in groups]
            yield
            y_c = jnp.concatenate(
                [_dot_nt(rp[rs, gsl[g]], stbs[g])
                 + _dot(pre[g][1], jnp.concatenate([_expand(sabs[g], lane_mask), v_xs[g]], axis=0))
                 for g in groups], axis=1)
            yield
            for g in groups:
                outer = _dot_tn(jnp.concatenate([sabs[g], vb[rs, gsl[g]]], axis=0), bk_e[:, gsl[g]])
                sbd_ref[s_base + g] = sts[g] * p_end[:, gsl[g]] + jnp.where(state_mask, outer, 0.0)
            yield
            return y_c
        return gen

    tasks = []
    for ci in range(n_seq * n_chunks):
        tasks.append((("advance", ci),
                      [("prepare", ci, g) for g in groups] + ([("advance", ci - 1)] if ci % n_chunks else []),
                      advance(ci)))
        for g in groups:
            tasks.append((("prepare", ci, g), [], prepare(ci, g)))
    done = _run_tasks(tasks)
    y_rows = [done["advance", ci] for ci in range(n_seq * n_chunks)]
    y = y_rows[0] if len(y_rows) == 1 else jnp.concatenate(y_rows, axis=0)
    inv_n = 1.0 / HEAD_DIM
    mean = _seg_sum(y, seg2) * inv_n
    yc = y - mean
    var = _seg_sum(yc * yc, seg2) * inv_n
    yn = yc * lax.rsqrt(var + RW_GN_EPS) * pv[5:6] + pv[6:7]
    bonus = _seg_sum(r * k * pv[4:5], seg2) * v
    o_ref[...] = ((yn + bonus) * g_out).reshape(n_seq, tb, RW_W).astype(BF16)

    @pl.when(c == pl.num_programs(1) - 1)
    def _():
        for i in range(n_seq):
            for g in range(RW_GROUPS):
                st = sbd_ref[i * RW_GROUPS + g]
                for h in range(GROUP):
                    hs = slice(h * HEAD_DIM, (h + 1) * HEAD_DIM)
                    so_ref[i, g * GROUP + h] = st[hs, hs]


def _rwkv(zr, shift0, s0, w, layer, batch, seq, *, chunk, n_chunks, n_seq, t_valid):
    tb = chunk * n_chunks
    nc = seq // tb
    assert batch % n_seq == 0
    body = functools.partial(_rwkv_body, chunk=chunk, n_chunks=n_chunks, n_seq=n_seq, t_valid=t_valid)
    lspec = lambda a: _const_spec((None,) + a.shape[1:], lambda b, c: (layer, 0, 0))
    state_spec = pl.BlockSpec((n_seq, RW_HEADS, HEAD_DIM, HEAD_DIM), lambda b, c: (b, 0, 0, 0))
    s0_layer = layer if s0.shape[0] > 1 else 0
    o_rw, state = pl.pallas_call(
        body,
        grid=(batch // n_seq, nc),
        in_specs=[
            pl.BlockSpec((n_seq, tb, RW_COLS), lambda b, c: (b, c, 0)),
            pl.BlockSpec((n_seq, 1, RW_COLS), lambda b, c: (b, 0, 0)),
            pl.BlockSpec((None, n_seq, RW_HEADS, HEAD_DIM, HEAD_DIM), lambda b, c: (s0_layer, b, 0, 0, 0)),
            lspec(w["mu"]), lspec(w["pv"]), lspec(w["w_up"]), lspec(w["a_up"]), lspec(w["g_up"]),
            _const_spec((2 * RW_W, RW_W), lambda b, c: (0, 0)),
        ],
        out_specs=(pl.BlockSpec((n_seq, tb, RW_W), lambda b, c: (b, c, 0)), state_spec),
        out_shape=(
            jax.ShapeDtypeStruct((batch, seq, RW_W), BF16),
            jax.ShapeDtypeStruct((batch, RW_HEADS, HEAD_DIM, HEAD_DIM), F32),
        ),
        scratch_shapes=[pltpu.VMEM((n_seq * RW_GROUPS, RW_GW, RW_GW), F32),
                        pltpu.VMEM((n_seq, SUBLANES, RW_COLS), F32)],
        compiler_params=_cparams("parallel", "arbitrary"),
        name="rwkv",
    )(zr.reshape(batch, seq, RW_COLS), shift0, s0, w["mu"], w["pv"], w["w_up"], w["a_up"], w["g_up"], w["seg2"])
    return o_rw.reshape(batch * seq, RW_W), state


def _gdn_body(x_ref, gz_ref, sm_ref, conv0_ref, s0_ref, cw_ref, ng_ref, seg_ref,
              o_ref, so_ref, prev_ref, *, chunk, n_chunks, n_seq, t_valid):
    c = pl.program_id(1)
    tb = chunk * n_chunks
    rows_all = n_seq * tb
    rows_x = GD_HEADS * chunk
    lane_mask = _head_mask(rows_x, chunk, GD_W, GD_DK)
    amask = _head_mask(rows_x, chunk, rows_x, chunk)

    @pl.when(c == 0)
    def _():
        prev_ref[...] = conv0_ref[...]
        so_ref[...] = s0_ref[...]

    x = x_ref[...].reshape(rows_all, GD_QKV)
    prevs = [prev_ref[i] for i in range(n_seq)]
    for i in range(n_seq):
        prev_ref[i] = x[(i + 1) * tb - SUBLANES:(i + 1) * tb, :]
    cw = cw_ref[...]
    row8 = lax.broadcasted_iota(jnp.int32, (SUBLANES, 1), 0)
    conv = x * cw[GD_CONV - 1:GD_CONV]
    for sft in range(1, GD_CONV):
        xr = pltpu.roll(x, sft, axis=0)
        parts = []
        for i in range(n_seq):
            parts.append(jnp.where(row8 < sft, pltpu.roll(prevs[i], sft, axis=0), xr[i * tb:i * tb + SUBLANES]))
            if tb > SUBLANES:
                parts.append(xr[i * tb + SUBLANES:(i + 1) * tb])
        xs = parts[0] if len(parts) == 1 else jnp.concatenate(parts, axis=0)
        conv = conv + xs * cw[GD_CONV - 1 - sft:GD_CONV - sft]
    act = conv * _sigmoid(conv)
    seg2 = seg_ref[...]
    q = act[:, :GD_W]
    k = act[:, GD_W:2 * GD_W]
    v = act[:, 2 * GD_W:]
    q = q * lax.rsqrt(_seg_sum(q * q, seg2) + L2_EPS) * (GD_DK ** -0.5)
    k = k * lax.rsqrt(_seg_sum(k * k, seg2) + L2_EPS)
    sm = sm_ref[...].reshape(rows_all, N_SMALL)
    beta = sm[:, 8:8 + GD_HEADS]
    g = sm[:, 12:12 + GD_HEADS]
    if t_valid < chunk:
        rowc = lax.broadcasted_iota(jnp.int32, (rows_all, 1), 0) % tb
        beta = jnp.where(rowc < t_valid, beta, 0.0)
        g = jnp.where(rowc < t_valid, g, 0.0)
    gc = _dot_sel_l(_block_tri(rows_all, chunk, strict=False).astype(BF16), g)
    head_lanes = _head_mask(GD_HEADS, 1, GD_W, GD_DK).astype(BF16)
    beta_l = _dot_sel_r(beta, head_lanes)
    gc_l = _dot_sel_r(gc, head_lanes)
    egc_l = jnp.exp(gc_l)
    kb = k * beta_l
    vbeta = (v * beta_l).astype(BF16)
    kbg = (kb * egc_l).astype(BF16)
    qg = (q * egc_l).astype(BF16)
    kb = kb.astype(BF16)
    qb = q.astype(BF16)
    kbf = k.astype(BF16)
    t_idx = lax.broadcasted_iota(jnp.int32, (chunk, rows_x), 0)
    s_idx = lax.broadcasted_iota(jnp.int32, (chunk, rows_x), 1) % chunk
    strict = t_idx > s_idx
    incl = t_idx >= s_idx
    diag = t_idx == s_idx
    head_cols = _head_mask(GD_HEADS, 1, rows_x, chunk).astype(BF16)
    ones_cc = jnp.ones((chunk, chunk), BF16)
    hsl = [slice(h * GD_DK, (h + 1) * GD_DK) for h in range(GD_HEADS)]

    def prepare(ci):
        def gen(done):
            rs = slice(ci * chunk, (ci + 1) * chunk)
            g_t = _dot_sel_r(gc[rs], head_cols)
            yield
            g_s = _dot_sel_l(ones_cc, jnp.where(diag, g_t, 0.0))
            qk = _dot_nt(jnp.concatenate([kb[rs], qb[rs]], axis=0), _expand(kbf[rs], lane_mask))
            yield
            decay = jnp.exp(jnp.minimum(g_t - g_s, 0.0))
            amat = jnp.where(incl, qk[chunk:] * decay, 0.0).astype(BF16)
            t_m1 = yield from _unit_lower_inverse_minus_eye(
                jnp.where(strict, qk[:chunk] * decay, 0.0), chunk, amask)
            vk = jnp.concatenate([vbeta[rs], kbg[rs]], axis=1)
            vk_x = jnp.concatenate([_expand(vbeta[rs], lane_mask), _expand(kbg[rs], lane_mask)], axis=1)
            uw = vk.astype(F32) + _dot(t_m1.astype(BF16), vk_x)
            yield
            return amat, uw
        return gen

    def advance(ci):
        def gen(done):
            rs = slice(ci * chunk, (ci + 1) * chunk)
            amat, uw = done["prepare", ci]
            glast_l = gc_l[(ci + 1) * chunk - 1:(ci + 1) * chunk]
            kg = (k[rs] * jnp.exp(glast_l - gc_l[rs])).astype(BF16)
            wb = uw[:, GD_W:].astype(BF16)
            i_seq = ci // n_chunks
            sts = [so_ref[i_seq, h] for h in range(GD_HEADS)]
            stbs = [st.astype(BF16) for st in sts]
            v_news = [uw[:, hs] - _dot(wb[:, hs], stb) for hs, stb in zip(hsl, stbs)]
            yield
            vnb = jnp.concatenate(v_news, axis=1).astype(BF16)
            for hs, st, h in zip(hsl, sts, range(GD_HEADS)):
                so_ref[i_seq, h] = st * jnp.exp(glast_l[:, hs]) + _dot_tn(kg[:, hs], vnb[:, hs])
            yield
            o_c = (jnp.concatenate([_dot(qg[rs, hs], stb) for hs, stb in zip(hsl, stbs)], axis=1)
                   + _dot(amat, _expand(vnb, lane_mask)))
            yield
            return o_c
        return gen

    tasks = []
    for ci in range(n_seq * n_chunks):
        tasks.append((("advance", ci), [("prepare", ci)] + ([("advance", ci - 1)] if ci % n_chunks else []),
                      advance(ci)))
        tasks.append((("prepare", ci), [], prepare(ci)))
    done = _run_tasks(tasks)
    o_rows = [done["advance", ci] for ci in range(n_seq * n_chunks)]
    o = o_rows[0] if len(o_rows) == 1 else jnp.concatenate(o_rows, axis=0)
    o = o * lax.rsqrt(_seg_sum(o * o, seg2) * (1.0 / GD_DV) + EPS) * ng_ref[...]
    gz = gz_ref[...].reshape(rows_all, GD_HEADS * GD_DV)
    o_ref[...] = (o * (gz * _sigmoid(gz))).reshape(n_seq, tb, GD_HEADS * GD_DV).astype(BF16)


def _gdn(gqkv, gz, sm, conv0, s0, w, layer, batch, seq, *, chunk, n_chunks, n_seq, t_valid):
    tb = chunk * n_chunks
    nc = seq // tb
    assert batch % n_seq == 0
    body = functools.partial(_gdn_body, chunk=chunk, n_chunks=n_chunks, n_seq=n_seq, t_valid=t_valid)
    lspec = lambda a: _const_spec((None,) + a.shape[1:], lambda b, c: (layer, 0, 0))
    state_spec = pl.BlockSpec((n_seq, GD_HEADS, GD_DK, GD_DV), lambda b, c: (b, 0, 0, 0))
    s0_layer = layer if s0.shape[0] > 1 else 0
    seq_blk = lambda n: pl.BlockSpec((n_seq, tb, n), lambda b, c: (b, c, 0))
    d_v = GD_HEADS * GD_DV
    o_gd, state = pl.pallas_call(
        body,
        grid=(batch // n_seq, nc),
        in_specs=[
            seq_blk(GD_QKV), seq_blk(d_v), seq_blk(N_SMALL),
            pl.BlockSpec((n_seq, SUBLANES, GD_QKV), lambda b, c: (b, 0, 0)),
            pl.BlockSpec((None, n_seq, GD_HEADS, GD_DK, GD_DV), lambda b, c: (s0_layer, b, 0, 0, 0)),
            lspec(w["conv"]), lspec(w["norm_g"]),
            _const_spec((2 * GD_W, GD_W), lambda b, c: (0, 0)),
        ],
        out_specs=(seq_blk(d_v), state_spec),
        out_shape=(
            jax.ShapeDtypeStruct((batch, seq, d_v), BF16),
            jax.ShapeDtypeStruct((batch, GD_HEADS, GD_DK, GD_DV), F32),
        ),
        scratch_shapes=[pltpu.VMEM((n_seq, SUBLANES, GD_QKV), F32)],
        compiler_params=_cparams("parallel", "arbitrary"),
        name="gdn",
    )(gqkv.reshape(batch, seq, GD_QKV), gz.reshape(batch, seq, d_v), sm.reshape(batch, seq, N_SMALL),
      conv0, s0, w["conv"], w["norm_g"], w["seg2"])
    return o_gd.reshape(batch * seq, d_v), state


def _merge_body(x_ref, of_ref, or_ref, og_ref, zg_ref, wf_ref, wr_ref, wg_ref, wo_ref, o_ref):
    d = x_ref.shape[1]
    merged = _sigmoid(zg_ref[:, 0:d]) * _dot(of_ref[...], wf_ref[...])
    merged = merged + _sigmoid(zg_ref[:, d:2 * d]) * _dot(or_ref[...], wr_ref[...])
    merged = merged + _sigmoid(zg_ref[:, 2 * d:3 * d]) * _dot(og_ref[...], wg_ref[...])
    o_ref[...] = x_ref[...] + _dot(merged.astype(BF16), wo_ref[...])


def _merge(x, o_fox, o_rw, o_gd, zg, w, layer, *, tm):
    m, d = x.shape
    row_blk = lambda n: pl.BlockSpec((tm, n), lambda i: (i, 0))
    wspec = lambda a: _const_spec((None,) + a.shape[1:], lambda i: (layer, 0, 0))
    return pl.pallas_call(
        _merge_body,
        grid=(m // tm,),
        in_specs=[row_blk(d), row_blk(FOX_W), row_blk(RW_W), row_blk(GD_HEADS * GD_DV),
                  row_blk(N_BRANCH * d),
                  wspec(w["br_fox"]), wspec(w["br_rw"]), wspec(w["br_gd"]), wspec(w["out"])],
        out_specs=row_blk(d),
        out_shape=jax.ShapeDtypeStruct((m, d), F32),
        compiler_params=_cparams("parallel"),
        name="merge",
    )(x, o_fox, o_rw, o_gd, zg, w["br_fox"], w["br_rw"], w["br_gd"], w["out"])


def _same_segment2(n, width):
    idx = jnp.arange(n) // width
    seg = (idx[:, None] == idx[None, :]).astype(BF16)
    return jnp.concatenate([seg, seg], axis=0)


def _row_tile(m):
    for tm in (512, 256, 128, 64, 32, 16, 8):
        if m % tm == 0:
            return tm
    raise ValueError(f"row count {m} is not a multiple of 8")


def _pages_per_step(n_pages):
    for pps in (16, 8, 4, 2, 1):
        if n_pages % pps == 0:
            return pps


def kernel(x_prompt, x_sample, cache_k, cache_v, cache_logf, state_rwkv_shift, state_rwkv,
           state_gdn_conv, state_gdn, page_table, norm_ffn1, ffn1_wg, ffn1_wu, ffn1_wd, norm_mix,
           w_in, fox_fb, rw_mu, rw_w0, rw_w_up, rw_a0, rw_a_up, rw_g_up, rw_kk, rw_ka, rw_rk,
           rw_ln_g, rw_ln_b, gd_conv, gd_a_log, gd_dt_bias, gd_norm_g, w_br_fox, w_br_rw, w_br_gd,
           w_out, norm_ffn2, ffn2_wg, ffn2_wu, ffn2_wd, final_norm):
    bp, seq, d = x_prompt.shape
    bd, t_s, _ = x_sample.shape
    depth = w_in.shape[0]
    tp = SUBLANES
    chunk_p = 64
    n_chunks_p = 4 if seq % (4 * chunk_p) == 0 else 1
    n_seq_p = 2 if bp % 2 == 0 else 1
    n_seq_s = 8 if bd % 8 == 0 else 1
    assert t_s <= tp and seq % chunk_p == 0

    o_rw_c = 3 * FOX_W + FOX_HEADS
    o_gd_c = o_rw_c + RW_COLS
    o_gz = o_gd_c + GD_QKV
    o_gb = o_gz + GD_HEADS * GD_DV
    o_ga = o_gb + GD_HEADS
    o_gate = o_ga + GD_HEADS
    scale = HEAD_DIM ** -0.5
    w_small = jnp.concatenate([w_in[:, :, 3 * FOX_W:o_rw_c], w_in[:, :, o_gb:o_gate]], axis=-1)
    zeros4 = jnp.zeros((depth, GD_HEADS), F32)
    small_bias = jnp.concatenate([fox_fb, zeros4, gd_dt_bias], axis=-1)
    small_alog = jnp.concatenate([jnp.zeros((depth, 8), F32), zeros4, gd_a_log], axis=-1)
    w_proj = {
        "fox": jnp.concatenate([w_in[:, :, :FOX_W] * scale, w_in[:, :, FOX_W:3 * FOX_W]], axis=-1).astype(BF16),
        "rw": w_in[:, :, o_rw_c:o_gd_c].astype(BF16),
        "gd": w_in[:, :, o_gd_c:o_gb].astype(BF16),
        "gate": w_in[:, :, o_gate:].astype(BF16),
        "small": w_small.astype(BF16),
        "kv_t": jnp.swapaxes(w_in[:, :, FOX_W:3 * FOX_W], 1, 2).astype(BF16),
        "small_pc": jnp.stack([small_bias, small_alog], axis=1),
    }
    w_rw = {
        "mu": rw_mu[:, None, :],
        "pv": jnp.stack([rw_w0, rw_a0, rw_kk, rw_ka, rw_rk.reshape(depth, RW_W), rw_ln_g, rw_ln_b,
                         jnp.zeros_like(rw_w0)], axis=1),
        "w_up": rw_w_up.astype(BF16), "a_up": rw_a_up.astype(BF16), "g_up": rw_g_up.astype(BF16),
        "seg2": _same_segment2(RW_W, HEAD_DIM),
    }
    w_gd = {
        "conv": jnp.pad(gd_conv, ((0, 0), (0, SUBLANES - GD_CONV), (0, 0))),
        "norm_g": jnp.tile(gd_norm_g, (1, GD_HEADS))[:, None, :],
        "seg2": _same_segment2(GD_W, GD_DK),
    }
    w_mg = {"br_fox": w_br_fox.astype(BF16), "br_rw": w_br_rw.astype(BF16),
            "br_gd": w_br_gd.astype(BF16), "out": w_out.astype(BF16)}
    ffn1 = (norm_ffn1[:, None, :], ffn1_wg.astype(BF16), ffn1_wu.astype(BF16), ffn1_wd.astype(BF16))
    ffn2 = (norm_ffn2[:, None, :], ffn2_wg.astype(BF16), ffn2_wu.astype(BF16), ffn2_wd.astype(BF16))
    norm_mix3 = norm_mix[:, None, :]
    final_g = final_norm[None, :]
    fox_consts = _fox_prompt_consts()

    n_pool, page = cache_k.shape[1], cache_k.shape[2]
    ckt = jnp.transpose(cache_k, (0, 1, 3, 4, 2)).reshape(depth, n_pool, FOX_W, page)
    cvt = jnp.transpose(cache_v, (0, 1, 3, 4, 2)).reshape(depth, n_pool, FOX_W, page)
    clft = jnp.swapaxes(cache_logf, 2, 3)
    pps = _pages_per_step(page_table.shape[1])
    gd_conv0_s = jnp.pad(state_gdn_conv, ((0, 0), (0, 0), (SUBLANES - (GD_CONV - 1), 0), (0, 0)))
    zeros_p = {
        "shift": jnp.zeros((bp, 1, RW_COLS), F32),
        "rw_s": jnp.zeros((1, bp, RW_HEADS, HEAD_DIM, HEAD_DIM), F32),
        "conv": jnp.zeros((bp, SUBLANES, GD_QKV), F32),
        "gd_s": jnp.zeros((1, bp, GD_HEADS, GD_DK, GD_DV), F32),
    }

    xp = x_prompt.reshape(bp * seq, d)
    xs = jnp.pad(x_sample, ((0, 0), (0, tp - t_s), (0, 0))).reshape(bd * tp, d)
    tm_p, tm_s = _row_tile(bp * seq), _row_tile(bd * tp)
    tm_in_p = min(tm_p, 256)
    tq = 512 if seq % 512 == 0 else min(256, seq)

    def layer_fn(x, l, *, batch, t_len, tm, tm_in, fox_fn, shift0, rw_s0, conv0, gd_s0, chunk, n_chunks,
                 n_seq, t_valid, last):
        x = _ffn(x, *ffn1, final_g, l, tm=tm)
        qkvb, fk, fv, kt, vt, zr, gqkv, gz, zg, sm = _inproj(x, norm_mix3, w_proj, l, tm=tm_in, seq=t_len)
        o_fox = fox_fn(qkvb, vt, sm, l)
        o_rw, rw_st = _rwkv(zr, shift0, rw_s0, w_rw, l, batch, t_len, chunk=chunk, n_chunks=n_chunks,
                            n_seq=n_seq, t_valid=t_valid)
        o_gd, gd_s = _gdn(gqkv, gz, sm, conv0, gd_s0, w_gd, l, batch, t_len, chunk=chunk, n_chunks=n_chunks,
                          n_seq=n_seq, t_valid=t_valid)
        x = _merge(x, o_fox, o_rw, o_gd, zg, w_mg, l, tm=tm)
        x = _ffn(x, *ffn2, final_g, l, tm=tm, final_norm=last)
        return x, (fk, fv, kt, vt, sm, zr, rw_st, gqkv, gd_s)

    p_states, s_states = [], []
    for l in range(depth):
        last = l == depth - 1
        xp, st = layer_fn(
            xp, l, batch=bp, t_len=seq, tm=tm_p, tm_in=tm_in_p,
            fox_fn=lambda qkvb, vt, sm, l: _fox_prompt(qkvb, vt, sm, fox_consts, bp, seq, tq=tq),
            shift0=zeros_p["shift"], rw_s0=zeros_p["rw_s"], conv0=zeros_p["conv"], gd_s0=zeros_p["gd_s"],
            chunk=chunk_p, n_chunks=n_chunks_p, n_seq=n_seq_p, t_valid=chunk_p, last=last)
        _, _, kt, vt, sm, zr, rw_st, gqkv, gd_s = st
        p_states.append((
            kt, vt,
            sm[:, :FOX_HEADS].reshape(bp, seq, FOX_HEADS),
            zr.reshape(bp, seq, RW_COLS)[:, seq - 1],
            rw_st,
            gqkv.reshape(bp, seq, GD_QKV)[:, seq - (GD_CONV - 1):],
            gd_s))
        xs, st = layer_fn(
            xs, l, batch=bd, t_len=tp, tm=tm_s, tm_in=tm_s,
            fox_fn=lambda qkvb, vt, sm, l: _fox_sample(qkvb, sm, ckt, cvt, clft, page_table, l,
                                                       t_valid=t_s, pps=pps),
            shift0=state_rwkv_shift[l][:, None, :], rw_s0=state_rwkv, conv0=gd_conv0_s[l],
            gd_s0=state_gdn, chunk=tp, n_chunks=1, n_seq=n_seq_s, t_valid=t_s, last=last)
        fk, fv, _, _, sm, zr, rw_st, gqkv, gd_s = st
        conv_ext = jnp.concatenate([state_gdn_conv[l], gqkv.reshape(bd, tp, GD_QKV)[:, :t_s]], axis=1)
        s_states.append((
            fk.reshape(bd, tp, FOX_HEADS, HEAD_DIM)[:, :t_s], fv.reshape(bd, tp, FOX_HEADS, HEAD_DIM)[:, :t_s],
            sm[:, :FOX_HEADS].reshape(bd, tp, FOX_HEADS)[:, :t_s],
            zr.reshape(bd, tp, RW_COLS)[:, t_s - 1],
            rw_st,
            conv_ext[:, t_s:],
            gd_s))

    p_out = [jnp.stack(s) for s in zip(*p_states)]
    for i in (0, 1):
        p_out[i] = jnp.transpose(p_out[i].reshape(depth, bp, FOX_HEADS, HEAD_DIM, seq), (0, 1, 4, 2, 3))
    s_out = [jnp.stack(s) for s in zip(*s_states)]
    y_prompt = xp.reshape(bp, seq, d)
    y_sample = xs.reshape(bd, tp, d)[:, :t_s]
    return (y_prompt, y_sample, *p_out, *s_out)
```
